```python
import math
import jax, jax.numpy as jnp
from jax import lax
import numpy as np

D_MODEL = 2048
BATCH = 4
SEQ = 4096
DEPTH = 4

HEAD_DIM = 64
HEADS_PER_GROUP = 8
ATTN_GROUPS = ((128, 1), (512, 4), (2048, 16))
N_ATTN_HEADS = HEADS_PER_GROUP * len(ATTN_GROUPS)
ATTN_QW = N_ATTN_HEADS * HEAD_DIM
ATTN_OUT_W = HEADS_PER_GROUP * HEAD_DIM
N_KEYS = ATTN_GROUPS[0][0] // ATTN_GROUPS[0][1] + 1
W_MAX = max(w for w, _ in ATTN_GROUPS)
Q_BLOCK = 128
NUM_BUCKETS = 32
MAX_DISTANCE = W_MAX
ATTN_SCALE = HEAD_DIM ** -0.5
NEG_INF = -1e30

SSM_WIDTH = 1024
SSM_GROUP = 16
SSM_N_GROUPS = SSM_WIDTH // SSM_GROUP
SSM_STATE = 64
DT_MIN = 1e-3
DT_MAX = 1e-1

CONV_WIDTH = 1024
CONV_K = 31

N_BRANCHES = 3
D_FF = -(-8 * D_MODEL // (3 * 256)) * 256
EPS = 1e-6

IN_SPLITS = tuple(int(v) for v in np.cumsum([ATTN_QW, ATTN_QW, ATTN_QW, SSM_WIDTH, CONV_WIDTH, CONV_WIDTH]))
D_IN = IN_SPLITS[-1] + N_BRANCHES * D_MODEL

kernel_name = 'hybrid_gated_dilated_s5_conv_block'


def _rmsnorm(x, g):
    xf = x.astype(jnp.float32)
    y = xf * lax.rsqrt(jnp.mean(xf * xf, axis=-1, keepdims=True) + EPS)
    return (y * g.astype(jnp.float32)).astype(x.dtype)


def _layernorm(x, g, b):
    xf = x.astype(jnp.float32)
    mu = jnp.mean(xf, axis=-1, keepdims=True)
    var = jnp.mean(jnp.square(xf - mu), axis=-1, keepdims=True)
    y = (xf - mu) * lax.rsqrt(var + EPS)
    return (y * g.astype(jnp.float32) + b.astype(jnp.float32)).astype(x.dtype)


def _t5_bucket_causal(dist):
    max_exact = NUM_BUCKETS // 2
    d = np.maximum(dist, 1).astype(np.float32)
    large = max_exact + (np.log(d / max_exact) / math.log(MAX_DISTANCE / max_exact)
                         * (NUM_BUCKETS - max_exact)).astype(np.int32)
    large = np.minimum(large, NUM_BUCKETS - 1)
    return np.where(dist < max_exact, dist, large).astype(np.int32)


def _dilated_attention(q, k, v, rel_bias):
    b_, s_ = q.shape[0], q.shape[1]
    pad = ((0, 0), (W_MAX, 0), (0, 0), (0, 0))
    kp = jnp.pad(k, pad)
    vp = jnp.pad(v, pad)
    q_pos = np.arange(Q_BLOCK)[:, None]
    k_off = np.arange(N_KEYS)[None, :]
    groups = []
    for g, (w, d) in enumerate(ATTN_GROUPS):
        hs = slice(g * HEADS_PER_GROUP, (g + 1) * HEADS_PER_GROUP)
        local_idx = q_pos + w - k_off * d
        rel = q_pos - k_off * d
        bias = rel_bias[_t5_bucket_causal(np.arange(N_KEYS) * d)][:, hs].T.astype(jnp.float32)
        groups.append((w, hs, kp[:, :, hs], vp[:, :, hs], local_idx, rel, bias))

    def block(bi):
        q0 = bi * Q_BLOCK
        qb = lax.dynamic_slice_in_dim(q, q0, Q_BLOCK, axis=1)
        outs, lses = [], []
        for w, hs, kpg, vpg, local_idx, rel, bias in groups:
            start = q0 + W_MAX - w
            kw = lax.dynamic_slice_in_dim(kpg, start, w + Q_BLOCK, axis=1)
            vw = lax.dynamic_slice_in_dim(vpg, start, w + Q_BLOCK, axis=1)
            kg = jnp.take(kw, local_idx, axis=1)
            vg = jnp.take(vw, local_idx, axis=1)
            logits = (jnp.einsum('bqhd,bqkhd->bhqk', qb[:, :, hs], kg).astype(jnp.float32) * ATTN_SCALE
                      + bias[None, :, None, :])
            valid = (q0 + rel) >= 0
            logits = jnp.where(valid[None, None], logits, NEG_INF)
            m = jnp.max(logits, axis=-1, keepdims=True)
            p = jnp.exp(logits - m)
            den = jnp.sum(p, axis=-1, keepdims=True)
            outs.append(jnp.einsum('bhqk,bqkhd->bhqd', p / den, vg.astype(jnp.float32)))
            lses.append(m + jnp.log(den))
        wts = jax.nn.softmax(jnp.stack(lses), axis=0)
        o = jnp.sum(wts * jnp.stack(outs), axis=0)
        return o.transpose(0, 2, 1, 3).astype(q.dtype)

    ob = lax.map(block, jnp.arange(s_ // Q_BLOCK))
    return ob.transpose(1, 0, 2, 3, 4).reshape(b_, s_, ATTN_OUT_W)


def _s5_ssm(u, lam_re, lam_im, log_dt, b_re, b_im, c_re, c_im, d_skip):
    f32 = jnp.float32
    b_, s_ = u.shape[0], u.shape[1]
    uf = u.astype(f32).reshape(b_, s_, SSM_N_GROUPS, SSM_GROUP)
    dt = jnp.exp(log_dt.astype(f32))[:, None]
    lr, li = lam_re.astype(f32), lam_im.astype(f32)
    mag = jnp.exp(lr * dt)
    ab_re, ab_im = mag * jnp.cos(li * dt), mag * jnp.sin(li * dt)
    nr, ni = ab_re - 1.0, ab_im
    den = lr * lr + li * li
    z_re, z_im = (nr * lr + ni * li) / den, (ni * lr - nr * li) / den
    br, bi = b_re.astype(f32), b_im.astype(f32)
    bb_re = z_re[..., None] * br - z_im[..., None] * bi
    bb_im = z_re[..., None] * bi + z_im[..., None] * br
    bu_re = jnp.einsum('bsgc,gpc->sbgp', uf, bb_re)
    bu_im = jnp.einsum('bsgc,gpc->sbgp', uf, bb_im)
    a_re = jnp.broadcast_to(ab_re[None, None], (s_, 1, SSM_N_GROUPS, SSM_STATE))
    a_im = jnp.broadcast_to(ab_im[None, None], (s_, 1, SSM_N_GROUPS, SSM_STATE))

    def combine(e1, e2):
        a1r, a1i, b1r, b1i = e1
        a2r, a2i, b2r, b2i = e2
        return (a1r * a2r - a1i * a2i, a1r * a2i + a1i * a2r,
                a2r * b1r - a2i * b1i + b2r, a2r * b1i + a2i * b1r + b2i)

    _, _, xr, xi = lax.associative_scan(combine, (a_re, a_im, bu_re, bu_im), axis=0)
    y = (jnp.einsum('sbgp,gcp->bsgc', xr, c_re.astype(f32))
         - jnp.einsum('sbgp,gcp->bsgc', xi, c_im.astype(f32)))
    y = y + d_skip.astype(f32).reshape(SSM_N_GROUPS, SSM_GROUP) * uf
    return y.reshape(b_, s_, SSM_WIDTH).astype(u.dtype)


def _conformer_conv(val, gate, conv_w, conv_b, ln_g, ln_b, w_out):
    c = val * jax.nn.sigmoid(gate)
    c = jnp.pad(c, ((0, 0), (CONV_K - 1, 0), (0, 0)))
    c = lax.conv_general_dilated(c, conv_w[:, None, :].astype(c.dtype), window_strides=(1,), padding='VALID',
                                 dimension_numbers=('NWC', 'WIO', 'NWC'),
                                 feature_group_count=CONV_WIDTH) + conv_b
    c = jax.nn.silu(_layernorm(c, ln_g, ln_b))
    return c @ w_out


def setup_inputs(seed: int = 0) -> dict:
    key = jax.random.key(seed)
    ks = jax.random.split(key, 32)
    nrm = lambda k, shape, s: jax.random.normal(k, shape, jnp.float32) * s
    L, G, P = DEPTH, SSM_N_GROUPS, SSM_STATE
    n_idx = jnp.arange(P, dtype=jnp.float32)
    return {
        'x': nrm(ks[0], (BATCH, SEQ, D_MODEL), 1.0),
        'w_in': nrm(ks[1], (L, D_MODEL, D_IN), D_MODEL ** -0.5),
        'rel_bias': nrm(ks[2], (NUM_BUCKETS, N_ATTN_HEADS), 0.5),
        'lam_re': -0.5 + nrm(ks[3], (L, G, P), 0.01),
        'lam_im': math.pi * n_idx + nrm(ks[4], (L, G, P), 0.01),
        'log_dt': jax.random.uniform(ks[5], (L, G), jnp.float32, math.log(DT_MIN), math.log(DT_MAX)),
        'b_re': nrm(ks[6], (L, G, P, SSM_GROUP), (2 * SSM_GROUP) ** -0.5),
        'b_im': nrm(ks[7], (L, G, P, SSM_GROUP), (2 * SSM_GROUP) ** -0.5),
        'c_re': nrm(ks[8], (L, G, SSM_GROUP, P), (2 * P) ** -0.5 * 4.0),
        'c_im': nrm(ks[9], (L, G, SSM_GROUP, P), (2 * P) ** -0.5 * 4.0),
        'd_skip': nrm(ks[10], (L, SSM_WIDTH), 1.0),
        'w_ssm_glu_a': nrm(ks[11], (L, SSM_WIDTH, D_MODEL), SSM_WIDTH ** -0.5),
        'w_ssm_glu_b': nrm(ks[12], (L, SSM_WIDTH, D_MODEL), SSM_WIDTH ** -0.5),
        'conv_w': nrm(ks[13], (L, CONV_K, CONV_WIDTH), CONV_K ** -0.5),
        'conv_b': nrm(ks[14], (L, CONV_WIDTH), 0.01),
        'conv_ln_g': 1.0 + nrm(ks[15], (L, CONV_WIDTH), 0.01),
        'conv_ln_b': nrm(ks[16], (L, CONV_WIDTH), 0.01),
        'w_conv_out': nrm(ks[17], (L, CONV_WIDTH, D_MODEL), CONV_WIDTH ** -0.5),
        'w_attn_out': nrm(ks[18], (L, ATTN_OUT_W, D_MODEL), ATTN_OUT_W ** -0.5),
        'w_out': nrm(ks[19], (L, D_MODEL, D_MODEL), D_MODEL ** -0.5),
        'norm_mix_g': 1.0 + nrm(ks[20], (L, D_MODEL), 0.01),
        'norm_ffn_g': 1.0 + nrm(ks[21], (L, D_MODEL), 0.01),
        'w_ffn_gate': nrm(ks[22], (L, D_MODEL, D_FF), D_MODEL ** -0.5),
        'w_ffn_up': nrm(ks[23], (L, D_MODEL, D_FF), D_MODEL ** -0.5),
        'w_ffn_down': nrm(ks[24], (L, D_FF, D_MODEL), D_FF ** -0.5),
        'norm_final_g': 1.0 + nrm(ks[25], (D_MODEL,), 0.01),
    }


def reference(x, w_in, rel_bias, lam_re, lam_im, log_dt, b_re, b_im, c_re, c_im, d_skip,
              w_ssm_glu_a, w_ssm_glu_b, conv_w, conv_b, conv_ln_g, conv_ln_b, w_conv_out,
              w_attn_out, w_out, norm_mix_g, norm_ffn_g, w_ffn_gate, w_ffn_up, w_ffn_down,
              norm_final_g):
    b_, s_ = x.shape[0], x.shape[1]
    for l in range(DEPTH):
        h = _rmsnorm(x, norm_mix_g[l])
        z = h @ w_in[l]
        q, k, v, u, c_val, c_gate, gl = jnp.split(z, IN_SPLITS, axis=-1)
        hd = (b_, s_, N_ATTN_HEADS, HEAD_DIM)
        o_a = _dilated_attention(q.reshape(hd), k.reshape(hd), v.reshape(hd), rel_bias) @ w_attn_out[l]
        y = jax.nn.gelu(_s5_ssm(u, lam_re[l], lam_im[l], log_dt[l], b_re[l], b_im[l],
                                c_re[l], c_im[l], d_skip[l]))
        o_s = (y @ w_ssm_glu_a[l]) * jax.nn.sigmoid(y @ w_ssm_glu_b[l])
        o_c = _conformer_conv(c_val, c_gate, conv_w[l], conv_b[l], conv_ln_g[l], conv_ln_b[l], w_conv_out[l])
        g = jax.nn.sigmoid(gl.reshape(b_, s_, N_BRANCHES, D_MODEL))
        merged = g[:, :, 0] * o_a + g[:, :, 1] * o_s + g[:, :, 2] * o_c
        x = x + merged @ w_out[l]
        h = _rmsnorm(x, norm_ffn_g[l])
        x = x + (jax.nn.silu(h @ w_ffn_gate[l]) * (h @ w_ffn_up[l])) @ w_ffn_down[l]
    return _rmsnorm(x, norm_final_g)
```

```python
import functools
import math

import jax
import jax.numpy as jnp
import numpy as np
from jax import lax
from jax.experimental import pallas as pl
from jax.experimental.pallas import tpu as pltpu

F32 = jnp.float32
BF16 = jnp.bfloat16

D_MODEL = 2048
DEPTH = 4
HEAD_DIM = 64
HEADS_PER_GROUP = 8
ATTN_GROUPS = ((128, 1), (512, 4), (2048, 16))
N_GROUPS = len(ATTN_GROUPS)
N_ATTN_HEADS = HEADS_PER_GROUP * N_GROUPS
ATTN_QW = N_ATTN_HEADS * HEAD_DIM
ATTN_OUT_W = HEADS_PER_GROUP * HEAD_DIM
ATTN_LAGS = ATTN_GROUPS[0][0] // ATTN_GROUPS[0][1]
NUM_BUCKETS = 32
MAX_DISTANCE = max(w for w, _ in ATTN_GROUPS)
ATTN_SCALE = HEAD_DIM ** -0.5
NEG_INF = -1e30
SSM_WIDTH = 1024
SSM_GROUP = 16
SSM_STATE = 64
CONV_WIDTH = 1024
CONV_K = 31
N_BRANCHES = 3
D_FF = -(-8 * D_MODEL // (3 * 256)) * 256
EPS = 1e-6

LANES = 128
VMEM_LIMIT = 56 * 2 ** 20

Q_TILE = ATTN_LAGS
ATTN_ROWS = 512
SSM_CHUNK = 8
SSM_LANE_BLOCK = LANES
SSM_BLOCKS = SSM_WIDTH // SSM_LANE_BLOCK
SSM_GROUPS_PER_BLOCK = SSM_LANE_BLOCK // SSM_GROUP
SSM_STATE_LANES = SSM_GROUPS_PER_BLOCK * SSM_STATE
CONV_HALO = 32
CONV_ROWS = 512
CONV_RC = 64


def _params(*sem):
    return pltpu.CompilerParams(dimension_semantics=sem, vmem_limit_bytes=VMEM_LIMIT)


def _rmsnorm_rows(x, g):
    return x * lax.rsqrt(jnp.mean(x * x, axis=-1, keepdims=True) + EPS) * g


def _rmsnorm_kernel(x_ref, g_ref, o_ref):
    o_ref[...] = _rmsnorm_rows(x_ref[...], g_ref[...]).astype(o_ref.dtype)


def _rmsnorm(x, g, out_dtype, tm=512):
    m, d = x.shape
    return pl.pallas_call(
        _rmsnorm_kernel,
        grid=(m // tm,),
        in_specs=[pl.BlockSpec((tm, d), lambda i: (i, 0)),
                  pl.BlockSpec((1, d), lambda i: (0, 0))],
        out_specs=pl.BlockSpec((tm, d), lambda i: (i, 0)),
        out_shape=jax.ShapeDtypeStruct((m, d), out_dtype),
        compiler_params=_params("parallel"),
        name="rmsnorm",
    )(x, g.reshape(1, d))


def _matmul_kernel(a_ref, b_ref, o_ref):
    o_ref[...] = jnp.dot(a_ref[...], b_ref[...], preferred_element_type=F32).astype(o_ref.dtype)


def _matmul(a, b, tm, tn, name):
    m, k = a.shape
    n = b.shape[1]
    return pl.pallas_call(
        _matmul_kernel,
        grid=(m // tm, n // tn),
        in_specs=[pl.BlockSpec((tm, k), lambda i, j: (i, 0)),
                  pl.BlockSpec((k, tn), lambda i, j: (0, j))],
        out_specs=pl.BlockSpec((tm, tn), lambda i, j: (i, j)),
        out_shape=jax.ShapeDtypeStruct((m, n), BF16),
        compiler_params=_params("parallel", "arbitrary"),
        name=name,
    )(a, b)


def _matmul_lane_blocks_kernel(a_ref, b_ref, o_ref):
    acc = jnp.dot(a_ref[...], b_ref[...], preferred_element_type=F32)
    for blk in range(o_ref.shape[0]):
        o_ref[blk] = acc[:, blk * LANES:(blk + 1) * LANES].astype(o_ref.dtype)


def _matmul_lane_blocks(a, b, tm):
    m, k = a.shape
    n = b.shape[1]
    nblk = n // LANES
    return pl.pallas_call(
        _matmul_lane_blocks_kernel,
        grid=(m // tm,),
        in_specs=[pl.BlockSpec((tm, k), lambda i: (i, 0)),
                  pl.BlockSpec((k, n), lambda i: (0, 0))],
        out_specs=pl.BlockSpec((nblk, tm, LANES), lambda i: (0, i, 0)),
        out_shape=jax.ShapeDtypeStruct((nblk, m, LANES), BF16),
        compiler_params=_params("parallel"),
        name="inproj_ssm",
    )(a, b)


def _t5_bucket_causal(dist):
    max_exact = NUM_BUCKETS // 2
    d = np.maximum(dist, 1).astype(np.float32)
    large = max_exact + (np.log(d / max_exact) / math.log(MAX_DISTANCE / max_exact)
                         * (NUM_BUCKETS - max_exact)).astype(np.int32)
    large = np.minimum(large, NUM_BUCKETS - 1)
    return np.where(dist < max_exact, dist, large).astype(np.int32)


def _band_bias(rel_bias, g):
    _, d = ATTN_GROUPS[g]
    hs = slice(g * HEADS_PER_GROUP, (g + 1) * HEADS_PER_GROUP)
    bias = rel_bias[_t5_bucket_causal(np.arange(ATTN_LAGS + 1) * d)][:, hs].T.astype(F32)
    lag = np.arange(Q_TILE)[:, None] + ATTN_LAGS - np.arange(Q_TILE + ATTN_LAGS)[None, :]
    on_band = (lag >= 0) & (lag <= ATTN_LAGS)
    return jnp.where(on_band[None], bias[:, np.clip(lag, 0, ATTN_LAGS)], NEG_INF)


def _attn_kernel(q_ref, kp_ref, kc_ref, vp_ref, vc_ref, bm_ref, o_ref, lse_ref):
    tile = pl.program_id(2)
    rows = q_ref.shape[0]
    kwin = jnp.concatenate([kp_ref[...], kc_ref[...]], axis=0)
    vwin = jnp.concatenate([vp_ref[...], vc_ref[...]], axis=0)
    for sub in range(rows // Q_TILE):
        r0 = sub * Q_TILE
        q = q_ref[r0:r0 + Q_TILE, :]
        kk = kwin[r0:r0 + Q_TILE + ATTN_LAGS]
        vv = vwin[r0:r0 + Q_TILE + ATTN_LAGS]
        for h in range(HEADS_PER_GROUP):
            hs = slice(h * HEAD_DIM, (h + 1) * HEAD_DIM)
            s = lax.dot_general(q[:, hs], kk[:, hs], (((1,), (1,)), ((), ())),
                                preferred_element_type=F32)
            s = s + bm_ref[h]
            if sub == 0:
                col = lax.broadcasted_iota(jnp.int32, s.shape, 1)
                s = jnp.where((col >= ATTN_LAGS) | (tile > 0), s, NEG_INF)
            m = jnp.max(s, axis=-1, keepdims=True)
            p = jnp.exp(s - m)
            den = jnp.sum(p, axis=-1, keepdims=True)
            o = jnp.dot(p.astype(BF16), vv[:, hs], preferred_element_type=F32) * (1.0 / den)
            o_ref[r0:r0 + Q_TILE, hs] = o.astype(o_ref.dtype)
            lse_ref[r0:r0 + Q_TILE, hs] = jnp.broadcast_to(m + jnp.log(den), (Q_TILE, HEAD_DIM))


def _attention_group(qkv, bm, g, batch, seq):
    _, d = ATTN_GROUPS[g]
    sd = seq // d
    rows = min(ATTN_ROWS, sd)
    width = qkv.shape[1]
    cb = width // ATTN_OUT_W
    qkv3 = qkv.reshape(batch, sd, d * width)
    kb = ATTN_QW // ATTN_OUT_W
    sub_per_step = rows // ATTN_LAGS

    def cur(off):
        return pl.BlockSpec((None, rows, ATTN_OUT_W), lambda b, r, i: (b, i, r * cb + off + g))

    def prev(off):
        return pl.BlockSpec((None, ATTN_LAGS, ATTN_OUT_W),
                            lambda b, r, i: (b, jnp.maximum(i * sub_per_step - 1, 0), r * cb + off + g))

    out_spec = pl.BlockSpec((None, rows, ATTN_OUT_W), lambda b, r, i: (b, i, r))
    o, lse = pl.pallas_call(
        _attn_kernel,
        grid=(batch, d, sd // rows),
        in_specs=[cur(0), prev(kb), cur(kb), prev(2 * kb), cur(2 * kb),
                  pl.BlockSpec(bm.shape, lambda b, r, i: (0, 0, 0))],
        out_specs=[out_spec, out_spec],
        out_shape=[jax.ShapeDtypeStruct((batch, sd, d * ATTN_OUT_W), BF16),
                   jax.ShapeDtypeStruct((batch, sd, d * ATTN_OUT_W), F32)],
        compiler_params=_params("parallel", "parallel", "arbitrary"),
        name=f"attn_g{g}",
    )(qkv3, qkv3, qkv3, qkv3, qkv3, bm)
    return o.reshape(batch * seq, ATTN_OUT_W), lse.reshape(batch * seq, ATTN_OUT_W)


def _attn_combine_kernel(o0, o1, o2, l0, l1, l2, out_ref):
    m = jnp.maximum(jnp.maximum(l0[...], l1[...]), l2[...])
    e0, e1, e2 = jnp.exp(l0[...] - m), jnp.exp(l1[...] - m), jnp.exp(l2[...] - m)
    num = e0 * o0[...].astype(F32) + e1 * o1[...].astype(F32) + e2 * o2[...].astype(F32)
    out_ref[...] = (num / (e0 + e1 + e2)).astype(out_ref.dtype)


def _attn_combine(outs, lses, tm=1024):
    m, w = outs[0].shape
    spec = pl.BlockSpec((tm, w), lambda i: (i, 0))
    return pl.pallas_call(
        _attn_combine_kernel,
        grid=(m // tm,),
        in_specs=[spec] * 6,
        out_specs=spec,
        out_shape=jax.ShapeDtypeStruct((m, w), BF16),
        compiler_params=_params("parallel"),
        name="attn_combine",
    )(*outs, *lses)


def _ssm_operators(lam_re, lam_im, log_dt, b_re, b_im, c_re, c_im, d_skip):
    hp = lax.Precision.HIGHEST
    t_len = SSM_CHUNK
    n_g, n_p = lam_re.shape
    dt = jnp.exp(log_dt)[:, None]
    lr, li = lam_re, lam_im
    mag = jnp.exp(lr * dt)
    ab_re, ab_im = mag * jnp.cos(li * dt), mag * jnp.sin(li * dt)
    nr, ni = ab_re - 1.0, ab_im
    den = lr * lr + li * li
    z_re, z_im = (nr * lr + ni * li) / den, (ni * lr - nr * li) / den
    bb_re = z_re[..., None] * b_re - z_im[..., None] * b_im
    bb_im = z_re[..., None] * b_im + z_im[..., None] * b_re
    tau = jnp.arange(t_len + 1, dtype=F32)[:, None, None]
    pmag = jnp.exp(lr * dt * tau)
    pw_re, pw_im = pmag * jnp.cos(li * dt * tau), pmag * jnp.sin(li * dt * tau)
    e_re = pw_re[..., None] * bb_re - pw_im[..., None] * bb_im
    e_im = pw_re[..., None] * bb_im + pw_im[..., None] * bb_re
    kern = (jnp.einsum('gop,tgpc->gtoc', c_re, e_re[:t_len], precision=hp)
            - jnp.einsum('gop,tgpc->gtoc', c_im, e_im[:t_len], precision=hp))

    nb, gl = SSM_BLOCKS, SSM_GROUPS_PER_BLOCK
    eye = jnp.eye(gl, dtype=F32)
    lag = np.arange(t_len)[None, :] - np.arange(t_len)[:, None]
    kt = kern[:, np.clip(lag, 0, t_len - 1)] * jnp.asarray(lag >= 0, F32)[None, :, :, None, None]
    kt = kt.reshape(nb, gl, t_len, t_len, SSM_GROUP, SSM_GROUP)
    w = jnp.einsum('bgjioc,gh->bjgciho', kt, eye).reshape(nb, t_len * LANES, t_len * LANES)

    def state_in(e):
        v = e[:t_len][::-1].transpose(1, 0, 3, 2).reshape(nb, gl, t_len, SSM_GROUP, n_p)
        return jnp.einsum('bgjcp,gh->bjgchp', v, eye).reshape(nb, t_len * LANES, SSM_STATE_LANES)

    v = jnp.concatenate([state_in(e_re), state_in(e_im)], axis=-1)

    cp_re = c_re[:, None] * pw_re[1:].transpose(1, 0, 2)[:, :, None] \
        - c_im[:, None] * pw_im[1:].transpose(1, 0, 2)[:, :, None]
    cp_im = c_re[:, None] * pw_im[1:].transpose(1, 0, 2)[:, :, None] \
        + c_im[:, None] * pw_re[1:].transpose(1, 0, 2)[:, :, None]

    def state_out(cp):
        zz = cp.transpose(0, 3, 1, 2).reshape(nb, gl, n_p, t_len, SSM_GROUP)
        return jnp.einsum('bgpio,gh->bgpiho', zz, eye).reshape(nb, SSM_STATE_LANES, t_len * LANES)

    z = jnp.concatenate([state_out(cp_re), -state_out(cp_im)], axis=1)

    apow = jnp.concatenate([pw_re[t_len].reshape(nb, 1, SSM_STATE_LANES),
                            pw_im[t_len].reshape(nb, 1, SSM_STATE_LANES)], axis=-1)
    dsk = jnp.tile(d_skip.reshape(nb, 1, LANES), (1, 1, t_len))
    return w.astype(BF16), v.astype(BF16), z.astype(BF16), apow, dsk


def _ssm_kernel(u_ref, w_ref, v_ref, z_ref, ap_ref, dsk_ref, y_ref, s_ref, *, n_batch, row_tile):
    rows = u_ref.shape[0]
    n_chunks = rows // n_batch
    nt = SSM_STATE_LANES // LANES
    lane = lambda t: slice(t * LANES, (t + 1) * LANES)
    for r in range(rows // row_tile):
        rs = slice(r * row_tile, (r + 1) * row_tile)
        s = jnp.dot(u_ref[rs, :], v_ref[...], preferred_element_type=F32)
        for t in range(2 * nt):
            s_ref[t, rs, :] = s[:, lane(t)]

    ar = [jnp.broadcast_to(ap_ref[:, lane(t)], (n_batch, LANES)) for t in range(nt)]
    ai = [jnp.broadcast_to(ap_ref[:, lane(nt + t)], (n_batch, LANES)) for t in range(nt)]

    def step(k, carry):
        idx = pl.ds(k, n_batch, stride=n_chunks)
        new = []
        for t in range(nt):
            xr, xi = carry[t]
            re_ref, im_ref = s_ref.at[t], s_ref.at[nt + t]
            sr, si = re_ref[idx, :], im_ref[idx, :]
            re_ref[idx, :] = xr
            im_ref[idx, :] = xi
            new.append((ar[t] * xr - ai[t] * xi + sr, ar[t] * xi + ai[t] * xr + si))
        return tuple(new)

    zero = jnp.zeros((n_batch, LANES), F32)
    lax.fori_loop(0, n_chunks, step, tuple((zero, zero) for _ in range(nt)))

    for r in range(rows // row_tile):
        rs = slice(r * row_tile, (r + 1) * row_tile)
        u = u_ref[rs, :]
        state = jnp.concatenate([s_ref[t, rs, :] for t in range(2 * nt)], axis=1).astype(BF16)
        y = (jnp.dot(u, w_ref[...], preferred_element_type=F32)
             + jnp.dot(state, z_ref[...], preferred_element_type=F32)
             + dsk_ref[...] * u.astype(F32))
        y_ref[rs, :] = jax.nn.gelu(y).astype(y_ref.dtype)


def _ssm(u_blocks, ops, batch, seq):
    w, v, z, apow, dsk = ops
    nb = u_blocks.shape[0]
    rows = batch * seq // SSM_CHUNK
    cw = SSM_CHUNK * LANES
    u3 = u_blocks.reshape(nb, rows, cw)
    blk = lambda shape: pl.BlockSpec((None,) + shape, lambda b: (b, 0, 0))
    y = pl.pallas_call(
        functools.partial(_ssm_kernel, n_batch=batch, row_tile=min(512, rows)),
        grid=(nb,),
        in_specs=[blk((rows, cw)), blk(w.shape[1:]), blk(v.shape[1:]), blk(z.shape[1:]),
                  blk(apow.shape[1:]), blk(dsk.shape[1:])],
        out_specs=blk((rows, cw)),
        out_shape=jax.ShapeDtypeStruct((nb, rows, cw), BF16),
        scratch_shapes=[pltpu.VMEM((2 * SSM_STATE_LANES // LANES, rows, LANES), F32)],
        compiler_params=_params("parallel"),
        name="ssm",
    )(u3, w, v, z, apow, dsk)
    return y.reshape(nb, batch * seq, LANES)


def _conv_kernel(vc_ref, gc_ref, vh_ref, gh_ref, w_ref, cb_ref, lg_ref, lb_ref, o_ref, buf_ref, acc_ref):
    tile = pl.program_id(1)
    rows = vc_ref.shape[0]
    n_lt = CONV_WIDTH // LANES
    halo = vh_ref[...].astype(F32) * jax.nn.sigmoid(gh_ref[...].astype(F32))
    halo = jnp.where(tile > 0, halo, 0.0)
    glu = vc_ref[...].astype(F32) * jax.nn.sigmoid(gc_ref[...].astype(F32))
    for lt in range(n_lt):
        buf_ref[lt, 0:CONV_HALO, :] = halo[:, lt * LANES:(lt + 1) * LANES]
        buf_ref[lt, CONV_HALO:, :] = glu[:, lt * LANES:(lt + 1) * LANES]
    first = CONV_HALO - (CONV_K - 1)

    def lane_tile(lt, carry):
        for r0 in range(0, rows, CONV_RC):
            acc = jnp.broadcast_to(cb_ref[lt], (CONV_RC, LANES))
            for k in range(CONV_K):
                acc = acc + w_ref[lt, k:k + 1, :] * buf_ref[lt, r0 + first + k:r0 + first + k + CONV_RC, :]
            acc_ref[lt, r0:r0 + CONV_RC, :] = acc
        return carry

    lax.fori_loop(0, n_lt, lane_tile, 0)
    c = jnp.concatenate([acc_ref[lt] for lt in range(n_lt)], axis=1)
    mu = jnp.mean(c, axis=-1, keepdims=True)
    var = jnp.mean(jnp.square(c - mu), axis=-1, keepdims=True)
    y = (c - mu) * lax.rsqrt(var + EPS) * lg_ref[...] + lb_ref[...]
    o_ref[...] = jax.nn.silu(y).astype(o_ref.dtype)


def _conformer_conv(cv, conv_w, conv_b, ln_g, ln_b, batch, seq):
    cw = CONV_WIDTH
    n_lt = cw // LANES
    rows = min(CONV_ROWS, seq)
    cv3 = cv.reshape(batch, seq, 2 * cw)
    hb = rows // CONV_HALO
    cur = lambda c: pl.BlockSpec((None, rows, cw), lambda b, i: (b, i, c))
    halo = lambda c: pl.BlockSpec((None, CONV_HALO, cw), lambda b, i: (b, jnp.maximum(i * hb - 1, 0), c))
    vec = pl.BlockSpec((1, cw), lambda b, i: (0, 0))
    tiled = lambda n: pl.BlockSpec((n_lt, n, LANES), lambda b, i: (0, 0, 0))
    w_tiles = conv_w.reshape(CONV_K, n_lt, LANES).transpose(1, 0, 2)
    out = pl.pallas_call(
        _conv_kernel,
        grid=(batch, seq // rows),
        in_specs=[cur(0), cur(1), halo(0), halo(1), tiled(CONV_K), tiled(1), vec, vec],
        out_specs=pl.BlockSpec((None, rows, cw), lambda b, i: (b, i, 0)),
        out_shape=jax.ShapeDtypeStruct((batch, seq, cw), BF16),
        scratch_shapes=[pltpu.VMEM((n_lt, CONV_HALO + rows, LANES), F32), pltpu.VMEM((n_lt, rows, LANES), F32)],
        compiler_params=_params("parallel", "arbitrary"),
        name="conformer_conv",
    )(cv3, cv3, cv3, cv3, w_tiles, conv_b.reshape(n_lt, 1, LANES), ln_g.reshape(1, cw), ln_b.reshape(1, cw))
    return out.reshape(batch * seq, cw)


def _merge_kernel(a_ref, y_ref, c_ref, g0_ref, g1_ref, g2_ref, wao_ref, wa_ref, wb_ref, wco_ref, o_ref, y_scr):
    @pl.when(pl.program_id(1) == 0)
    def _():
        y_scr[...] = jnp.concatenate([y_ref[b] for b in range(y_ref.shape[0])], axis=1)

    dot = functools.partial(jnp.dot, preferred_element_type=F32)
    sig = lambda r: jax.nn.sigmoid(r[...].astype(F32))
    y = y_scr[...]
    o_a = dot(a_ref[...], wao_ref[...])
    o_s = dot(y, wa_ref[...]) * jax.nn.sigmoid(dot(y, wb_ref[...]))
    o_c = dot(c_ref[...], wco_ref[...])
    o_ref[...] = (sig(g0_ref) * o_a + sig(g1_ref) * o_s + sig(g2_ref) * o_c).astype(o_ref.dtype)


def _merge(attn, y_blocks, conv, gates, w_attn_out, w_glu_a, w_glu_b, w_conv_out, tm=1024, tn=512):
    m = attn.shape[0]
    d = w_attn_out.shape[1]
    nt = d // tn
    nb = y_blocks.shape[0]
    tok = lambda w: pl.BlockSpec((tm, w), lambda i, j: (i, 0))
    gate = lambda br: pl.BlockSpec((tm, tn), lambda i, j: (i, br * nt + j))
    wgt = lambda k: pl.BlockSpec((k, tn), lambda i, j: (0, j))
    return pl.pallas_call(
        _merge_kernel,
        grid=(m // tm, nt),
        in_specs=[tok(attn.shape[1]),
                  pl.BlockSpec((nb, tm, LANES), lambda i, j: (0, i, 0)),
                  tok(conv.shape[1]), gate(0), gate(1), gate(2),
                  wgt(w_attn_out.shape[0]), wgt(w_glu_a.shape[0]), wgt(w_glu_b.shape[0]),
                  wgt(w_conv_out.shape[0])],
        out_specs=pl.BlockSpec((tm, tn), lambda i, j: (i, j)),
        out_shape=jax.ShapeDtypeStruct((m, d), BF16),
        scratch_shapes=[pltpu.VMEM((tm, nb * LANES), BF16)],
        compiler_params=_params("parallel", "arbitrary"),
        name="merge",
    )(attn, y_blocks, conv, gates, gates, gates, w_attn_out, w_glu_a, w_glu_b, w_conv_out)


def _residual_proj_kernel(a_ref, w_ref, x_ref, g_ref, xo_ref, ho_ref, acc_ref):
    k = pl.program_id(1)

    @pl.when(k == 0)
    def _():
        acc_ref[...] = x_ref[...]

    acc_ref[...] += jnp.dot(a_ref[...], w_ref[...], preferred_element_type=F32)

    @pl.when(k == pl.num_programs(1) - 1)
    def _():
        x = acc_ref[...]
        xo_ref[...] = x
        ho_ref[...] = _rmsnorm_rows(x, g_ref[...]).astype(ho_ref.dtype)


def _residual_proj(a, w, x, g, h_dtype, tm, tk, name):
    m, kdim = a.shape
    d = w.shape[1]
    row = lambda i, k: (i, 0)
    return pl.pallas_call(
        _residual_proj_kernel,
        grid=(m // tm, kdim // tk),
        in_specs=[pl.BlockSpec((tm, tk), lambda i, k: (i, k)),
                  pl.BlockSpec((tk, d), lambda i, k: (k, 0)),
                  pl.BlockSpec((tm, d), row),
                  pl.BlockSpec((1, d), lambda i, k: (0, 0))],
        out_specs=[pl.BlockSpec((tm, d), row), pl.BlockSpec((tm, d), row)],
        out_shape=[jax.ShapeDtypeStruct((m, d), F32), jax.ShapeDtypeStruct((m, d), h_dtype)],
        scratch_shapes=[pltpu.VMEM((tm, d), F32)],
        compiler_params=_params("parallel", "arbitrary"),
        name=name,
    )(a, w, x, g.reshape(1, d))


def _ffn_up_kernel(h_ref, wg_ref, wu_ref, o_ref):
    h = h_ref[...]
    gate = jnp.dot(h, wg_ref[...], preferred_element_type=F32)
    up = jnp.dot(h, wu_ref[...], preferred_element_type=F32)
    o_ref[...] = (jax.nn.silu(gate) * up).astype(o_ref.dtype)


def _ffn_up(h, wg, wu, tm=1024, tn=512):
    m, k = h.shape
    n = wg.shape[1]
    wspec = pl.BlockSpec((k, tn), lambda i, j: (0, j))
    return pl.pallas_call(
        _ffn_up_kernel,
        grid=(m // tm, n // tn),
        in_specs=[pl.BlockSpec((tm, k), lambda i, j: (i, 0)), wspec, wspec],
        out_specs=pl.BlockSpec((tm, tn), lambda i, j: (i, j)),
        out_shape=jax.ShapeDtypeStruct((m, n), BF16),
        compiler_params=_params("parallel", "arbitrary"),
        name="ffn_up",
    )(h, wg, wu)


def kernel(x, w_in, rel_bias, lam_re, lam_im, log_dt, b_re, b_im, c_re, c_im, d_skip, w_ssm_glu_a, w_ssm_glu_b, conv_w, conv_b, conv_ln_g, conv_ln_b, w_conv_out, w_attn_out, w_out, norm_mix_g, norm_ffn_g, w_ffn_gate, w_ffn_up, w_ffn_down, norm_final_g):
    batch, seq, d_model = x.shape
    m = batch * seq
    xf = x.reshape(m, d_model)
    bands = [_band_bias(rel_bias, g) for g in range(N_GROUPS)]
    q_end, u_end, c_end = 3 * ATTN_QW, 3 * ATTN_QW + SSM_WIDTH, 3 * ATTN_QW + SSM_WIDTH + 2 * CONV_WIDTH

    h = _rmsnorm(xf, norm_mix_g[0], BF16)
    for l in range(DEPTH):
        w_qkv = jnp.concatenate([w_in[l][:, :ATTN_QW] * ATTN_SCALE, w_in[l][:, ATTN_QW:q_end]], axis=1).astype(BF16)
        qkv = _matmul(h, w_qkv, 1024, 1536, "inproj_qkv")
        u_blocks = _matmul_lane_blocks(h, w_in[l][:, q_end:u_end].astype(BF16), 1024)
        cv = _matmul(h, w_in[l][:, u_end:c_end].astype(BF16), 1024, 1024, "inproj_conv")
        gates = _matmul(h, w_in[l][:, c_end:].astype(BF16), 1024, 1024, "inproj_gates")

        outs, lses = zip(*[_attention_group(qkv, bands[g], g, batch, seq) for g in range(N_GROUPS)])
        attn = _attn_combine(outs, lses)
        ops = _ssm_operators(lam_re[l], lam_im[l], log_dt[l], b_re[l], b_im[l], c_re[l], c_im[l], d_skip[l])
        y_blocks = _ssm(u_blocks, ops, batch, seq)
        conv = _conformer_conv(cv, conv_w[l], conv_b[l], conv_ln_g[l], conv_ln_b[l], batch, seq)

        merged = _merge(attn, y_blocks, conv, gates, w_attn_out[l].astype(BF16), w_ssm_glu_a[l].astype(BF16),
                        w_ssm_glu_b[l].astype(BF16), w_conv_out[l].astype(BF16))
        xf, h = _residual_proj(merged, w_out[l].astype(BF16), xf, norm_ffn_g[l], BF16, 512, d_model, "out_proj")
        act = _ffn_up(h, w_ffn_gate[l].astype(BF16), w_ffn_up[l].astype(BF16))
        last = l == DEPTH - 1
        g_next = norm_final_g if last else norm_mix_g[l + 1]
        xf, h = _residual_proj(act, w_ffn_down[l].astype(BF16), xf, g_next, F32 if last else BF16,
                               512, D_FF // 4, "ffn_down")
    return h.reshape(batch, seq, d_model)
```

```python
import functools
import math

import jax
import jax.numpy as jnp
import numpy as np
from jax import lax
from jax.experimental import pallas as pl
from jax.experimental.pallas import tpu as pltpu

F32 = jnp.float32
BF16 = jnp.bfloat16

D_MODEL = 2048
DEPTH = 4
HEAD_DIM = 64
HEADS_PER_GROUP = 8
ATTN_GROUPS = ((128, 1), (512, 4), (2048, 16))
N_GROUPS = len(ATTN_GROUPS)
N_ATTN_HEADS = HEADS_PER_GROUP * N_GROUPS
ATTN_QW = N_ATTN_HEADS * HEAD_DIM
ATTN_OUT_W = HEADS_PER_GROUP * HEAD_DIM
ATTN_LAGS = ATTN_GROUPS[0][0] // ATTN_GROUPS[0][1]
NUM_BUCKETS = 32
MAX_DISTANCE = max(w for w, _ in ATTN_GROUPS)
ATTN_SCALE = HEAD_DIM ** -0.5
NEG_INF = -1e30
SSM_WIDTH = 1024
SSM_GROUP = 16
SSM_STATE = 64
CONV_WIDTH = 1024
CONV_K = 31
N_BRANCHES = 3
D_FF = -(-8 * D_MODEL // (3 * 256)) * 256
EPS = 1e-6

LANES = 128
BF16_ROWS = 16
VMEM_LIMIT = 56 * 2 ** 20

SLABS = max(d for _, d in ATTN_GROUPS)
ATTN_TILES = ((BF16_ROWS, 64), (32, 128), (128, 256))
SSM_CHUNK = SLABS
SSM_BLOCKS = SSM_WIDTH // LANES
SSM_GROUPS_PER_BLOCK = LANES // SSM_GROUP
SSM_STATE_LANES = SSM_GROUPS_PER_BLOCK * SSM_STATE
SSM_BATCH_PER_STEP = 2
CONV_ROWS = 32
CONV_HALO = BF16_ROWS


def _params(*sem):
    return pltpu.CompilerParams(dimension_semantics=sem, vmem_limit_bytes=VMEM_LIMIT)


def _rmsnorm_rows(x, g):
    return x * lax.rsqrt(jnp.mean(x * x, axis=-1, keepdims=True) + EPS) * g


def _rmsnorm_kernel(x_ref, g_ref, o_ref):
    o_ref[...] = _rmsnorm_rows(x_ref[...], g_ref[...]).astype(o_ref.dtype)


def _rmsnorm(x, g, out_dtype, tm=512):
    m, d = x.shape
    return pl.pallas_call(
        _rmsnorm_kernel,
        grid=(m // tm,),
        in_specs=[pl.BlockSpec((tm, d), lambda i: (i, 0)),
                  pl.BlockSpec((1, d), lambda i: (0, 0))],
        out_specs=pl.BlockSpec((tm, d), lambda i: (i, 0)),
        out_shape=jax.ShapeDtypeStruct((m, d), out_dtype),
        compiler_params=_params("parallel"),
        name="rmsnorm",
    )(x, g.reshape(1, d))


def _matmul_kernel(a_ref, b_ref, o_ref):
    o_ref[...] = jnp.dot(a_ref[...], b_ref[...], preferred_element_type=F32).astype(o_ref.dtype)


def _matmul(a, b, tm, tn, name):
    m, k = a.shape
    n = b.shape[1]
    return pl.pallas_call(
        _matmul_kernel,
        grid=(m // tm, n // tn),
        in_specs=[pl.BlockSpec((tm, k), lambda i, j: (i, 0)),
                  pl.BlockSpec((k, tn), lambda i, j: (0, j))],
        out_specs=pl.BlockSpec((tm, tn), lambda i, j: (i, j)),
        out_shape=jax.ShapeDtypeStruct((m, n), BF16),
        compiler_params=_params("parallel", "arbitrary"),
        name=name,
    )(a, b)


def _t5_bucket_causal(dist):
    max_exact = NUM_BUCKETS // 2
    d = np.maximum(dist, 1).astype(np.float32)
    large = max_exact + (np.log(d / max_exact) / math.log(MAX_DISTANCE / max_exact)
                         * (NUM_BUCKETS - max_exact)).astype(np.int32)
    large = np.minimum(large, NUM_BUCKETS - 1)
    return np.where(dist < max_exact, dist, large).astype(np.int32)


def _band_bias(rel_bias, g):
    _, d = ATTN_GROUPS[g]
    n_c = SLABS // d
    ni = ATTN_TILES[g][0]
    hs = slice(g * HEADS_PER_GROUP, (g + 1) * HEADS_PER_GROUP)
    bias = rel_bias[_t5_bucket_causal(np.arange(ATTN_LAGS + 1) * d)][:, hs].T.astype(F32)
    qc, qi = np.divmod(np.arange(n_c * ni), ni)
    kc, ki = np.divmod(np.arange(n_c * 2 * ni), 2 * ni)
    lag = n_c * (qi[:, None] - ki[None, :] + ni) + (qc[:, None] - kc[None, :])
    on_band = (lag >= 0) & (lag <= ATTN_LAGS)
    return jnp.where(on_band[None], bias[:, np.clip(lag, 0, ATTN_LAGS)], NEG_INF)


def _attn_kernel(q_ref, kp_ref, kc_ref, vp_ref, vc_ref, bm_ref, o_ref, lse_ref, *, ni):
    tile = pl.program_id(2)
    n_c, rows, _ = q_ref.shape
    win = 2 * ni
    kwin = [jnp.concatenate([kp_ref[c], kc_ref[c]], axis=0) for c in range(n_c)]
    vwin = [jnp.concatenate([vp_ref[c], vc_ref[c]], axis=0) for c in range(n_c)]
    for t in range(rows // ni):
        r0 = t * ni
        q = jnp.concatenate([q_ref[c, r0:r0 + ni, :] for c in range(n_c)], axis=0)
        kk = jnp.concatenate([kwin[c][r0:r0 + win] for c in range(n_c)], axis=0)
        vv = jnp.concatenate([vwin[c][r0:r0 + win] for c in range(n_c)], axis=0)
        for h in range(HEADS_PER_GROUP):
            hs = slice(h * HEAD_DIM, (h + 1) * HEAD_DIM)
            s = lax.dot_general(q[:, hs], kk[:, hs], (((1,), (1,)), ((), ())), preferred_element_type=F32)
            s = s + bm_ref[h]
            if t == 0:
                col = lax.broadcasted_iota(jnp.int32, s.shape, 1)
                s = jnp.where((col % win >= ni) | (tile > 0), s, NEG_INF)
            m = jnp.max(s, axis=-1, keepdims=True)
            p = jnp.exp(s - m)
            den = jnp.sum(p, axis=-1, keepdims=True)
            o = jnp.dot(p.astype(BF16), vv[:, hs], preferred_element_type=F32) * (1.0 / den)
            lse = jnp.broadcast_to(m + jnp.log(den), o.shape)
            for c in range(n_c):
                o_ref[c, r0:r0 + ni, hs] = o[c * ni:(c + 1) * ni].astype(o_ref.dtype)
                lse_ref[c, r0:r0 + ni, hs] = lse[c * ni:(c + 1) * ni]


def _attention_group(qkv, bm, g, batch, seq):
    _, d = ATTN_GROUPS[g]
    n_c, n_r = SLABS // d, d
    ni, rows = ATTN_TILES[g]
    si = seq // SLABS
    rows = min(rows, si)
    width = qkv.shape[1]
    qkv5 = qkv.reshape(batch, n_c, n_r, si, width)
    kb = ATTN_QW // ATTN_OUT_W
    prev_per_step = rows // ni

    def cur(off):
        return pl.BlockSpec((None, n_c, None, rows, ATTN_OUT_W), lambda b, r, i: (b, 0, r, i, off + g))

    def prev(off):
        return pl.BlockSpec((None, n_c, None, ni, ATTN_OUT_W),
                            lambda b, r, i: (b, 0, r, jnp.maximum(i * prev_per_step - 1, 0), off + g))

    out_spec = pl.BlockSpec((None, n_c, None, rows, ATTN_OUT_W), lambda b, r, i: (b, 0, r, i, 0))
    o, lse = pl.pallas_call(
        functools.partial(_attn_kernel, ni=ni),
        grid=(batch, n_r, si // rows),
        in_specs=[cur(0), prev(kb), cur(kb), prev(2 * kb), cur(2 * kb),
                  pl.BlockSpec(bm.shape, lambda b, r, i: (0, 0, 0))],
        out_specs=[out_spec, out_spec],
        out_shape=[jax.ShapeDtypeStruct((batch, n_c, n_r, si, ATTN_OUT_W), BF16),
                   jax.ShapeDtypeStruct((batch, n_c, n_r, si, ATTN_OUT_W), F32)],
        compiler_params=_params("parallel", "parallel", "arbitrary"),
        name=f"attn_g{g}",
    )(qkv5, qkv5, qkv5, qkv5, qkv5, bm)
    return o.reshape(batch * seq, ATTN_OUT_W), lse.reshape(batch * seq, ATTN_OUT_W)


def _attn_combine_kernel(o0, o1, o2, l0, l1, l2, out_ref):
    m = jnp.maximum(jnp.maximum(l0[...], l1[...]), l2[...])
    e0, e1, e2 = jnp.exp(l0[...] - m), jnp.exp(l1[...] - m), jnp.exp(l2[...] - m)
    num = e0 * o0[...].astype(F32) + e1 * o1[...].astype(F32) + e2 * o2[...].astype(F32)
    out_ref[...] = (num / (e0 + e1 + e2)).astype(out_ref.dtype)


def _attn_combine(outs, lses, tm=1024):
    m, w = outs[0].shape
    spec = pl.BlockSpec((tm, w), lambda i: (i, 0))
    return pl.pallas_call(
        _attn_combine_kernel,
        grid=(m // tm,),
        in_specs=[spec] * 6,
        out_specs=spec,
        out_shape=jax.ShapeDtypeStruct((m, w), BF16),
        compiler_params=_params("parallel"),
        name="attn_combine",
    )(*outs, *lses)


def _ssm_operators(lam_re, lam_im, log_dt, b_re, b_im, c_re, c_im, d_skip):
    hp = lax.Precision.HIGHEST
    t_len, half = SSM_CHUNK, SSM_CHUNK // 2
    n_p = lam_re.shape[1]
    dt = jnp.exp(log_dt)[:, None]
    lr, li = lam_re, lam_im
    mag = jnp.exp(lr * dt)
    ab_re, ab_im = mag * jnp.cos(li * dt), mag * jnp.sin(li * dt)
    nr, ni = ab_re - 1.0, ab_im
    den = lr * lr + li * li
    z_re, z_im = (nr * lr + ni * li) / den, (ni * lr - nr * li) / den
    bb_re = z_re[..., None] * b_re - z_im[..., None] * b_im
    bb_im = z_re[..., None] * b_im + z_im[..., None] * b_re
    tau = jnp.arange(t_len + 1, dtype=F32)[:, None, None]
    pmag = jnp.exp(lr * dt * tau)
    pw_re, pw_im = pmag * jnp.cos(li * dt * tau), pmag * jnp.sin(li * dt * tau)
    e_re = pw_re[..., None] * bb_re - pw_im[..., None] * bb_im
    e_im = pw_re[..., None] * bb_im + pw_im[..., None] * bb_re
    kern = (jnp.einsum('gop,tgpc->gtoc', c_re, e_re[:t_len], precision=hp)
            - jnp.einsum('gop,tgpc->gtoc', c_im, e_im[:t_len], precision=hp))

    nb, gl = SSM_BLOCKS, SSM_GROUPS_PER_BLOCK
    eye = jnp.eye(gl, dtype=F32)

    def toeplitz(i0):
        lag = i0 + np.arange(half)[None, :] - np.arange(half)[:, None]
        kt = kern[:, np.clip(lag, 0, t_len - 1)] * jnp.asarray(lag >= 0, F32)[None, :, :, None, None]
        kt = kt.reshape(nb, gl, half, half, SSM_GROUP, SSM_GROUP)
        return jnp.einsum('bgjioc,gh->bjgciho', kt, eye).reshape(nb, half * LANES, half * LANES)

    def state_in(e):
        v = e[:t_len][::-1].transpose(1, 0, 3, 2).reshape(nb, gl, t_len, SSM_GROUP, n_p)
        return jnp.einsum('bgjcp,gh->bjgchp', v, eye).reshape(nb, t_len * LANES, SSM_STATE_LANES)

    v = jnp.concatenate([state_in(e_re), state_in(e_im)], axis=-1)

    pr, pi = pw_re[1:].transpose(1, 0, 2)[:, :, None], pw_im[1:].transpose(1, 0, 2)[:, :, None]
    cp_re = c_re[:, None] * pr - c_im[:, None] * pi
    cp_im = c_re[:, None] * pi + c_im[:, None] * pr

    def state_out(cp):
        zz = cp.transpose(0, 3, 1, 2).reshape(nb, gl, n_p, t_len, SSM_GROUP)
        return jnp.einsum('bgpio,gh->bgpiho', zz, eye).reshape(nb, SSM_STATE_LANES, t_len * LANES)

    z = jnp.concatenate([state_out(cp_re), -state_out(cp_im)], axis=1)

    apow = jnp.concatenate([pw_re[t_len].reshape(nb, 1, SSM_STATE_LANES),
                            pw_im[t_len].reshape(nb, 1, SSM_STATE_LANES)], axis=-1)
    dsk = jnp.tile(d_skip.reshape(nb, 1, LANES), (1, 1, t_len))
    return toeplitz(0).astype(BF16), toeplitz(half).astype(BF16), v.astype(BF16), z.astype(BF16), apow, dsk


def _ssm_kernel(u_ref, wd_ref, wx_ref, v_ref, z_ref, ap_ref, dsk_ref, y_ref, s_ref):
    n_batch, n_slab, n_chunks, _ = u_ref.shape
    nt = SSM_STATE_LANES // LANES
    half = n_slab // 2 * LANES
    lane = lambda t: slice(t * LANES, (t + 1) * LANES)
    dot = functools.partial(jnp.dot, preferred_element_type=F32)
    chunk_rows = lambda b: jnp.concatenate([u_ref[b, r] for r in range(n_slab)], axis=1)

    for b in range(n_batch):
        s = dot(chunk_rows(b), v_ref[...])
        for t in range(2 * nt):
            s_ref[t, b * n_chunks:(b + 1) * n_chunks, :] = s[:, lane(t)]

    ar = [jnp.broadcast_to(ap_ref[:, lane(t)], (n_batch, LANES)) for t in range(nt)]
    ai = [jnp.broadcast_to(ap_ref[:, lane(nt + t)], (n_batch, LANES)) for t in range(nt)]

    def step(k, carry):
        idx = pl.ds(k, n_batch, stride=n_chunks)
        new = []
        for t in range(nt):
            xr, xi = carry[t]
            re_ref, im_ref = s_ref.at[t], s_ref.at[nt + t]
            sr, si = re_ref[idx, :], im_ref[idx, :]
            re_ref[idx, :] = xr
            im_ref[idx, :] = xi
            new.append((ar[t] * xr - ai[t] * xi + sr, ar[t] * xi + ai[t] * xr + si))
        return tuple(new)

    zero = jnp.zeros((n_batch, LANES), F32)
    lax.fori_loop(0, n_chunks, step, tuple((zero, zero) for _ in range(nt)))

    for b in range(n_batch):
        u = chunk_rows(b)
        rs = slice(b * n_chunks, (b + 1) * n_chunks)
        state = jnp.concatenate([s_ref[t, rs, :] for t in range(2 * nt)], axis=1).astype(BF16)
        y_lo = dot(u[:, :half], wd_ref[...]) + dot(state, z_ref[:, :half])
        y_hi = dot(u[:, :half], wx_ref[...]) + dot(u[:, half:], wd_ref[...]) + dot(state, z_ref[:, half:])
        y = jnp.concatenate([y_lo, y_hi], axis=1) + dsk_ref[...] * u.astype(F32)
        y = jax.nn.gelu(y).astype(y_ref.dtype)
        for r in range(n_slab):
            y_ref[b, r] = y[:, lane(r)]


def _ssm(u, ops, batch, seq):
    wd, wx, v, z, apow, dsk = ops
    si = seq // SLABS
    nbs = min(SSM_BATCH_PER_STEP, batch)
    u4 = u.reshape(batch, SLABS, si, SSM_WIDTH)
    tok = pl.BlockSpec((nbs, SLABS, si, LANES), lambda c, b: (b, 0, 0, c))
    op = lambda a: pl.BlockSpec((None,) + a.shape[1:], lambda c, b: (c, 0, 0))
    y = pl.pallas_call(
        _ssm_kernel,
        grid=(SSM_BLOCKS, batch // nbs),
        in_specs=[tok, op(wd), op(wx), op(v), op(z), op(apow), op(dsk)],
        out_specs=tok,
        out_shape=jax.ShapeDtypeStruct(u4.shape, BF16),
        scratch_shapes=[pltpu.VMEM((2 * SSM_STATE_LANES // LANES, nbs * si, LANES), F32)],
        compiler_params=_params("parallel", "arbitrary"),
        name="ssm",
    )(u4, wd, wx, v, z, apow, dsk)
    return y.reshape(batch * seq, SSM_WIDTH)


def _conv_kernel(vc_ref, gc_ref, vh_ref, gh_ref, w_ref, cb_ref, lg_ref, lb_ref, o_ref, buf_ref, acc_ref):
    tile = pl.program_id(1)
    n_slab, rows, width = vc_ref.shape
    n_lt = width // LANES
    lane = lambda t: slice(t * LANES, (t + 1) * LANES)
    for r in range(n_slab):
        halo = vh_ref[r].astype(F32) * jax.nn.sigmoid(gh_ref[r].astype(F32))
        halo = jnp.where(tile > 0, halo, 0.0)
        glu = vc_ref[r].astype(F32) * jax.nn.sigmoid(gc_ref[r].astype(F32))
        for lt in range(n_lt):
            buf_ref[lt, r, 0:CONV_HALO, :] = halo[:, lane(lt)]
            buf_ref[lt, r, CONV_HALO:, :] = glu[:, lane(lt)]

    def lane_tile(lt, carry):
        for r in range(n_slab):
            acc = jnp.broadcast_to(cb_ref[lt], (rows, LANES))
            for back in range(CONV_K):
                src = (r - back) % n_slab
                wrap = (back - r + n_slab - 1) // n_slab
                k = CONV_K - 1 - back
                acc = acc + w_ref[lt, k:k + 1, :] * buf_ref[lt, src, CONV_HALO - wrap:CONV_HALO - wrap + rows, :]
            acc_ref[lt, r] = acc
        return carry

    lax.fori_loop(0, n_lt, lane_tile, 0)
    for r in range(n_slab):
        c = jnp.concatenate([acc_ref[lt, r] for lt in range(n_lt)], axis=1)
        mu = jnp.mean(c, axis=-1, keepdims=True)
        var = jnp.mean(jnp.square(c - mu), axis=-1, keepdims=True)
        y = (c - mu) * lax.rsqrt(var + EPS) * lg_ref[...] + lb_ref[...]
        o_ref[r] = jax.nn.silu(y).astype(o_ref.dtype)


def _conformer_conv(cv, conv_w, conv_b, ln_g, ln_b, batch, seq):
    cw = CONV_WIDTH
    n_lt = cw // LANES
    si = seq // SLABS
    rows = min(CONV_ROWS, si)
    cv4 = cv.reshape(batch, SLABS, si, 2 * cw)
    hb = rows // CONV_HALO
    cur = lambda c: pl.BlockSpec((None, SLABS, rows, cw), lambda b, i: (b, 0, i, c))
    halo = lambda c: pl.BlockSpec((None, SLABS, CONV_HALO, cw), lambda b, i: (b, 0, jnp.maximum(i * hb - 1, 0), c))
    vec = pl.BlockSpec((1, cw), lambda b, i: (0, 0))
    tiled = lambda n: pl.BlockSpec((n_lt, n, LANES), lambda b, i: (0, 0, 0))
    w_tiles = conv_w.reshape(CONV_K, n_lt, LANES).transpose(1, 0, 2)
    out = pl.pallas_call(
        _conv_kernel,
        grid=(batch, si // rows),
        in_specs=[cur(0), cur(1), halo(0), halo(1), tiled(CONV_K), tiled(1), vec, vec],
        out_specs=pl.BlockSpec((None, SLABS, rows, cw), lambda b, i: (b, 0, i, 0)),
        out_shape=jax.ShapeDtypeStruct((batch, SLABS, si, cw), BF16),
        scratch_shapes=[pltpu.VMEM((n_lt, SLABS, CONV_HALO + rows, LANES), F32),
                        pltpu.VMEM((n_lt, SLABS, rows, LANES), F32)],
        compiler_params=_params("parallel", "arbitrary"),
        name="conformer_conv",
    )(cv4, cv4, cv4, cv4, w_tiles, conv_b.reshape(n_lt, 1, LANES), ln_g.reshape(1, cw), ln_b.reshape(1, cw))
    return out.reshape(batch * seq, cw)


def _merge_kernel(a_ref, y_ref, c_ref, g0_ref, g1_ref, g2_ref, wao_ref, wa_ref, wb_ref, wco_ref, o_ref):
    dot = functools.partial(jnp.dot, preferred_element_type=F32)
    sig = lambda r: jax.nn.sigmoid(r[...].astype(F32))
    y = y_ref[...]
    o_a = dot(a_ref[...], wao_ref[...])
    o_s = dot(y, wa_ref[...]) * jax.nn.sigmoid(dot(y, wb_ref[...]))
    o_c = dot(c_ref[...], wco_ref[...])
    o_ref[...] = (sig(g0_ref) * o_a + sig(g1_ref) * o_s + sig(g2_ref) * o_c).astype(o_ref.dtype)


def _merge(attn, y, conv, gates, w_attn_out, w_glu_a, w_glu_b, w_conv_out, tm=1024, tn=512):
    m = attn.shape[0]
    d = w_attn_out.shape[1]
    nt = d // tn
    tok = lambda a: pl.BlockSpec((tm, a.shape[1]), lambda i, j: (i, 0))
    gate = lambda br: pl.BlockSpec((tm, tn), lambda i, j: (i, br * nt + j))
    wgt = lambda w: pl.BlockSpec((w.shape[0], tn), lambda i, j: (0, j))
    return pl.pallas_call(
        _merge_kernel,
        grid=(m // tm, nt),
        in_specs=[tok(attn), tok(y), tok(conv), gate(0), gate(1), gate(2),
                  wgt(w_attn_out), wgt(w_glu_a), wgt(w_glu_b), wgt(w_conv_out)],
        out_specs=pl.BlockSpec((tm, tn), lambda i, j: (i, j)),
        out_shape=jax.ShapeDtypeStruct((m, d), BF16),
        compiler_params=_params("parallel", "arbitrary"),
        name="merge",
    )(attn, y, conv, gates, gates, gates, w_attn_out, w_glu_a, w_glu_b, w_conv_out)


def _residual_proj_kernel(a_ref, w_ref, x_ref, g_ref, xo_ref, ho_ref, acc_ref):
    k = pl.program_id(1)

    @pl.when(k == 0)
    def _():
        acc_ref[...] = x_ref[...]

    acc_ref[...] += jnp.dot(a_ref[...], w_ref[...], preferred_element_type=F32)

    @pl.when(k == pl.num_programs(1) - 1)
    def _():
        x = acc_ref[...]
        xo_ref[...] = x
        ho_ref[...] = _rmsnorm_rows(x, g_ref[...]).astype(ho_ref.dtype)


def _residual_proj(a, w, x, g, h_dtype, tm, tk, name):
    m, kdim = a.shape
    d = w.shape[1]
    row = lambda i, k: (i, 0)
    return pl.pallas_call(
        _residual_proj_kernel,
        grid=(m // tm, kdim // tk),
        in_specs=[pl.BlockSpec((tm, tk), lambda i, k: (i, k)),
                  pl.BlockSpec((tk, d), lambda i, k: (k, 0)),
                  pl.BlockSpec((tm, d), row),
                  pl.BlockSpec((1, d), lambda i, k: (0, 0))],
        out_specs=[pl.BlockSpec((tm, d), row), pl.BlockSpec((tm, d), row)],
        out_shape=[jax.ShapeDtypeStruct((m, d), F32), jax.ShapeDtypeStruct((m, d), h_dtype)],
        scratch_shapes=[pltpu.VMEM((tm, d), F32)],
        compiler_params=_params("parallel", "arbitrary"),
        name=name,
    )(a, w, x, g.reshape(1, d))


def _ffn_up_kernel(h_ref, wg_ref, wu_ref, o_ref):
    h = h_ref[...]
    gate = jnp.dot(h, wg_ref[...], preferred_element_type=F32)
    up = jnp.dot(h, wu_ref[...], preferred_element_type=F32)
    o_ref[...] = (jax.nn.silu(gate) * up).astype(o_ref.dtype)


def _ffn_up(h, wg, wu, tm=1024, tn=512):
    m, k = h.shape
    n = wg.shape[1]
    wspec = pl.BlockSpec((k, tn), lambda i, j: (0, j))
    return pl.pallas_call(
        _ffn_up_kernel,
        grid=(m // tm, n // tn),
        in_specs=[pl.BlockSpec((tm, k), lambda i, j: (i, 0)), wspec, wspec],
        out_specs=pl.BlockSpec((tm, tn), lambda i, j: (i, j)),
        out_shape=jax.ShapeDtypeStruct((m, n), BF16),
        compiler_params=_params("parallel", "arbitrary"),
        name="ffn_up",
    )(h, wg, wu)


def kernel(x, w_in, rel_bias, lam_re, lam_im, log_dt, b_re, b_im, c_re, c_im, d_skip, w_ssm_glu_a, w_ssm_glu_b, conv_w, conv_b, conv_ln_g, conv_ln_b, w_conv_out, w_attn_out, w_out, norm_mix_g, norm_ffn_g, w_ffn_gate, w_ffn_up, w_ffn_down, norm_final_g):
    batch, seq, d_model = x.shape
    m = batch * seq
    si = seq // SLABS
    xf = x.reshape(batch, si, SLABS, d_model).transpose(0, 2, 1, 3).reshape(m, d_model)
    bands = [_band_bias(rel_bias, g) for g in range(N_GROUPS)]
    q_end, u_end, c_end = 3 * ATTN_QW, 3 * ATTN_QW + SSM_WIDTH, 3 * ATTN_QW + SSM_WIDTH + 2 * CONV_WIDTH

    h = _rmsnorm(xf, norm_mix_g[0], BF16)
    for l in range(DEPTH):
        w_qkv = jnp.concatenate([w_in[l][:, :ATTN_QW] * ATTN_SCALE, w_in[l][:, ATTN_QW:q_end]], axis=1).astype(BF16)
        qkv = _matmul(h, w_qkv, 1024, 1536, "inproj_qkv")
        u = _matmul(h, w_in[l][:, q_end:u_end].astype(BF16), 1024, 1024, "inproj_ssm")
        cv = _matmul(h, w_in[l][:, u_end:c_end].astype(BF16), 1024, 1024, "inproj_conv")
        gates = _matmul(h, w_in[l][:, c_end:].astype(BF16), 1024, 1024, "inproj_gates")

        outs, lses = zip(*[_attention_group(qkv, bands[g], g, batch, seq) for g in range(N_GROUPS)])
        attn = _attn_combine(outs, lses)
        ops = _ssm_operators(lam_re[l], lam_im[l], log_dt[l], b_re[l], b_im[l], c_re[l], c_im[l], d_skip[l])
        y = _ssm(u, ops, batch, seq)
        conv = _conformer_conv(cv, conv_w[l], conv_b[l], conv_ln_g[l], conv_ln_b[l], batch, seq)

        merged = _merge(attn, y, conv, gates, w_attn_out[l].astype(BF16), w_ssm_glu_a[l].astype(BF16),
                        w_ssm_glu_b[l].astype(BF16), w_conv_out[l].astype(BF16))
        xf, h = _residual_proj(merged, w_out[l].astype(BF16), xf, norm_ffn_g[l], BF16, 512, d_model, "out_proj")
        act = _ffn_up(h, w_ffn_gate[l].astype(BF16), w_ffn_up[l].astype(BF16))
        last = l == DEPTH - 1
        g_next = norm_final_g if last else norm_mix_g[l + 1]
        xf, h = _residual_proj(act, w_ffn_down[l].astype(BF16), xf, g_next, F32 if last else BF16,
                               512, D_FF // 4, "ffn_down")
    return h.reshape(batch, SLABS, si, d_model).transpose(0, 2, 1, 3).reshape(batch, seq, d_model)
```

```python
import functools
import math

import jax
import jax.numpy as jnp
import numpy as np
from jax import lax
from jax.experimental import pallas as pl
from jax.experimental.pallas import tpu as pltpu

F32 = jnp.float32
BF16 = jnp.bfloat16

D_MODEL = 2048
DEPTH = 4
HEAD_DIM = 64
HEADS_PER_GROUP = 8
ATTN_GROUPS = ((128, 1), (512, 4), (2048, 16))
N_GROUPS = len(ATTN_GROUPS)
N_ATTN_HEADS = HEADS_PER_GROUP * N_GROUPS
ATTN_QW = N_ATTN_HEADS * HEAD_DIM
ATTN_OUT_W = HEADS_PER_GROUP * HEAD_DIM
ATTN_LAGS = ATTN_GROUPS[0][0] // ATTN_GROUPS[0][1]
NUM_BUCKETS = 32
MAX_DISTANCE = max(w for w, _ in ATTN_GROUPS)
ATTN_SCALE = HEAD_DIM ** -0.5
NEG_INF = -1e30
SSM_WIDTH = 1024
SSM_GROUP = 16
SSM_STATE = 64
CONV_WIDTH = 1024
CONV_K = 31
N_BRANCHES = 3
D_FF = -(-8 * D_MODEL // (3 * 256)) * 256
EPS = 1e-6
COL_U = 3 * ATTN_QW
COL_CONV = COL_U + SSM_WIDTH
COL_GATES = COL_CONV + 2 * CONV_WIDTH

LANES = 128
BF16_ROWS = 16
VMEM_LIMIT = 56 * 2 ** 20

SLABS = max(d for _, d in ATTN_GROUPS)
ATTN_TILES = ((BF16_ROWS, 64), (32, 128), (128, 256))
SSM_CHUNK = SLABS
SSM_BLOCKS = SSM_WIDTH // LANES
SSM_GROUPS_PER_BLOCK = LANES // SSM_GROUP
SSM_STATE_LANES = SSM_GROUPS_PER_BLOCK * SSM_STATE
SSM_BATCH_PER_STEP = 2
CONV_ROWS = 32
CONV_HALO = BF16_ROWS
CONV_COLS = 512


def _params(*sem):
    return pltpu.CompilerParams(dimension_semantics=sem, vmem_limit_bytes=VMEM_LIMIT)


def _rmsnorm_rows(x, g):
    return x * lax.rsqrt(jnp.mean(x * x, axis=-1, keepdims=True) + EPS) * g


def _rmsnorm_kernel(x_ref, g_ref, o_ref):
    o_ref[...] = _rmsnorm_rows(x_ref[...], g_ref[...]).astype(o_ref.dtype)


def _rmsnorm(x, g, out_dtype, tm=512):
    m, d = x.shape
    return pl.pallas_call(
        _rmsnorm_kernel,
        grid=(m // tm,),
        in_specs=[pl.BlockSpec((tm, d), lambda i: (i, 0)),
                  pl.BlockSpec((1, d), lambda i: (0, 0))],
        out_specs=pl.BlockSpec((tm, d), lambda i: (i, 0)),
        out_shape=jax.ShapeDtypeStruct((m, d), out_dtype),
        compiler_params=_params("parallel"),
        name="rmsnorm",
    )(x, g.reshape(1, d))


def _inproj_kernel(a_ref, w_ref, scale_ref, o_ref, wb_ref):
    @pl.when(pl.program_id(1) == 0)
    def _():
        wb_ref[...] = (w_ref[...] * scale_ref[...]).astype(BF16)

    o_ref[...] = jnp.dot(a_ref[...], wb_ref[...], preferred_element_type=F32).astype(o_ref.dtype)


def _inproj(h, w_all, layer, col_scale, tm=1024, tn=768):
    m, k = h.shape
    n = w_all.shape[2]
    return pl.pallas_call(
        _inproj_kernel,
        grid=(n // tn, m // tm),
        in_specs=[pl.BlockSpec((tm, k), lambda j, i: (i, 0)),
                  pl.BlockSpec((None, k, tn), lambda j, i: (layer, 0, j)),
                  pl.BlockSpec((1, tn), lambda j, i: (0, j))],
        out_specs=pl.BlockSpec((tm, tn), lambda j, i: (i, j)),
        out_shape=jax.ShapeDtypeStruct((m, n), BF16),
        scratch_shapes=[pltpu.VMEM((k, tn), BF16)],
        compiler_params=_params("arbitrary", "arbitrary"),
        name="inproj",
    )(h, w_all, col_scale)


def _t5_bucket_causal(dist):
    max_exact = NUM_BUCKETS // 2
    d = np.maximum(dist, 1).astype(np.float32)
    large = max_exact + (np.log(d / max_exact) / math.log(MAX_DISTANCE / max_exact)
                         * (NUM_BUCKETS - max_exact)).astype(np.int32)
    large = np.minimum(large, NUM_BUCKETS - 1)
    return np.where(dist < max_exact, dist, large).astype(np.int32)


def _band_bias(rel_bias, g):
    _, d = ATTN_GROUPS[g]
    n_c = SLABS // d
    ni = ATTN_TILES[g][0]
    qc, qi = np.divmod(np.arange(n_c * ni), ni)
    kc, ki = np.divmod(np.arange(n_c * 2 * ni), 2 * ni)
    lag = n_c * (qi[:, None] - ki[None, :] + ni) + (qc[:, None] - kc[None, :])
    on_band = (lag >= 0) & (lag <= ATTN_LAGS)
    bucket = np.where(on_band, _t5_bucket_causal(np.clip(lag, 0, ATTN_LAGS) * d), -1).astype(np.int32)

    def band_kernel(rel_ref, bucket_ref, out_ref):
        bkt = bucket_ref[...]
        for h in range(HEADS_PER_GROUP):
            acc = jnp.full(bkt.shape, NEG_INF, F32)
            for b in range(NUM_BUCKETS):
                acc = jnp.where(bkt == b, rel_ref[b, g * HEADS_PER_GROUP + h], acc)
            out_ref[h] = acc

    return pl.pallas_call(
        band_kernel,
        in_specs=[pl.BlockSpec(memory_space=pltpu.SMEM), pl.BlockSpec(memory_space=pltpu.VMEM)],
        out_specs=pl.BlockSpec(memory_space=pltpu.VMEM),
        out_shape=jax.ShapeDtypeStruct((HEADS_PER_GROUP,) + bucket.shape, F32),
        name=f"band_bias_g{g}",
    )(rel_bias, jnp.asarray(bucket))


def _attn_kernel(q_ref, kp_ref, kc_ref, vp_ref, vc_ref, bm_ref, o_ref, lse_ref, *, ni):
    tile = pl.program_id(2)
    n_c, rows, _ = q_ref.shape
    win = 2 * ni
    kwin = [jnp.concatenate([kp_ref[c], kc_ref[c]], axis=0) for c in range(n_c)]
    vwin = [jnp.concatenate([vp_ref[c], vc_ref[c]], axis=0) for c in range(n_c)]
    for t in range(rows // ni):
        r0 = t * ni
        q = jnp.concatenate([q_ref[c, r0:r0 + ni, :] for c in range(n_c)], axis=0)
        kk = jnp.concatenate([kwin[c][r0:r0 + win] for c in range(n_c)], axis=0)
        vv = jnp.concatenate([vwin[c][r0:r0 + win] for c in range(n_c)], axis=0)
        for h in range(HEADS_PER_GROUP):
            hs = slice(h * HEAD_DIM, (h + 1) * HEAD_DIM)
            s = lax.dot_general(q[:, hs], kk[:, hs], (((1,), (1,)), ((), ())), preferred_element_type=F32)
            s = s + bm_ref[h]
            if t == 0:
                col = lax.broadcasted_iota(jnp.int32, s.shape, 1)
                s = jnp.where((col % win >= ni) | (tile > 0), s, NEG_INF)
            m = jnp.max(s, axis=-1, keepdims=True)
            p = jnp.exp(s - m)
            den = jnp.sum(p, axis=-1, keepdims=True)
            o = jnp.dot(p.astype(BF16), vv[:, hs], preferred_element_type=F32) * (1.0 / den)
            lse = jnp.broadcast_to(m + jnp.log(den), o.shape)
            for c in range(n_c):
                o_ref[c, r0:r0 + ni, hs] = o[c * ni:(c + 1) * ni].astype(o_ref.dtype)
                lse_ref[c, r0:r0 + ni, hs] = lse[c * ni:(c + 1) * ni]


def _attention_group(z, bm, g, batch, seq):
    _, d = ATTN_GROUPS[g]
    n_c, n_r = SLABS // d, d
    ni, rows = ATTN_TILES[g]
    si = seq // SLABS
    rows = min(rows, si)
    z5 = z.reshape(batch, n_c, n_r, si, z.shape[1])
    kb = ATTN_QW // ATTN_OUT_W
    prev_per_step = rows // ni

    def cur(off):
        return pl.BlockSpec((None, n_c, None, rows, ATTN_OUT_W), lambda b, r, i: (b, 0, r, i, off + g))

    def prev(off):
        return pl.BlockSpec((None, n_c, None, ni, ATTN_OUT_W),
                            lambda b, r, i: (b, 0, r, jnp.maximum(i * prev_per_step - 1, 0), off + g))

    out_spec = pl.BlockSpec((None, n_c, None, rows, ATTN_OUT_W), lambda b, r, i: (b, 0, r, i, 0))
    o, lse = pl.pallas_call(
        functools.partial(_attn_kernel, ni=ni),
        grid=(batch, n_r, si // rows),
        in_specs=[cur(0), prev(kb), cur(kb), prev(2 * kb), cur(2 * kb),
                  pl.BlockSpec(bm.shape, lambda b, r, i: (0, 0, 0))],
        out_specs=[out_spec, out_spec],
        out_shape=[jax.ShapeDtypeStruct((batch, n_c, n_r, si, ATTN_OUT_W), BF16),
                   jax.ShapeDtypeStruct((batch, n_c, n_r, si, ATTN_OUT_W), F32)],
        compiler_params=_params("parallel", "parallel", "arbitrary"),
        name=f"attn_g{g}",
    )(z5, z5, z5, z5, z5, bm)
    return o.reshape(batch * seq, ATTN_OUT_W), lse.reshape(batch * seq, ATTN_OUT_W)


def _attn_combine_kernel(o0, o1, o2, l0, l1, l2, out_ref):
    m = jnp.maximum(jnp.maximum(l0[...], l1[...]), l2[...])
    e0, e1, e2 = jnp.exp(l0[...] - m), jnp.exp(l1[...] - m), jnp.exp(l2[...] - m)
    num = e0 * o0[...].astype(F32) + e1 * o1[...].astype(F32) + e2 * o2[...].astype(F32)
    out_ref[...] = (num / (e0 + e1 + e2)).astype(out_ref.dtype)


def _attn_combine(outs, lses, tm=1024):
    m, w = outs[0].shape
    spec = pl.BlockSpec((tm, w), lambda i: (i, 0))
    return pl.pallas_call(
        _attn_combine_kernel,
        grid=(m // tm,),
        in_specs=[spec] * 6,
        out_specs=spec,
        out_shape=jax.ShapeDtypeStruct((m, w), BF16),
        compiler_params=_params("parallel"),
        name="attn_combine",
    )(*outs, *lses)


def _ssm_ops_kernel(v0r_ref, v0i_ref, cmr_ref, cmi_ref, pwr_ref, pwi_ref, pwrt_ref, pwit_ref,
                    wd_ref, wx_ref, v_ref, z_ref):
    t_len, half = SSM_CHUNK, SSM_CHUNK // 2
    blk = lambda i: slice(i * LANES, (i + 1) * LANES)
    v0r, v0i = v0r_ref[...], v0i_ref[...]
    cmr, cmi = cmr_ref[...], cmi_ref[...]
    cm = jnp.concatenate([cmr, -cmi], axis=0)
    kern = []
    for tau in range(t_len):
        pr, pi = pwr_ref[tau:tau + 1, :], pwi_ref[tau:tau + 1, :]
        vt = jnp.concatenate([v0r * pr - v0i * pi, v0r * pi + v0i * pr], axis=1)
        v_ref[blk(t_len - 1 - tau), :] = vt.astype(v_ref.dtype)
        kern.append(jnp.dot(vt, cm, precision=lax.Precision.HIGHEST, preferred_element_type=F32))
    zero = jnp.zeros((LANES, LANES), F32)
    for j in range(half):
        wd_ref[blk(j), :] = jnp.concatenate([kern[i - j] if i >= j else zero for i in range(half)],
                                            axis=1).astype(wd_ref.dtype)
        wx_ref[blk(j), :] = jnp.concatenate([kern[half + i - j] for i in range(half)], axis=1).astype(wx_ref.dtype)
    ns = SSM_STATE_LANES
    for i in range(t_len):
        pr, pi = pwrt_ref[:, i + 1:i + 2], pwit_ref[:, i + 1:i + 2]
        z_ref[0:ns, blk(i)] = (cmr * pr - cmi * pi).astype(z_ref.dtype)
        z_ref[ns:2 * ns, blk(i)] = (-(cmr * pi + cmi * pr)).astype(z_ref.dtype)


def _ssm_operators(lam_re, lam_im, log_dt, b_re, b_im, c_re, c_im, d_skip):
    t_len = SSM_CHUNK
    n_l, n_g, n_p = lam_re.shape
    nb, gl, ns = SSM_BLOCKS, SSM_GROUPS_PER_BLOCK, SSM_STATE_LANES
    dt = jnp.exp(log_dt)[..., None]
    lr, li = lam_re, lam_im
    mag = jnp.exp(lr * dt)
    ab_re, ab_im = mag * jnp.cos(li * dt), mag * jnp.sin(li * dt)
    nr, ni = ab_re - 1.0, ab_im
    den = lr * lr + li * li
    z_re, z_im = (nr * lr + ni * li) / den, (ni * lr - nr * li) / den
    bb_re = z_re[..., None] * b_re - z_im[..., None] * b_im
    bb_im = z_re[..., None] * b_im + z_im[..., None] * b_re
    tau = jnp.arange(t_len + 1, dtype=F32)[:, None, None, None]
    pmag = jnp.exp(lr * dt * tau)
    pw_re, pw_im = pmag * jnp.cos(li * dt * tau), pmag * jnp.sin(li * dt * tau)

    eye = jnp.eye(gl, dtype=F32)
    nlb = n_l * nb
    seed_in = lambda b: jnp.einsum('xgpc,gh->xgchp', b.reshape(nlb, gl, n_p, SSM_GROUP), eye).reshape(nlb, LANES, ns)
    seed_out = lambda c: jnp.einsum('xgop,gh->xgpho', c.reshape(nlb, gl, SSM_GROUP, n_p), eye).reshape(nlb, ns, LANES)
    pw_lanes = lambda p: p.reshape(t_len + 1, nlb, ns).transpose(1, 0, 2)
    pw_rows = lambda p: p.reshape(t_len + 1, nlb, ns).transpose(1, 2, 0)

    per_blk = lambda *shape: pl.BlockSpec((None,) + shape, lambda x: (x, 0, 0))
    cw = t_len * LANES
    wd, wx, v, z = pl.pallas_call(
        _ssm_ops_kernel,
        grid=(nlb,),
        in_specs=[per_blk(LANES, ns), per_blk(LANES, ns), per_blk(ns, LANES), per_blk(ns, LANES),
                  per_blk(t_len + 1, ns), per_blk(t_len + 1, ns), per_blk(ns, t_len + 1), per_blk(ns, t_len + 1)],
        out_specs=[per_blk(cw // 2, cw // 2), per_blk(cw // 2, cw // 2), per_blk(cw, 2 * ns), per_blk(2 * ns, cw)],
        out_shape=[jax.ShapeDtypeStruct((nlb, cw // 2, cw // 2), BF16),
                   jax.ShapeDtypeStruct((nlb, cw // 2, cw // 2), BF16),
                   jax.ShapeDtypeStruct((nlb, cw, 2 * ns), BF16),
                   jax.ShapeDtypeStruct((nlb, 2 * ns, cw), BF16)],
        compiler_params=_params("parallel"),
        name="ssm_operators",
    )(seed_in(bb_re), seed_in(bb_im), seed_out(c_re), seed_out(c_im),
      pw_lanes(pw_re), pw_lanes(pw_im), pw_rows(pw_re), pw_rows(pw_im))

    apow = jnp.concatenate([pw_re[t_len].reshape(nlb, 1, ns), pw_im[t_len].reshape(nlb, 1, ns)], axis=-1)
    dsk = jnp.tile(d_skip.reshape(nlb, 1, LANES), (1, 1, t_len))
    return wd, wx, v, z, apow, dsk


def _ssm_kernel(u_ref, wd_ref, wx_ref, v_ref, z_ref, ap_ref, dsk_ref, y_ref, s_ref):
    n_batch, n_slab, n_chunks, _ = u_ref.shape
    nt = SSM_STATE_LANES // LANES
    half = n_slab // 2 * LANES
    lane = lambda t: slice(t * LANES, (t + 1) * LANES)
    dot = functools.partial(jnp.dot, preferred_element_type=F32)
    chunk_rows = lambda b: jnp.concatenate([u_ref[b, r] for r in range(n_slab)], axis=1)

    for b in range(n_batch):
        s = dot(chunk_rows(b), v_ref[...])
        for t in range(2 * nt):
            s_ref[t, b * n_chunks:(b + 1) * n_chunks, :] = s[:, lane(t)]

    ar = [jnp.broadcast_to(ap_ref[:, lane(t)], (n_batch, LANES)) for t in range(nt)]
    ai = [jnp.broadcast_to(ap_ref[:, lane(nt + t)], (n_batch, LANES)) for t in range(nt)]

    def step(k, carry):
        idx = pl.ds(k, n_batch, stride=n_chunks)
        new = []
        for t in range(nt):
            xr, xi = carry[t]
            re_ref, im_ref = s_ref.at[t], s_ref.at[nt + t]
            sr, si = re_ref[idx, :], im_ref[idx, :]
            re_ref[idx, :] = xr
            im_ref[idx, :] = xi
            new.append((ar[t] * xr - ai[t] * xi + sr, ar[t] * xi + ai[t] * xr + si))
        return tuple(new)

    zero = jnp.zeros((n_batch, LANES), F32)
    lax.fori_loop(0, n_chunks, step, tuple((zero, zero) for _ in range(nt)))

    for b in range(n_batch):
        u = chunk_rows(b)
        rs = slice(b * n_chunks, (b + 1) * n_chunks)
        state = jnp.concatenate([s_ref[t, rs, :] for t in range(2 * nt)], axis=1).astype(BF16)
        y_lo = dot(u[:, :half], wd_ref[...]) + dot(state, z_ref[:, :half])
        y_hi = dot(u[:, :half], wx_ref[...]) + dot(u[:, half:], wd_ref[...]) + dot(state, z_ref[:, half:])
        y = jnp.concatenate([y_lo, y_hi], axis=1) + dsk_ref[...] * u.astype(F32)
        y = jax.nn.gelu(y).astype(y_ref.dtype)
        for r in range(n_slab):
            y_ref[b, r] = y[:, lane(r)]


def _ssm(z, ops, layer, batch, seq):
    wd, wx, v, zo, apow, dsk = ops
    si = seq // SLABS
    nbs = min(SSM_BATCH_PER_STEP, batch)
    z4 = z.reshape(batch, SLABS, si, z.shape[1])
    col0 = COL_U // LANES
    op = lambda a: pl.BlockSpec((None,) + a.shape[1:], lambda c, b: (layer * SSM_BLOCKS + c, 0, 0))
    y = pl.pallas_call(
        _ssm_kernel,
        grid=(SSM_BLOCKS, batch // nbs),
        in_specs=[pl.BlockSpec((nbs, SLABS, si, LANES), lambda c, b: (b, 0, 0, col0 + c)),
                  op(wd), op(wx), op(v), op(zo), op(apow), op(dsk)],
        out_specs=pl.BlockSpec((nbs, SLABS, si, LANES), lambda c, b: (b, 0, 0, c)),
        out_shape=jax.ShapeDtypeStruct((batch, SLABS, si, SSM_WIDTH), BF16),
        scratch_shapes=[pltpu.VMEM((2 * SSM_STATE_LANES // LANES, nbs * si, LANES), F32)],
        compiler_params=_params("parallel", "arbitrary"),
        name="ssm",
    )(z4, wd, wx, v, zo, apow, dsk)
    return y.reshape(batch * seq, SSM_WIDTH)


def _conv_kernel(*refs, n_col):
    vc, gc = refs[0:n_col], refs[n_col:2 * n_col]
    vh, gh = refs[2 * n_col:3 * n_col], refs[3 * n_col:4 * n_col]
    w_ref, cb_ref, lg_ref, lb_ref, o_ref, buf_ref, acc_ref = refs[4 * n_col:]
    tile = pl.program_id(1)
    n_slab, rows, cols = vc[0].shape
    lt_per_col = cols // LANES
    n_lt = n_col * lt_per_col
    lane = lambda t: slice(t * LANES, (t + 1) * LANES)
    for r in range(n_slab):
        for c in range(n_col):
            halo = vh[c][r].astype(F32) * jax.nn.sigmoid(gh[c][r].astype(F32))
            halo = jnp.where(tile > 0, halo, 0.0)
            glu = vc[c][r].astype(F32) * jax.nn.sigmoid(gc[c][r].astype(F32))
            for t in range(lt_per_col):
                buf_ref[c * lt_per_col + t, r, 0:CONV_HALO, :] = halo[:, lane(t)]
                buf_ref[c * lt_per_col + t, r, CONV_HALO:, :] = glu[:, lane(t)]

    def lane_tile(lt, carry):
        for r in range(n_slab):
            acc = jnp.broadcast_to(cb_ref[lt], (rows, LANES))
            for back in range(CONV_K):
                src = (r - back) % n_slab
                wrap = (back - r + n_slab - 1) // n_slab
                k = CONV_K - 1 - back
                acc = acc + w_ref[lt, k:k + 1, :] * buf_ref[lt, src, CONV_HALO - wrap:CONV_HALO - wrap + rows, :]
            acc_ref[lt, r] = acc
        return carry

    lax.fori_loop(0, n_lt, lane_tile, 0)
    for r in range(n_slab):
        c = jnp.concatenate([acc_ref[lt, r] for lt in range(n_lt)], axis=1)
        mu = jnp.mean(c, axis=-1, keepdims=True)
        var = jnp.mean(jnp.square(c - mu), axis=-1, keepdims=True)
        y = (c - mu) * lax.rsqrt(var + EPS) * lg_ref[...] + lb_ref[...]
        o_ref[r] = jax.nn.silu(y).astype(o_ref.dtype)


def _conformer_conv(z, conv_w, conv_b, ln_g, ln_b, batch, seq):
    cw = CONV_WIDTH
    n_lt = cw // LANES
    n_col = cw // CONV_COLS
    si = seq // SLABS
    rows = min(CONV_ROWS, si)
    z4 = z.reshape(batch, SLABS, si, z.shape[1])
    hb = rows // CONV_HALO
    col0 = COL_CONV // CONV_COLS
    cur = lambda c: pl.BlockSpec((None, SLABS, rows, CONV_COLS), lambda b, i: (b, 0, i, col0 + c))
    halo = lambda c: pl.BlockSpec((None, SLABS, CONV_HALO, CONV_COLS),
                                  lambda b, i: (b, 0, jnp.maximum(i * hb - 1, 0), col0 + c))
    vec = pl.BlockSpec((1, cw), lambda b, i: (0, 0))
    tiled = lambda n: pl.BlockSpec((n_lt, n, LANES), lambda b, i: (0, 0, 0))
    w_tiles = conv_w.reshape(CONV_K, n_lt, LANES).transpose(1, 0, 2)
    cols = range(2 * n_col)
    out = pl.pallas_call(
        functools.partial(_conv_kernel, n_col=n_col),
        grid=(batch, si // rows),
        in_specs=[cur(c) for c in cols] + [halo(c) for c in cols] + [tiled(CONV_K), tiled(1), vec, vec],
        out_specs=pl.BlockSpec((None, SLABS, rows, cw), lambda b, i: (b, 0, i, 0)),
        out_shape=jax.ShapeDtypeStruct((batch, SLABS, si, cw), BF16),
        scratch_shapes=[pltpu.VMEM((n_lt, SLABS, CONV_HALO + rows, LANES), F32),
                        pltpu.VMEM((n_lt, SLABS, rows, LANES), F32)],
        compiler_params=_params("parallel", "arbitrary"),
        name="conformer_conv",
    )(*([z4] * (4 * n_col)), w_tiles, conv_b.reshape(n_lt, 1, LANES), ln_g.reshape(1, cw), ln_b.reshape(1, cw))
    return out.reshape(batch * seq, cw)


def _merge_kernel(a_ref, y_ref, c_ref, g0_ref, g1_ref, g2_ref, wao_ref, wa_ref, wb_ref, wco_ref, o_ref):
    dot = functools.partial(jnp.dot, preferred_element_type=F32)
    sig = lambda r: jax.nn.sigmoid(r[...].astype(F32))
    y = y_ref[...]
    o_a = dot(a_ref[...], wao_ref[...])
    o_s = dot(y, wa_ref[...]) * jax.nn.sigmoid(dot(y, wb_ref[...]))
    o_c = dot(c_ref[...], wco_ref[...])
    o_ref[...] = (sig(g0_ref) * o_a + sig(g1_ref) * o_s + sig(g2_ref) * o_c).astype(o_ref.dtype)


def _merge(attn, y, conv, z, w_attn_out, w_glu_a, w_glu_b, w_conv_out, tm=1024, tn=512):
    m = attn.shape[0]
    d = w_attn_out.shape[1]
    nt = d // tn
    col0 = COL_GATES // tn
    tok = lambda a: pl.BlockSpec((tm, a.shape[1]), lambda i, j: (i, 0))
    gate = lambda br: pl.BlockSpec((tm, tn), lambda i, j: (i, col0 + br * nt + j))
    wgt = lambda w: pl.BlockSpec((w.shape[0], tn), lambda i, j: (0, j))
    return pl.pallas_call(
        _merge_kernel,
        grid=(m // tm, nt),
        in_specs=[tok(attn), tok(y), tok(conv), gate(0), gate(1), gate(2),
                  wgt(w_attn_out), wgt(w_glu_a), wgt(w_glu_b), wgt(w_conv_out)],
        out_specs=pl.BlockSpec((tm, tn), lambda i, j: (i, j)),
        out_shape=jax.ShapeDtypeStruct((m, d), BF16),
        compiler_params=_params("parallel", "arbitrary"),
        name="merge",
    )(attn, y, conv, z, z, z, w_attn_out, w_glu_a, w_glu_b, w_conv_out)


def _residual_proj_kernel(a_ref, w_ref, x_ref, g_ref, xo_ref, ho_ref, acc_ref):
    k = pl.program_id(1)

    @pl.when(k == 0)
    def _():
        acc_ref[...] = x_ref[...]

    acc_ref[...] += jnp.dot(a_ref[...], w_ref[...], preferred_element_type=F32)

    @pl.when(k == pl.num_programs(1) - 1)
    def _():
        x = acc_ref[...]
        xo_ref[...] = x
        ho_ref[...] = _rmsnorm_rows(x, g_ref[...]).astype(ho_ref.dtype)


def _residual_proj(a, w, x, g, h_dtype, tm, tk, name):
    m, kdim = a.shape
    d = w.shape[1]
    row = lambda i, k: (i, 0)
    return pl.pallas_call(
        _residual_proj_kernel,
        grid=(m // tm, kdim // tk),
        in_specs=[pl.BlockSpec((tm, tk), lambda i, k: (i, k)),
                  pl.BlockSpec((tk, d), lambda i, k: (k, 0)),
                  pl.BlockSpec((tm, d), row),
                  pl.BlockSpec((1, d), lambda i, k: (0, 0))],
        out_specs=[pl.BlockSpec((tm, d), row), pl.BlockSpec((tm, d), row)],
        out_shape=[jax.ShapeDtypeStruct((m, d), F32), jax.ShapeDtypeStruct((m, d), h_dtype)],
        scratch_shapes=[pltpu.VMEM((tm, d), F32)],
        compiler_params=_params("parallel", "arbitrary"),
        name=name,
    )(a, w, x, g.reshape(1, d))


def _ffn_up_kernel(h_ref, wg_ref, wu_ref, o_ref, wgb_ref, wub_ref):
    @pl.when(pl.program_id(1) == 0)
    def _():
        wgb_ref[...] = wg_ref[...].astype(BF16)
        wub_ref[...] = wu_ref[...].astype(BF16)

    h = h_ref[...]
    gate = jnp.dot(h, wgb_ref[...], preferred_element_type=F32)
    up = jnp.dot(h, wub_ref[...], preferred_element_type=F32)
    o_ref[...] = (jax.nn.silu(gate) * up).astype(o_ref.dtype)


def _ffn_up(h, wg_all, wu_all, layer, tm=1024, tn=512):
    m, k = h.shape
    n = wg_all.shape[2]
    wspec = pl.BlockSpec((None, k, tn), lambda j, i: (layer, 0, j))
    return pl.pallas_call(
        _ffn_up_kernel,
        grid=(n // tn, m // tm),
        in_specs=[pl.BlockSpec((tm, k), lambda j, i: (i, 0)), wspec, wspec],
        out_specs=pl.BlockSpec((tm, tn), lambda j, i: (i, j)),
        out_shape=jax.ShapeDtypeStruct((m, n), BF16),
        scratch_shapes=[pltpu.VMEM((k, tn), BF16), pltpu.VMEM((k, tn), BF16)],
        compiler_params=_params("arbitrary", "arbitrary"),
        name="ffn_up",
    )(h, wg_all, wu_all)


def kernel(x, w_in, rel_bias, lam_re, lam_im, log_dt, b_re, b_im, c_re, c_im, d_skip, w_ssm_glu_a, w_ssm_glu_b, conv_w, conv_b, conv_ln_g, conv_ln_b, w_conv_out, w_attn_out, w_out, norm_mix_g, norm_ffn_g, w_ffn_gate, w_ffn_up, w_ffn_down, norm_final_g):
    batch, seq, d_model = x.shape
    m = batch * seq
    si = seq // SLABS
    xf = x.reshape(batch, si, SLABS, d_model).transpose(0, 2, 1, 3).reshape(m, d_model)
    bands = [_band_bias(rel_bias, g) for g in range(N_GROUPS)]
    ssm_ops = _ssm_operators(lam_re, lam_im, log_dt, b_re, b_im, c_re, c_im, d_skip)
    col_scale = jnp.where(jnp.arange(w_in.shape[2]) < ATTN_QW, ATTN_SCALE, 1.0).astype(F32)[None, :]
    bf = lambda w: w.astype(BF16)
    w_attn_out, w_ssm_glu_a, w_ssm_glu_b, w_conv_out = bf(w_attn_out), bf(w_ssm_glu_a), bf(w_ssm_glu_b), bf(w_conv_out)
    w_out, w_ffn_down = bf(w_out), bf(w_ffn_down)

    h = _rmsnorm(xf, norm_mix_g[0], BF16)
    for l in range(DEPTH):
        z = _inproj(h, w_in, l, col_scale)
        outs, lses = zip(*[_attention_group(z, bands[g], g, batch, seq) for g in range(N_GROUPS)])
        attn = _attn_combine(outs, lses)
        y = _ssm(z, ssm_ops, l, batch, seq)
        conv = _conformer_conv(z, conv_w[l], conv_b[l], conv_ln_g[l], conv_ln_b[l], batch, seq)
        merged = _merge(attn, y, conv, z, w_attn_out[l], w_ssm_glu_a[l], w_ssm_glu_b[l], w_conv_out[l])
        xf, h = _residual_proj(merged, w_out[l], xf, norm_ffn_g[l], BF16, 512, d_model, "out_proj")
        act = _ffn_up(h, w_ffn_gate, w_ffn_up, l)
        last = l == DEPTH - 1
        g_next = norm_final_g if last else norm_mix_g[l + 1]
        xf, h = _residual_proj(act, w_ffn_down[l], xf, g_next, F32 if last else BF16, 512, D_FF // 4, "ffn_down")
    return h.reshape(batch, SLABS, si, d_model).transpose(0, 2, 1, 3).reshape(batch, seq, d_model)
```

```python
import functools
import math

import jax
import jax.numpy as jnp
import numpy as np
from jax import lax
from jax.experimental import pallas as pl
from jax.experimental.pallas import tpu as pltpu

F32 = jnp.float32
BF16 = jnp.bfloat16

D_MODEL = 2048
DEPTH = 4
HEAD_DIM = 64
HEADS_PER_GROUP = 8
ATTN_GROUPS = ((128, 1), (512, 4), (2048, 16))
N_GROUPS = len(ATTN_GROUPS)
N_ATTN_HEADS = HEADS_PER_GROUP * N_GROUPS
ATTN_QW = N_ATTN_HEADS * HEAD_DIM
ATTN_OUT_W = HEADS_PER_GROUP * HEAD_DIM
ATTN_LAGS = ATTN_GROUPS[0][0] // ATTN_GROUPS[0][1]
NUM_BUCKETS = 32
MAX_DISTANCE = max(w for w, _ in ATTN_GROUPS)
ATTN_SCALE = HEAD_DIM ** -0.5
NEG_INF = -1e30
SSM_WIDTH = 1024
SSM_GROUP = 16
SSM_STATE = 64
CONV_WIDTH = 1024
CONV_K = 31
N_BRANCHES = 3
D_FF = -(-8 * D_MODEL // (3 * 256)) * 256
EPS = 1e-6
COL_U = 3 * ATTN_QW
COL_CONV = COL_U + SSM_WIDTH
COL_GATES = COL_CONV + 2 * CONV_WIDTH

LANES = 128
BF16_ROWS = 16
VMEM_LIMIT = 56 * 2 ** 20

SLABS = max(d for _, d in ATTN_GROUPS)
ATTN_TILES = ((8, 8, 64, 1), (32, 32, 128, 2), (128, 128, 256, 4))
HEAD_PAIR_W = 2 * HEAD_DIM
SSM_CHUNK = SLABS
SSM_BLOCKS = SSM_WIDTH // LANES
SSM_GROUPS_PER_BLOCK = LANES // SSM_GROUP
SSM_STATE_LANES = SSM_GROUPS_PER_BLOCK * SSM_STATE
SSM_BATCH_PER_STEP = 2
CONV_ROWS = 32
CONV_HALO = BF16_ROWS
CONV_COLS = 512


def _params(*sem):
    return pltpu.CompilerParams(dimension_semantics=sem, vmem_limit_bytes=VMEM_LIMIT)


def _rmsnorm_rows(x, g):
    return x * lax.rsqrt(jnp.mean(x * x, axis=-1, keepdims=True) + EPS) * g


def _rmsnorm_kernel(x_ref, g_ref, o_ref):
    o_ref[...] = _rmsnorm_rows(x_ref[...], g_ref[...]).astype(o_ref.dtype)


def _rmsnorm(x, g, out_dtype, tm=512):
    m, d = x.shape
    return pl.pallas_call(
        _rmsnorm_kernel,
        grid=(m // tm,),
        in_specs=[pl.BlockSpec((tm, d), lambda i: (i, 0)),
                  pl.BlockSpec((1, d), lambda i: (0, 0))],
        out_specs=pl.BlockSpec((tm, d), lambda i: (i, 0)),
        out_shape=jax.ShapeDtypeStruct((m, d), out_dtype),
        compiler_params=_params("parallel"),
        name="rmsnorm",
    )(x, g.reshape(1, d))


def _inproj_kernel(a_ref, w_ref, scale_ref, o_ref, wb_ref):
    @pl.when(pl.program_id(1) == 0)
    def _():
        wb_ref[...] = (w_ref[...] * scale_ref[...]).astype(BF16)

    o_ref[...] = jnp.dot(a_ref[...], wb_ref[...], preferred_element_type=F32).astype(o_ref.dtype)


def _inproj(h, w_all, layer, col_scale, tm=1024, tn=1536):
    m, k = h.shape
    n = w_all.shape[2]
    return pl.pallas_call(
        _inproj_kernel,
        grid=(n // tn, m // tm),
        in_specs=[pl.BlockSpec((tm, k), lambda j, i: (i, 0)),
                  pl.BlockSpec((None, k, tn), lambda j, i: (layer, 0, j)),
                  pl.BlockSpec((1, tn), lambda j, i: (0, j))],
        out_specs=pl.BlockSpec((tm, tn), lambda j, i: (i, j)),
        out_shape=jax.ShapeDtypeStruct((m, n), BF16),
        scratch_shapes=[pltpu.VMEM((k, tn), BF16)],
        compiler_params=_params("arbitrary", "arbitrary"),
        name="inproj",
    )(h, w_all, col_scale)


def _t5_bucket_causal(dist):
    max_exact = NUM_BUCKETS // 2
    d = np.maximum(dist, 1).astype(np.float32)
    large = max_exact + (np.log(d / max_exact) / math.log(MAX_DISTANCE / max_exact)
                         * (NUM_BUCKETS - max_exact)).astype(np.int32)
    large = np.minimum(large, NUM_BUCKETS - 1)
    return np.where(dist < max_exact, dist, large).astype(np.int32)


def _band_bias(rel_bias, g):
    _, d = ATTN_GROUPS[g]
    n_c = SLABS // d
    ni, n_prev = ATTN_TILES[g][:2]
    qc, qi = np.divmod(np.arange(n_c * ni), ni)
    kc, ki = np.divmod(np.arange(n_c * (n_prev + ni)), n_prev + ni)
    lag = n_c * (qi[:, None] - ki[None, :] + n_prev) + (qc[:, None] - kc[None, :])
    on_band = (lag >= 0) & (lag <= ATTN_LAGS)
    bucket = np.where(on_band, _t5_bucket_causal(np.clip(lag, 0, ATTN_LAGS) * d), -1).astype(np.int32)

    def band_kernel(rel_ref, bucket_ref, out_ref):
        bkt = bucket_ref[...]
        for h in range(HEADS_PER_GROUP):
            acc = jnp.full(bkt.shape, NEG_INF, F32)
            for b in range(NUM_BUCKETS):
                acc = jnp.where(bkt == b, rel_ref[b, g * HEADS_PER_GROUP + h], acc)
            out_ref[h] = acc

    return pl.pallas_call(
        band_kernel,
        in_specs=[pl.BlockSpec(memory_space=pltpu.SMEM), pl.BlockSpec(memory_space=pltpu.VMEM)],
        out_specs=pl.BlockSpec(memory_space=pltpu.VMEM),
        out_shape=jax.ShapeDtypeStruct((HEADS_PER_GROUP,) + bucket.shape, F32),
        name=f"band_bias_g{g}",
    )(rel_bias, jnp.asarray(bucket))


def _attn_kernel(q_ref, kp_ref, kc_ref, vp_ref, vc_ref, bm_ref, o_ref, lse_ref, *, ni, n_prev):
    tile = pl.program_id(2)
    n_c, n_cls, rows, _ = q_ref.shape
    hist = kp_ref.shape[2]
    win = n_prev + ni
    n_pair = HEADS_PER_GROUP // 2
    pair = lambda hp: slice(hp * HEAD_PAIR_W, (hp + 1) * HEAD_PAIR_W)
    stage = BF16 if ni % BF16_ROWS == 0 else F32
    per_store = max(1, BF16_ROWS // ni)
    upper = lax.broadcasted_iota(jnp.int32, (1, HEAD_PAIR_W), 1) >= HEAD_DIM
    nt_dims = (((1,), (1,)), ((), ()))

    def sub_tile(qs, kwin, vwin, t):
        r0, k0 = t * ni, hist - n_prev + t * ni
        q = jnp.concatenate([qs[c][r0:r0 + ni] for c in range(n_c)], axis=0).astype(BF16)
        kk = jnp.concatenate([kwin[c][k0:k0 + win] for c in range(n_c)], axis=0).astype(BF16)
        vv = jnp.concatenate([vwin[c][k0:k0 + win] for c in range(n_c)], axis=0).astype(BF16)
        logits = []
        for h in range(HEADS_PER_GROUP):
            q2 = q[:, pair(h // 2)]
            qm = jnp.where(upper if h % 2 else ~upper, q2, jnp.zeros_like(q2))
            s = lax.dot_general(qm, kk[:, pair(h // 2)], nt_dims, preferred_element_type=F32) + bm_ref[h]
            if t == 0:
                col = lax.broadcasted_iota(jnp.int32, s.shape, 1)
                s = jnp.where((col % win >= n_prev) | (tile > 0), s, NEG_INF)
            logits.append(s)
        mx = [jnp.max(s, axis=-1, keepdims=True) for s in logits]
        probs = [jnp.exp(s - m).astype(BF16) for s, m in zip(logits, mx)]
        out = []
        for hp in range(n_pair):
            v2 = vv[:, pair(hp)]
            one = jnp.ones_like(v2)
            o0 = jnp.dot(probs[2 * hp], jnp.where(upper, one, v2), preferred_element_type=F32)
            o1 = jnp.dot(probs[2 * hp + 1], jnp.where(upper, v2, one), preferred_element_type=F32)
            den = pltpu.roll(jnp.where(upper, o0, o1), HEAD_DIM, axis=1)
            out.append((jnp.where(upper, o1, o0) * (1.0 / den),
                        jnp.where(upper, mx[2 * hp + 1], mx[2 * hp]) + jnp.log(den)))
        return out

    for cls in range(n_cls):
        qs = [q_ref[c, cls].astype(stage) for c in range(n_c)]
        kwin = [jnp.concatenate([kp_ref[c, cls], kc_ref[c, cls]], axis=0).astype(stage) for c in range(n_c)]
        vwin = [jnp.concatenate([vp_ref[c, cls], vc_ref[c, cls]], axis=0).astype(stage) for c in range(n_c)]
        for t0 in range(0, rows // ni, per_store):
            done = [sub_tile(qs, kwin, vwin, t) for t in range(t0, t0 + per_store)]
            rs = slice(t0 * ni, (t0 + per_store) * ni)
            for hp in range(n_pair):
                for c in range(n_c):
                    cs = slice(c * ni, (c + 1) * ni)
                    o_ref[c, cls, rs, pair(hp)] = jnp.concatenate([d[hp][0][cs] for d in done],
                                                                  axis=0).astype(o_ref.dtype)
                    lse_ref[c, cls, rs, pair(hp)] = jnp.concatenate([d[hp][1][cs] for d in done], axis=0)


def _attention_group(z, bm, g, batch, seq):
    _, d = ATTN_GROUPS[g]
    n_c, n_r = SLABS // d, d
    ni, n_prev, rows, n_cls = ATTN_TILES[g]
    si = seq // SLABS
    rows = min(rows, si)
    z5 = z.reshape(batch, n_c, n_r, si, z.shape[1])
    kb = ATTN_QW // ATTN_OUT_W
    hist = max(n_prev, BF16_ROWS)
    prev_per_step = rows // hist

    def cur(off):
        return pl.BlockSpec((None, n_c, n_cls, rows, ATTN_OUT_W), lambda b, r, i: (b, 0, r, i, off + g))

    def prev(off):
        return pl.BlockSpec((None, n_c, n_cls, hist, ATTN_OUT_W),
                            lambda b, r, i: (b, 0, r, jnp.maximum(i * prev_per_step - 1, 0), off + g))

    out_spec = pl.BlockSpec((None, n_c, n_cls, rows, ATTN_OUT_W), lambda b, r, i: (b, 0, r, i, 0))
    o, lse = pl.pallas_call(
        functools.partial(_attn_kernel, ni=ni, n_prev=n_prev),
        grid=(batch, n_r // n_cls, si // rows),
        in_specs=[cur(0), prev(kb), cur(kb), prev(2 * kb), cur(2 * kb),
                  pl.BlockSpec(bm.shape, lambda b, r, i: (0, 0, 0))],
        out_specs=[out_spec, out_spec],
        out_shape=[jax.ShapeDtypeStruct((batch, n_c, n_r, si, ATTN_OUT_W), BF16),
                   jax.ShapeDtypeStruct((batch, n_c, n_r, si, ATTN_OUT_W), F32)],
        compiler_params=_params("parallel", "parallel", "arbitrary"),
        name=f"attn_g{g}",
    )(z5, z5, z5, z5, z5, bm)
    return o.reshape(batch * seq, ATTN_OUT_W), lse.reshape(batch * seq, ATTN_OUT_W)


def _attn_combine_kernel(o0, o1, o2, l0, l1, l2, out_ref):
    m = jnp.maximum(jnp.maximum(l0[...], l1[...]), l2[...])
    e0, e1, e2 = jnp.exp(l0[...] - m), jnp.exp(l1[...] - m), jnp.exp(l2[...] - m)
    num = e0 * o0[...].astype(F32) + e1 * o1[...].astype(F32) + e2 * o2[...].astype(F32)
    out_ref[...] = (num / (e0 + e1 + e2)).astype(out_ref.dtype)


def _attn_combine(outs, lses, tm=1024):
    m, w = outs[0].shape
    spec = pl.BlockSpec((tm, w), lambda i: (i, 0))
    return pl.pallas_call(
        _attn_combine_kernel,
        grid=(m // tm,),
        in_specs=[spec] * 6,
        out_specs=spec,
        out_shape=jax.ShapeDtypeStruct((m, w), BF16),
        compiler_params=_params("parallel"),
        name="attn_combine",
    )(*outs, *lses)


def _ssm_ops_kernel(v0r_ref, v0i_ref, cmr_ref, cmi_ref, pwr_ref, pwi_ref, pwrt_ref, pwit_ref,
                    wd_ref, wx_ref, v_ref, z_ref):
    t_len, half = SSM_CHUNK, SSM_CHUNK // 2
    blk = lambda i: slice(i * LANES, (i + 1) * LANES)
    v0r, v0i = v0r_ref[...], v0i_ref[...]
    cmr, cmi = cmr_ref[...], cmi_ref[...]
    cm = jnp.concatenate([cmr, -cmi], axis=0)
    kern = []
    for tau in range(t_len):
        pr, pi = pwr_ref[tau:tau + 1, :], pwi_ref[tau:tau + 1, :]
        vt = jnp.concatenate([v0r * pr - v0i * pi, v0r * pi + v0i * pr], axis=1)
        v_ref[blk(t_len - 1 - tau), :] = vt.astype(v_ref.dtype)
        kern.append(jnp.dot(vt, cm, precision=lax.Precision.HIGHEST, preferred_element_type=F32))
    zero = jnp.zeros((LANES, LANES), F32)
    for j in range(half):
        wd_ref[blk(j), :] = jnp.concatenate([kern[i - j] if i >= j else zero for i in range(half)],
                                            axis=1).astype(wd_ref.dtype)
        wx_ref[blk(j), :] = jnp.concatenate([kern[half + i - j] for i in range(half)], axis=1).astype(wx_ref.dtype)
    ns = SSM_STATE_LANES
    for i in range(t_len):
        pr, pi = pwrt_ref[:, i + 1:i + 2], pwit_ref[:, i + 1:i + 2]
        z_ref[0:ns, blk(i)] = (cmr * pr - cmi * pi).astype(z_ref.dtype)
        z_ref[ns:2 * ns, blk(i)] = (-(cmr * pi + cmi * pr)).astype(z_ref.dtype)


def _ssm_operators(lam_re, lam_im, log_dt, b_re, b_im, c_re, c_im, d_skip):
    t_len = SSM_CHUNK
    n_l, n_g, n_p = lam_re.shape
    nb, gl, ns = SSM_BLOCKS, SSM_GROUPS_PER_BLOCK, SSM_STATE_LANES
    dt = jnp.exp(log_dt)[..., None]
    lr, li = lam_re, lam_im
    mag = jnp.exp(lr * dt)
    ab_re, ab_im = mag * jnp.cos(li * dt), mag * jnp.sin(li * dt)
    nr, ni = ab_re - 1.0, ab_im
    den = lr * lr + li * li
    z_re, z_im = (nr * lr + ni * li) / den, (ni * lr - nr * li) / den
    bb_re = z_re[..., None] * b_re - z_im[..., None] * b_im
    bb_im = z_re[..., None] * b_im + z_im[..., None] * b_re
    tau = jnp.arange(t_len + 1, dtype=F32)[:, None, None, None]
    pmag = jnp.exp(lr * dt * tau)
    pw_re, pw_im = pmag * jnp.cos(li * dt * tau), pmag * jnp.sin(li * dt * tau)

    eye = jnp.eye(gl, dtype=F32)
    nlb = n_l * nb
    seed_in = lambda b: jnp.einsum('xgpc,gh->xgchp', b.reshape(nlb, gl, n_p, SSM_GROUP), eye).reshape(nlb, LANES, ns)
    seed_out = lambda c: jnp.einsum('xgop,gh->xgpho', c.reshape(nlb, gl, SSM_GROUP, n_p), eye).reshape(nlb, ns, LANES)
    pw_lanes = lambda p: p.reshape(t_len + 1, nlb, ns).transpose(1, 0, 2)
    pw_rows = lambda p: p.reshape(t_len + 1, nlb, ns).transpose(1, 2, 0)

    per_blk = lambda *shape: pl.BlockSpec((None,) + shape, lambda x: (x, 0, 0))
    cw = t_len * LANES
    wd, wx, v, z = pl.pallas_call(
        _ssm_ops_kernel,
        grid=(nlb,),
        in_specs=[per_blk(LANES, ns), per_blk(LANES, ns), per_blk(ns, LANES), per_blk(ns, LANES),
                  per_blk(t_len + 1, ns), per_blk(t_len + 1, ns), per_blk(ns, t_len + 1), per_blk(ns, t_len + 1)],
        out_specs=[per_blk(cw // 2, cw // 2), per_blk(cw // 2, cw // 2), per_blk(cw, 2 * ns), per_blk(2 * ns, cw)],
        out_shape=[jax.ShapeDtypeStruct((nlb, cw // 2, cw // 2), BF16),
                   jax.ShapeDtypeStruct((nlb, cw // 2, cw // 2), BF16),
                   jax.ShapeDtypeStruct((nlb, cw, 2 * ns), BF16),
                   jax.ShapeDtypeStruct((nlb, 2 * ns, cw), BF16)],
        compiler_params=_params("parallel"),
        name="ssm_operators",
    )(seed_in(bb_re), seed_in(bb_im), seed_out(c_re), seed_out(c_im),
      pw_lanes(pw_re), pw_lanes(pw_im), pw_rows(pw_re), pw_rows(pw_im))

    apow = jnp.concatenate([pw_re[t_len].reshape(nlb, 1, ns), pw_im[t_len].reshape(nlb, 1, ns)], axis=-1)
    dsk = jnp.tile(d_skip.reshape(nlb, 1, LANES), (1, 1, t_len))
    return wd, wx, v, z, apow, dsk


def _ssm_kernel(u_ref, wd_ref, wx_ref, v_ref, z_ref, ap_ref, dsk_ref, y_ref, s_ref):
    n_batch, n_slab, n_chunks, _ = u_ref.shape
    nt = SSM_STATE_LANES // LANES
    half = n_slab // 2 * LANES
    lane = lambda t: slice(t * LANES, (t + 1) * LANES)
    dot = functools.partial(jnp.dot, preferred_element_type=F32)
    chunk_rows = lambda b: jnp.concatenate([u_ref[b, r] for r in range(n_slab)], axis=1)

    for b in range(n_batch):
        s = dot(chunk_rows(b), v_ref[...])
        for t in range(2 * nt):
            s_ref[t, b * n_chunks:(b + 1) * n_chunks, :] = s[:, lane(t)]

    ar = [jnp.broadcast_to(ap_ref[:, lane(t)], (n_batch, LANES)) for t in range(nt)]
    ai = [jnp.broadcast_to(ap_ref[:, lane(nt + t)], (n_batch, LANES)) for t in range(nt)]

    def step(k, carry):
        idx = pl.ds(k, n_batch, stride=n_chunks)
        new = []
        for t in range(nt):
            xr, xi = carry[t]
            re_ref, im_ref = s_ref.at[t], s_ref.at[nt + t]
            sr, si = re_ref[idx, :], im_ref[idx, :]
            re_ref[idx, :] = xr
            im_ref[idx, :] = xi
            new.append((ar[t] * xr - ai[t] * xi + sr, ar[t] * xi + ai[t] * xr + si))
        return tuple(new)

    zero = jnp.zeros((n_batch, LANES), F32)
    lax.fori_loop(0, n_chunks, step, tuple((zero, zero) for _ in range(nt)))

    for b in range(n_batch):
        u = chunk_rows(b)
        rs = slice(b * n_chunks, (b + 1) * n_chunks)
        state = jnp.concatenate([s_ref[t, rs, :] for t in range(2 * nt)], axis=1).astype(BF16)
        y_lo = dot(u[:, :half], wd_ref[...]) + dot(state, z_ref[:, :half])
        y_hi = dot(u[:, :half], wx_ref[...]) + dot(u[:, half:], wd_ref[...]) + dot(state, z_ref[:, half:])
        y = jnp.concatenate([y_lo, y_hi], axis=1) + dsk_ref[...] * u.astype(F32)
        y = jax.nn.gelu(y).astype(y_ref.dtype)
        for r in range(n_slab):
            y_ref[b, r] = y[:, lane(r)]


def _ssm(z, ops, layer, batch, seq):
    wd, wx, v, zo, apow, dsk = ops
    si = seq // SLABS
    nbs = min(SSM_BATCH_PER_STEP, batch)
    z4 = z.reshape(batch, SLABS, si, z.shape[1])
    col0 = COL_U // LANES
    op = lambda a: pl.BlockSpec((None,) + a.shape[1:], lambda c, b: (layer * SSM_BLOCKS + c, 0, 0))
    y = pl.pallas_call(
        _ssm_kernel,
        grid=(SSM_BLOCKS, batch // nbs),
        in_specs=[pl.BlockSpec((nbs, SLABS, si, LANES), lambda c, b: (b, 0, 0, col0 + c)),
                  op(wd), op(wx), op(v), op(zo), op(apow), op(dsk)],
        out_specs=pl.BlockSpec((nbs, SLABS, si, LANES), lambda c, b: (b, 0, 0, c)),
        out_shape=jax.ShapeDtypeStruct((batch, SLABS, si, SSM_WIDTH), BF16),
        scratch_shapes=[pltpu.VMEM((2 * SSM_STATE_LANES // LANES, nbs * si, LANES), F32)],
        compiler_params=_params("parallel", "arbitrary"),
        name="ssm",
    )(z4, wd, wx, v, zo, apow, dsk)
    return y.reshape(batch * seq, SSM_WIDTH)


def _conv_kernel(*refs, n_col):
    vc, gc = refs[0:n_col], refs[n_col:2 * n_col]
    vh, gh = refs[2 * n_col:3 * n_col], refs[3 * n_col:4 * n_col]
    w_ref, cb_ref, lg_ref, lb_ref, o_ref, buf_ref, acc_ref = refs[4 * n_col:]
    tile = pl.program_id(1)
    n_slab, rows, cols = vc[0].shape
    lt_per_col = cols // LANES
    n_lt = n_col * lt_per_col
    lane = lambda t: slice(t * LANES, (t + 1) * LANES)
    for r in range(n_slab):
        for c in range(n_col):
            halo = vh[c][r].astype(F32) * jax.nn.sigmoid(gh[c][r].astype(F32))
            halo = jnp.where(tile > 0, halo, 0.0)
            glu = vc[c][r].astype(F32) * jax.nn.sigmoid(gc[c][r].astype(F32))
            for t in range(lt_per_col):
                buf_ref[c * lt_per_col + t, r, 0:CONV_HALO, :] = halo[:, lane(t)]
                buf_ref[c * lt_per_col + t, r, CONV_HALO:, :] = glu[:, lane(t)]

    def lane_tile(lt, carry):
        for r in range(n_slab):
            acc = jnp.broadcast_to(cb_ref[lt], (rows, LANES))
            for back in range(CONV_K):
                src = (r - back) % n_slab
                wrap = (back - r + n_slab - 1) // n_slab
                k = CONV_K - 1 - back
                acc = acc + w_ref[lt, k:k + 1, :] * buf_ref[lt, src, CONV_HALO - wrap:CONV_HALO - wrap + rows, :]
            acc_ref[lt, r] = acc
        return carry

    lax.fori_loop(0, n_lt, lane_tile, 0)
    for r in range(n_slab):
        c = jnp.concatenate([acc_ref[lt, r] for lt in range(n_lt)], axis=1)
        mu = jnp.mean(c, axis=-1, keepdims=True)
        var = jnp.mean(jnp.square(c - mu), axis=-1, keepdims=True)
        y = (c - mu) * lax.rsqrt(var + EPS) * lg_ref[...] + lb_ref[...]
        o_ref[r] = jax.nn.silu(y).astype(o_ref.dtype)


def _conformer_conv(z, conv_w, conv_b, ln_g, ln_b, batch, seq):
    cw = CONV_WIDTH
    n_lt = cw // LANES
    n_col = cw // CONV_COLS
    si = seq // SLABS
    rows = min(CONV_ROWS, si)
    z4 = z.reshape(batch, SLABS, si, z.shape[1])
    hb = rows // CONV_HALO
    col0 = COL_CONV // CONV_COLS
    cur = lambda c: pl.BlockSpec((None, SLABS, rows, CONV_COLS), lambda b, i: (b, 0, i, col0 + c))
    halo = lambda c: pl.BlockSpec((None, SLABS, CONV_HALO, CONV_COLS),
                                  lambda b, i: (b, 0, jnp.maximum(i * hb - 1, 0), col0 + c))
    vec = pl.BlockSpec((1, cw), lambda b, i: (0, 0))
    tiled = lambda n: pl.BlockSpec((n_lt, n, LANES), lambda b, i: (0, 0, 0))
    w_tiles = conv_w.reshape(CONV_K, n_lt, LANES).transpose(1, 0, 2)
    cols = range(2 * n_col)
    out = pl.pallas_call(
        functools.partial(_conv_kernel, n_col=n_col),
        grid=(batch, si // rows),
        in_specs=[cur(c) for c in cols] + [halo(c) for c in cols] + [tiled(CONV_K), tiled(1), vec, vec],
        out_specs=pl.BlockSpec((None, SLABS, rows, cw), lambda b, i: (b, 0, i, 0)),
        out_shape=jax.ShapeDtypeStruct((batch, SLABS, si, cw), BF16),
        scratch_shapes=[pltpu.VMEM((n_lt, SLABS, CONV_HALO + rows, LANES), F32),
                        pltpu.VMEM((n_lt, SLABS, rows, LANES), F32)],
        compiler_params=_params("parallel", "arbitrary"),
        name="conformer_conv",
    )(*([z4] * (4 * n_col)), w_tiles, conv_b.reshape(n_lt, 1, LANES), ln_g.reshape(1, cw), ln_b.reshape(1, cw))
    return out.reshape(batch * seq, cw)


def _merge_kernel(a_ref, y_ref, c_ref, g0_ref, g1_ref, g2_ref, wao_ref, wa_ref, wb_ref, wco_ref, o_ref):
    dot = functools.partial(jnp.dot, preferred_element_type=F32)
    sig = lambda r: jax.nn.sigmoid(r[...].astype(F32))
    y = y_ref[...]
    o_a = dot(a_ref[...], wao_ref[...])
    o_s = dot(y, wa_ref[...]) * jax.nn.sigmoid(dot(y, wb_ref[...]))
    o_c = dot(c_ref[...], wco_ref[...])
    o_ref[...] = (sig(g0_ref) * o_a + sig(g1_ref) * o_s + sig(g2_ref) * o_c).astype(o_ref.dtype)


def _merge(attn, y, conv, z, w_attn_out, w_glu_a, w_glu_b, w_conv_out, tm=1024, tn=512):
    m = attn.shape[0]
    d = w_attn_out.shape[1]
    nt = d // tn
    col0 = COL_GATES // tn
    tok = lambda a: pl.BlockSpec((tm, a.shape[1]), lambda i, j: (i, 0))
    gate = lambda br: pl.BlockSpec((tm, tn), lambda i, j: (i, col0 + br * nt + j))
    wgt = lambda w: pl.BlockSpec((w.shape[0], tn), lambda i, j: (0, j))
    return pl.pallas_call(
        _merge_kernel,
        grid=(m // tm, nt),
        in_specs=[tok(attn), tok(y), tok(conv), gate(0), gate(1), gate(2),
                  wgt(w_attn_out), wgt(w_glu_a), wgt(w_glu_b), wgt(w_conv_out)],
        out_specs=pl.BlockSpec((tm, tn), lambda i, j: (i, j)),
        out_shape=jax.ShapeDtypeStruct((m, d), BF16),
        compiler_params=_params("parallel", "arbitrary"),
        name="merge",
    )(attn, y, conv, z, z, z, w_attn_out, w_glu_a, w_glu_b, w_conv_out)


def _residual_proj_kernel(a_ref, w_ref, x_ref, g_ref, xo_ref, ho_ref, acc_ref):
    k = pl.program_id(1)

    @pl.when(k == 0)
    def _():
        acc_ref[...] = x_ref[...]

    acc_ref[...] += jnp.dot(a_ref[...], w_ref[...], preferred_element_type=F32)

    @pl.when(k == pl.num_programs(1) - 1)
    def _():
        x = acc_ref[...]
        xo_ref[...] = x
        ho_ref[...] = _rmsnorm_rows(x, g_ref[...]).astype(ho_ref.dtype)


def _residual_proj(a, w, x, g, h_dtype, tm, tk, name):
    m, kdim = a.shape
    d = w.shape[1]
    row = lambda i, k: (i, 0)
    return pl.pallas_call(
        _residual_proj_kernel,
        grid=(m // tm, kdim // tk),
        in_specs=[pl.BlockSpec((tm, tk), lambda i, k: (i, k)),
                  pl.BlockSpec((tk, d), lambda i, k: (k, 0)),
                  pl.BlockSpec((tm, d), row),
                  pl.BlockSpec((1, d), lambda i, k: (0, 0))],
        out_specs=[pl.BlockSpec((tm, d), row), pl.BlockSpec((tm, d), row)],
        out_shape=[jax.ShapeDtypeStruct((m, d), F32), jax.ShapeDtypeStruct((m, d), h_dtype)],
        scratch_shapes=[pltpu.VMEM((tm, d), F32)],
        compiler_params=_params("parallel", "arbitrary"),
        name=name,
    )(a, w, x, g.reshape(1, d))


def _ffn_up_kernel(h_ref, wg_ref, wu_ref, o_ref, wgb_ref, wub_ref):
    @pl.when(pl.program_id(1) == 0)
    def _():
        wgb_ref[...] = wg_ref[...].astype(BF16)
        wub_ref[...] = wu_ref[...].astype(BF16)

    h = h_ref[...]
    gate = jnp.dot(h, wgb_ref[...], preferred_element_type=F32)
    up = jnp.dot(h, wub_ref[...], preferred_element_type=F32)
    o_ref[...] = (jax.nn.silu(gate) * up).astype(o_ref.dtype)


def _ffn_up(h, wg_all, wu_all, layer, tm=2048, tn=512):
    m, k = h.shape
    n = wg_all.shape[2]
    wspec = pl.BlockSpec((None, k, tn), lambda j, i: (layer, 0, j))
    return pl.pallas_call(
        _ffn_up_kernel,
        grid=(n // tn, m // tm),
        in_specs=[pl.BlockSpec((tm, k), lambda j, i: (i, 0)), wspec, wspec],
        out_specs=pl.BlockSpec((tm, tn), lambda j, i: (i, j)),
        out_shape=jax.ShapeDtypeStruct((m, n), BF16),
        scratch_shapes=[pltpu.VMEM((k, tn), BF16), pltpu.VMEM((k, tn), BF16)],
        compiler_params=_params("arbitrary", "arbitrary"),
        name="ffn_up",
    )(h, wg_all, wu_all)


def kernel(x, w_in, rel_bias, lam_re, lam_im, log_dt, b_re, b_im, c_re, c_im, d_skip, w_ssm_glu_a, w_ssm_glu_b, conv_w, conv_b, conv_ln_g, conv_ln_b, w_conv_out, w_attn_out, w_out, norm_mix_g, norm_ffn_g, w_ffn_gate, w_ffn_up, w_ffn_down, norm_final_g):
    batch, seq, d_model = x.shape
    m = batch * seq
    si = seq // SLABS
    xf = x.reshape(batch, si, SLABS, d_model).transpose(0, 2, 1, 3).reshape(m, d_model)
    bands = [_band_bias(rel_bias, g) for g in range(N_GROUPS)]
    ssm_ops = _ssm_operators(lam_re, lam_im, log_dt, b_re, b_im, c_re, c_im, d_skip)
    col_scale = jnp.where(jnp.arange(w_in.shape[2]) < ATTN_QW, ATTN_SCALE, 1.0).astype(F32)[None, :]
    bf = lambda w: w.astype(BF16)
    w_attn_out, w_ssm_glu_a, w_ssm_glu_b, w_conv_out = bf(w_attn_out), bf(w_ssm_glu_a), bf(w_ssm_glu_b), bf(w_conv_out)
    w_out, w_ffn_down = bf(w_out), bf(w_ffn_down)

    h = _rmsnorm(xf, norm_mix_g[0], BF16)
    for l in range(DEPTH):
        z = _inproj(h, w_in, l, col_scale)
        outs, lses = zip(*[_attention_group(z, bands[g], g, batch, seq) for g in range(N_GROUPS)])
        attn = _attn_combine(outs, lses)
        y = _ssm(z, ssm_ops, l, batch, seq)
        conv = _conformer_conv(z, conv_w[l], conv_b[l], conv_ln_g[l], conv_ln_b[l], batch, seq)
        merged = _merge(attn, y, conv, z, w_attn_out[l], w_ssm_glu_a[l], w_ssm_glu_b[l], w_conv_out[l])
        xf, h = _residual_proj(merged, w_out[l], xf, norm_ffn_g[l], BF16, 512, d_model, "out_proj")
        act = _ffn_up(h, w_ffn_gate, w_ffn_up, l)
        last = l == DEPTH - 1
        g_next = norm_final_g if last else norm_mix_g[l + 1]
        xf, h = _residual_proj(act, w_ffn_down[l], xf, g_next, F32 if last else BF16, 512, D_FF // 4, "ffn_down")
    return h.reshape(batch, SLABS, si, d_model).transpose(0, 2, 1, 3).reshape(batch, seq, d_model)
```

```python
import functools
import math

import jax
import jax.numpy as jnp
import numpy as np
from jax import lax
from jax.experimental import pallas as pl
from jax.experimental.pallas import tpu as pltpu

F32 = jnp.float32
BF16 = jnp.bfloat16

D_MODEL = 2048
DEPTH = 4
HEAD_DIM = 64
HEADS_PER_GROUP = 8
ATTN_GROUPS = ((128, 1), (512, 4), (2048, 16))
N_GROUPS = len(ATTN_GROUPS)
N_ATTN_HEADS = HEADS_PER_GROUP * N_GROUPS
ATTN_QW = N_ATTN_HEADS * HEAD_DIM
ATTN_OUT_W = HEADS_PER_GROUP * HEAD_DIM
ATTN_LAGS = ATTN_GROUPS[0][0] // ATTN_GROUPS[0][1]
NUM_BUCKETS = 32
MAX_DISTANCE = max(w for w, _ in ATTN_GROUPS)
ATTN_SCALE = HEAD_DIM ** -0.5
NEG_INF = -1e30
SSM_WIDTH = 1024
SSM_GROUP = 16
SSM_STATE = 64
CONV_WIDTH = 1024
CONV_K = 31
N_BRANCHES = 3
D_FF = -(-8 * D_MODEL // (3 * 256)) * 256
EPS = 1e-6
COL_U = 3 * ATTN_QW
COL_CONV = COL_U + SSM_WIDTH
COL_GATES = COL_CONV + 2 * CONV_WIDTH

LANES = 128
BF16_ROWS = 16
VMEM_LIMIT = 56 * 2 ** 20

SLABS = max(d for _, d in ATTN_GROUPS)
ATTN_TILES = ((8, 8, 64, 1), (32, 32, 128, 2), (128, 128, 256, 4))
HEAD_PAIR_W = 2 * HEAD_DIM
SSM_CHUNK = SLABS
SSM_BLOCKS = SSM_WIDTH // LANES
SSM_GROUPS_PER_BLOCK = LANES // SSM_GROUP
SSM_STATE_LANES = SSM_GROUPS_PER_BLOCK * SSM_STATE
SSM_BATCH_PER_STEP = 4
CONV_ROWS = 32
CONV_HALO = BF16_ROWS
CONV_COLS = 512


def _params(*sem):
    return pltpu.CompilerParams(dimension_semantics=sem, vmem_limit_bytes=VMEM_LIMIT)


def _rmsnorm_rows(x, g):
    return x * lax.rsqrt(jnp.mean(x * x, axis=-1, keepdims=True) + EPS) * g


def _rmsnorm_kernel(x_ref, g_ref, o_ref):
    o_ref[...] = _rmsnorm_rows(x_ref[...], g_ref[...]).astype(o_ref.dtype)


def _rmsnorm(x, g, out_dtype, tm=512):
    m, d = x.shape
    return pl.pallas_call(
        _rmsnorm_kernel,
        grid=(m // tm,),
        in_specs=[pl.BlockSpec((tm, d), lambda i: (i, 0)),
                  pl.BlockSpec((1, d), lambda i: (0, 0))],
        out_specs=pl.BlockSpec((tm, d), lambda i: (i, 0)),
        out_shape=jax.ShapeDtypeStruct((m, d), out_dtype),
        compiler_params=_params("parallel"),
        name="rmsnorm",
    )(x, g.reshape(1, d))


def _inproj_kernel(a_ref, w_ref, scale_ref, o_ref, wb_ref):
    @pl.when(pl.program_id(1) == 0)
    def _():
        wb_ref[...] = (w_ref[...] * scale_ref[...]).astype(BF16)

    o_ref[...] = jnp.dot(a_ref[...], wb_ref[...], preferred_element_type=F32).astype(o_ref.dtype)


def _inproj(h, w_all, layer, col_scale, tm=1024, tn=1536):
    m, k = h.shape
    n = w_all.shape[2]
    return pl.pallas_call(
        _inproj_kernel,
        grid=(n // tn, m // tm),
        in_specs=[pl.BlockSpec((tm, k), lambda j, i: (i, 0)),
                  pl.BlockSpec((None, k, tn), lambda j, i: (layer, 0, j)),
                  pl.BlockSpec((1, tn), lambda j, i: (0, j))],
        out_specs=pl.BlockSpec((tm, tn), lambda j, i: (i, j)),
        out_shape=jax.ShapeDtypeStruct((m, n), BF16),
        scratch_shapes=[pltpu.VMEM((k, tn), BF16)],
        compiler_params=_params("arbitrary", "arbitrary"),
        name="inproj",
    )(h, w_all, col_scale)


def _t5_bucket_causal(dist):
    max_exact = NUM_BUCKETS // 2
    d = np.maximum(dist, 1).astype(np.float32)
    large = max_exact + (np.log(d / max_exact) / math.log(MAX_DISTANCE / max_exact)
                         * (NUM_BUCKETS - max_exact)).astype(np.int32)
    large = np.minimum(large, NUM_BUCKETS - 1)
    return np.where(dist < max_exact, dist, large).astype(np.int32)


def _band_bias(rel_bias, g):
    _, d = ATTN_GROUPS[g]
    n_c = SLABS // d
    ni, n_prev = ATTN_TILES[g][:2]
    qc, qi = np.divmod(np.arange(n_c * ni), ni)
    kc, ki = np.divmod(np.arange(n_c * (n_prev + ni)), n_prev + ni)
    lag = n_c * (qi[:, None] - ki[None, :] + n_prev) + (qc[:, None] - kc[None, :])
    on_band = (lag >= 0) & (lag <= ATTN_LAGS)
    bucket = np.where(on_band, _t5_bucket_causal(np.clip(lag, 0, ATTN_LAGS) * d), -1).astype(np.int32)

    def band_kernel(rel_ref, bucket_ref, out_ref):
        bkt = bucket_ref[...]
        for h in range(HEADS_PER_GROUP):
            acc = jnp.full(bkt.shape, NEG_INF, F32)
            for b in range(NUM_BUCKETS):
                acc = jnp.where(bkt == b, rel_ref[b, g * HEADS_PER_GROUP + h], acc)
            out_ref[h] = acc

    return pl.pallas_call(
        band_kernel,
        in_specs=[pl.BlockSpec(memory_space=pltpu.SMEM), pl.BlockSpec(memory_space=pltpu.VMEM)],
        out_specs=pl.BlockSpec(memory_space=pltpu.VMEM),
        out_shape=jax.ShapeDtypeStruct((HEADS_PER_GROUP,) + bucket.shape, F32),
        name=f"band_bias_g{g}",
    )(rel_bias, jnp.asarray(bucket))


def _attn_kernel(*refs, ni, n_prev, chained, last):
    q_ref, kp_ref, kc_ref, vp_ref, vc_ref, bm_ref = refs[:6]
    po_ref, pl_ref = refs[6:8] if chained else (None, None)
    o_ref = refs[8 if chained else 6]
    lse_ref = None if last else refs[-1]
    tile = pl.program_id(2)
    n_c, n_cls, rows, _ = q_ref.shape
    hist = kp_ref.shape[2]
    win = n_prev + ni
    n_pair = HEADS_PER_GROUP // 2
    pair = lambda hp: slice(hp * HEAD_PAIR_W, (hp + 1) * HEAD_PAIR_W)
    stage = BF16 if ni % BF16_ROWS == 0 else F32
    per_store = max(1, BF16_ROWS // ni)
    upper = lax.broadcasted_iota(jnp.int32, (1, HEAD_PAIR_W), 1) >= HEAD_DIM
    nt_dims = (((1,), (1,)), ((), ()))

    def sub_tile(qs, kwin, vwin, t):
        r0, k0 = t * ni, hist - n_prev + t * ni
        q = jnp.concatenate([qs[c][r0:r0 + ni] for c in range(n_c)], axis=0).astype(BF16)
        kk = jnp.concatenate([kwin[c][k0:k0 + win] for c in range(n_c)], axis=0).astype(BF16)
        vv = jnp.concatenate([vwin[c][k0:k0 + win] for c in range(n_c)], axis=0).astype(BF16)
        logits = []
        for h in range(HEADS_PER_GROUP):
            q2 = q[:, pair(h // 2)]
            qm = jnp.where(upper if h % 2 else ~upper, q2, jnp.zeros_like(q2))
            s = lax.dot_general(qm, kk[:, pair(h // 2)], nt_dims, preferred_element_type=F32) + bm_ref[h]
            if t == 0:
                col = lax.broadcasted_iota(jnp.int32, s.shape, 1)
                s = jnp.where((col % win >= n_prev) | (tile > 0), s, NEG_INF)
            logits.append(s)
        mx = [jnp.max(s, axis=-1, keepdims=True) for s in logits]
        probs = [jnp.exp(s - m).astype(BF16) for s, m in zip(logits, mx)]
        out = []
        for hp in range(n_pair):
            v2 = vv[:, pair(hp)]
            one = jnp.ones_like(v2)
            o0 = jnp.dot(probs[2 * hp], jnp.where(upper, one, v2), preferred_element_type=F32)
            o1 = jnp.dot(probs[2 * hp + 1], jnp.where(upper, v2, one), preferred_element_type=F32)
            den = pltpu.roll(jnp.where(upper, o0, o1), HEAD_DIM, axis=1)
            out.append((jnp.where(upper, o1, o0) * (1.0 / den),
                        jnp.where(upper, mx[2 * hp + 1], mx[2 * hp]) + jnp.log(den)))
        return out

    for cls in range(n_cls):
        qs = [q_ref[c, cls].astype(stage) for c in range(n_c)]
        kwin = [jnp.concatenate([kp_ref[c, cls], kc_ref[c, cls]], axis=0).astype(stage) for c in range(n_c)]
        vwin = [jnp.concatenate([vp_ref[c, cls], vc_ref[c, cls]], axis=0).astype(stage) for c in range(n_c)]
        for t0 in range(0, rows // ni, per_store):
            done = [sub_tile(qs, kwin, vwin, t) for t in range(t0, t0 + per_store)]
            rs = slice(t0 * ni, (t0 + per_store) * ni)
            for hp in range(n_pair):
                for c in range(n_c):
                    cs = slice(c * ni, (c + 1) * ni)
                    o = jnp.concatenate([d[hp][0][cs] for d in done], axis=0)
                    lse = jnp.concatenate([d[hp][1][cs] for d in done], axis=0)
                    if chained:
                        o_p, lse_p = po_ref[c, cls, rs, pair(hp)].astype(F32), pl_ref[c, cls, rs, pair(hp)]
                        top = jnp.maximum(lse_p, lse)
                        e_p, e_c = jnp.exp(lse_p - top), jnp.exp(lse - top)
                        o = (e_p * o_p + e_c * o) / (e_p + e_c)
                        lse = top + jnp.log(e_p + e_c)
                    o_ref[c, cls, rs, pair(hp)] = o.astype(o_ref.dtype)
                    if not last:
                        lse_ref[c, cls, rs, pair(hp)] = lse


def _attention_group(z, bm, g, batch, seq, done=None):
    _, d = ATTN_GROUPS[g]
    n_c, n_r = SLABS // d, d
    ni, n_prev, rows, n_cls = ATTN_TILES[g]
    si = seq // SLABS
    rows = min(rows, si)
    z5 = z.reshape(batch, n_c, n_r, si, z.shape[1])
    kb = ATTN_QW // ATTN_OUT_W
    hist = max(n_prev, BF16_ROWS)
    prev_per_step = rows // hist

    def cur(off):
        return pl.BlockSpec((None, n_c, n_cls, rows, ATTN_OUT_W), lambda b, r, i: (b, 0, r, i, off + g))

    def prev(off):
        return pl.BlockSpec((None, n_c, n_cls, hist, ATTN_OUT_W),
                            lambda b, r, i: (b, 0, r, jnp.maximum(i * prev_per_step - 1, 0), off + g))

    out_spec = pl.BlockSpec((None, n_c, n_cls, rows, ATTN_OUT_W), lambda b, r, i: (b, 0, r, i, 0))
    last = g == N_GROUPS - 1
    shape5 = (batch, n_c, n_r, si, ATTN_OUT_W)
    prev_args = [] if done is None else [a.reshape(shape5) for a in done]
    res = pl.pallas_call(
        functools.partial(_attn_kernel, ni=ni, n_prev=n_prev, chained=done is not None, last=last),
        grid=(batch, n_r // n_cls, si // rows),
        in_specs=[cur(0), prev(kb), cur(kb), prev(2 * kb), cur(2 * kb),
                  pl.BlockSpec(bm.shape, lambda b, r, i: (0, 0, 0))] + [out_spec] * len(prev_args),
        out_specs=[out_spec] * (1 if last else 2),
        out_shape=[jax.ShapeDtypeStruct(shape5, BF16)] + ([] if last else [jax.ShapeDtypeStruct(shape5, F32)]),
        compiler_params=_params("parallel", "parallel", "arbitrary"),
        name=f"attn_g{g}",
    )(z5, z5, z5, z5, z5, bm, *prev_args)
    return tuple(a.reshape(batch * seq, ATTN_OUT_W) for a in res)


def _attention(z, bands, batch, seq):
    done = None
    for g in range(N_GROUPS):
        done = _attention_group(z, bands[g], g, batch, seq, done)
    return done[0]


def _ssm_ops_kernel(v0r_ref, v0i_ref, cmr_ref, cmi_ref, pwr_ref, pwi_ref, pwrt_ref, pwit_ref,
                    wd_ref, wx_ref, v_ref, z_ref):
    t_len, half = SSM_CHUNK, SSM_CHUNK // 2
    blk = lambda i: slice(i * LANES, (i + 1) * LANES)
    v0r, v0i = v0r_ref[...], v0i_ref[...]
    cmr, cmi = cmr_ref[...], cmi_ref[...]
    cm = jnp.concatenate([cmr, -cmi], axis=0)
    kern = []
    for tau in range(t_len):
        pr, pi = pwr_ref[tau:tau + 1, :], pwi_ref[tau:tau + 1, :]
        vt = jnp.concatenate([v0r * pr - v0i * pi, v0r * pi + v0i * pr], axis=1)
        v_ref[blk(t_len - 1 - tau), :] = vt.astype(v_ref.dtype)
        kern.append(jnp.dot(vt, cm, precision=lax.Precision.HIGHEST, preferred_element_type=F32))
    zero = jnp.zeros((LANES, LANES), F32)
    for j in range(half):
        wd_ref[blk(j), :] = jnp.concatenate([kern[i - j] if i >= j else zero for i in range(half)],
                                            axis=1).astype(wd_ref.dtype)
        wx_ref[blk(j), :] = jnp.concatenate([kern[half + i - j] for i in range(half)], axis=1).astype(wx_ref.dtype)
    ns = SSM_STATE_LANES
    for i in range(t_len):
        pr, pi = pwrt_ref[:, i + 1:i + 2], pwit_ref[:, i + 1:i + 2]
        z_ref[0:ns, blk(i)] = (cmr * pr - cmi * pi).astype(z_ref.dtype)
        z_ref[ns:2 * ns, blk(i)] = (-(cmr * pi + cmi * pr)).astype(z_ref.dtype)


def _ssm_operators(lam_re, lam_im, log_dt, b_re, b_im, c_re, c_im, d_skip):
    t_len = SSM_CHUNK
    n_l, n_g, n_p = lam_re.shape
    nb, gl, ns = SSM_BLOCKS, SSM_GROUPS_PER_BLOCK, SSM_STATE_LANES
    dt = jnp.exp(log_dt)[..., None]
    lr, li = lam_re, lam_im
    mag = jnp.exp(lr * dt)
    ab_re, ab_im = mag * jnp.cos(li * dt), mag * jnp.sin(li * dt)
    nr, ni = ab_re - 1.0, ab_im
    den = lr * lr + li * li
    z_re, z_im = (nr * lr + ni * li) / den, (ni * lr - nr * li) / den
    bb_re = z_re[..., None] * b_re - z_im[..., None] * b_im
    bb_im = z_re[..., None] * b_im + z_im[..., None] * b_re
    tau = jnp.arange(t_len + 1, dtype=F32)[:, None, None, None]
    pmag = jnp.exp(lr * dt * tau)
    pw_re, pw_im = pmag * jnp.cos(li * dt * tau), pmag * jnp.sin(li * dt * tau)

    eye = jnp.eye(gl, dtype=F32)
    nlb = n_l * nb
    seed_in = lambda b: jnp.einsum('xgpc,gh->xgchp', b.reshape(nlb, gl, n_p, SSM_GROUP), eye).reshape(nlb, LANES, ns)
    seed_out = lambda c: jnp.einsum('xgop,gh->xgpho', c.reshape(nlb, gl, SSM_GROUP, n_p), eye).reshape(nlb, ns, LANES)
    pw_lanes = lambda p: p.reshape(t_len + 1, nlb, ns).transpose(1, 0, 2)
    pw_rows = lambda p: p.reshape(t_len + 1, nlb, ns).transpose(1, 2, 0)

    per_blk = lambda *shape: pl.BlockSpec((None,) + shape, lambda x: (x, 0, 0))
    cw = t_len * LANES
    wd, wx, v, z = pl.pallas_call(
        _ssm_ops_kernel,
        grid=(nlb,),
        in_specs=[per_blk(LANES, ns), per_blk(LANES, ns), per_blk(ns, LANES), per_blk(ns, LANES),
                  per_blk(t_len + 1, ns), per_blk(t_len + 1, ns), per_blk(ns, t_len + 1), per_blk(ns, t_len + 1)],
        out_specs=[per_blk(cw // 2, cw // 2), per_blk(cw // 2, cw // 2), per_blk(cw, 2 * ns), per_blk(2 * ns, cw)],
        out_shape=[jax.ShapeDtypeStruct((nlb, cw // 2, cw // 2), BF16),
                   jax.ShapeDtypeStruct((nlb, cw // 2, cw // 2), BF16),
                   jax.ShapeDtypeStruct((nlb, cw, 2 * ns), BF16),
                   jax.ShapeDtypeStruct((nlb, 2 * ns, cw), BF16)],
        compiler_params=_params("parallel"),
        name="ssm_operators",
    )(seed_in(bb_re), seed_in(bb_im), seed_out(c_re), seed_out(c_im),
      pw_lanes(pw_re), pw_lanes(pw_im), pw_rows(pw_re), pw_rows(pw_im))

    apow = jnp.concatenate([pw_re[t_len].reshape(nlb, 1, ns), pw_im[t_len].reshape(nlb, 1, ns)], axis=-1)
    dsk = jnp.tile(d_skip.reshape(nlb, 1, LANES), (1, 1, t_len))
    return wd, wx, v, z, apow, dsk


def _ssm_kernel(u_ref, wd_ref, wx_ref, v_ref, z_ref, ap_ref, dsk_ref, y_ref, s_ref):
    n_batch, n_slab, n_chunks, _ = u_ref.shape
    nt = SSM_STATE_LANES // LANES
    half = n_slab // 2 * LANES
    lane = lambda t: slice(t * LANES, (t + 1) * LANES)
    dot = functools.partial(jnp.dot, preferred_element_type=F32)
    chunk_rows = lambda b: jnp.concatenate([u_ref[b, r] for r in range(n_slab)], axis=1)

    for b in range(n_batch):
        s = dot(chunk_rows(b), v_ref[...])
        for t in range(2 * nt):
            s_ref[t, b * n_chunks:(b + 1) * n_chunks, :] = s[:, lane(t)]

    ar = [jnp.broadcast_to(ap_ref[:, lane(t)], (n_batch, LANES)) for t in range(nt)]
    ai = [jnp.broadcast_to(ap_ref[:, lane(nt + t)], (n_batch, LANES)) for t in range(nt)]

    def step(k, carry):
        idx = pl.ds(k, n_batch, stride=n_chunks)
        new = []
        for t in range(nt):
            xr, xi = carry[t]
            re_ref, im_ref = s_ref.at[t], s_ref.at[nt + t]
            sr, si = re_ref[idx, :], im_ref[idx, :]
            re_ref[idx, :] = xr
            im_ref[idx, :] = xi
            new.append((ar[t] * xr - ai[t] * xi + sr, ar[t] * xi + ai[t] * xr + si))
        return tuple(new)

    zero = jnp.zeros((n_batch, LANES), F32)
    lax.fori_loop(0, n_chunks, step, tuple((zero, zero) for _ in range(nt)), unroll=8)

    for b in range(n_batch):
        u = chunk_rows(b)
        rs = slice(b * n_chunks, (b + 1) * n_chunks)
        state = jnp.concatenate([s_ref[t, rs, :] for t in range(2 * nt)], axis=1).astype(BF16)
        y_lo = dot(u[:, :half], wd_ref[...]) + dot(state, z_ref[:, :half])
        y_hi = dot(u[:, :half], wx_ref[...]) + dot(u[:, half:], wd_ref[...]) + dot(state, z_ref[:, half:])
        y = jnp.concatenate([y_lo, y_hi], axis=1) + dsk_ref[...] * u.astype(F32)
        y = jax.nn.gelu(y).astype(y_ref.dtype)
        for r in range(n_slab):
            y_ref[b, r] = y[:, lane(r)]


def _ssm(z, ops, layer, batch, seq):
    wd, wx, v, zo, apow, dsk = ops
    si = seq // SLABS
    nbs = min(SSM_BATCH_PER_STEP, batch)
    z4 = z.reshape(batch, SLABS, si, z.shape[1])
    col0 = COL_U // LANES
    op = lambda a: pl.BlockSpec((None,) + a.shape[1:], lambda c, b: (layer * SSM_BLOCKS + c, 0, 0))
    y = pl.pallas_call(
        _ssm_kernel,
        grid=(SSM_BLOCKS, batch // nbs),
        in_specs=[pl.BlockSpec((nbs, SLABS, si, LANES), lambda c, b: (b, 0, 0, col0 + c)),
                  op(wd), op(wx), op(v), op(zo), op(apow), op(dsk)],
        out_specs=pl.BlockSpec((nbs, SLABS, si, LANES), lambda c, b: (b, 0, 0, c)),
        out_shape=jax.ShapeDtypeStruct((batch, SLABS, si, SSM_WIDTH), BF16),
        scratch_shapes=[pltpu.VMEM((2 * SSM_STATE_LANES // LANES, nbs * si, LANES), F32)],
        compiler_params=_params("parallel", "arbitrary"),
        name="ssm",
    )(z4, wd, wx, v, zo, apow, dsk)
    return y.reshape(batch * seq, SSM_WIDTH)


def _conv_kernel(*refs, n_col):
    vc, gc = refs[0:n_col], refs[n_col:2 * n_col]
    vh, gh = refs[2 * n_col:3 * n_col], refs[3 * n_col:4 * n_col]
    w_ref, cb_ref, lg_ref, lb_ref, o_ref, buf_ref, acc_ref = refs[4 * n_col:]
    tile = pl.program_id(1)
    n_slab, rows, cols = vc[0].shape
    lt_per_col = cols // LANES
    n_lt = n_col * lt_per_col
    lane = lambda t: slice(t * LANES, (t + 1) * LANES)
    for r in range(n_slab):
        for c in range(n_col):
            halo = vh[c][r].astype(F32) * jax.nn.sigmoid(gh[c][r].astype(F32))
            halo = jnp.where(tile > 0, halo, 0.0)
            glu = vc[c][r].astype(F32) * jax.nn.sigmoid(gc[c][r].astype(F32))
            for t in range(lt_per_col):
                buf_ref[c * lt_per_col + t, r, 0:CONV_HALO, :] = halo[:, lane(t)]
                buf_ref[c * lt_per_col + t, r, CONV_HALO:, :] = glu[:, lane(t)]

    def lane_tile(lt, carry):
        for r in range(n_slab):
            acc = jnp.broadcast_to(cb_ref[lt], (rows, LANES))
            for back in range(CONV_K):
                src = (r - back) % n_slab
                wrap = (back - r + n_slab - 1) // n_slab
                k = CONV_K - 1 - back
                acc = acc + w_ref[lt, k:k + 1, :] * buf_ref[lt, src, CONV_HALO - wrap:CONV_HALO - wrap + rows, :]
            acc_ref[lt, r] = acc
        return carry

    lax.fori_loop(0, n_lt, lane_tile, 0)
    for r in range(n_slab):
        c = jnp.concatenate([acc_ref[lt, r] for lt in range(n_lt)], axis=1)
        mu = jnp.mean(c, axis=-1, keepdims=True)
        var = jnp.mean(jnp.square(c - mu), axis=-1, keepdims=True)
        y = (c - mu) * lax.rsqrt(var + EPS) * lg_ref[...] + lb_ref[...]
        o_ref[r] = jax.nn.silu(y).astype(o_ref.dtype)


def _conformer_conv(z, conv_w, conv_b, ln_g, ln_b, batch, seq):
    cw = CONV_WIDTH
    n_lt = cw // LANES
    n_col = cw // CONV_COLS
    si = seq // SLABS
    rows = min(CONV_ROWS, si)
    z4 = z.reshape(batch, SLABS, si, z.shape[1])
    hb = rows // CONV_HALO
    col0 = COL_CONV // CONV_COLS
    cur = lambda c: pl.BlockSpec((None, SLABS, rows, CONV_COLS), lambda b, i: (b, 0, i, col0 + c))
    halo = lambda c: pl.BlockSpec((None, SLABS, CONV_HALO, CONV_COLS),
                                  lambda b, i: (b, 0, jnp.maximum(i * hb - 1, 0), col0 + c))
    vec = pl.BlockSpec((1, cw), lambda b, i: (0, 0))
    tiled = lambda n: pl.BlockSpec((n_lt, n, LANES), lambda b, i: (0, 0, 0))
    w_tiles = conv_w.reshape(CONV_K, n_lt, LANES).transpose(1, 0, 2)
    cols = range(2 * n_col)
    out = pl.pallas_call(
        functools.partial(_conv_kernel, n_col=n_col),
        grid=(batch, si // rows),
        in_specs=[cur(c) for c in cols] + [halo(c) for c in cols] + [tiled(CONV_K), tiled(1), vec, vec],
        out_specs=pl.BlockSpec((None, SLABS, rows, cw), lambda b, i: (b, 0, i, 0)),
        out_shape=jax.ShapeDtypeStruct((batch, SLABS, si, cw), BF16),
        scratch_shapes=[pltpu.VMEM((n_lt, SLABS, CONV_HALO + rows, LANES), F32),
                        pltpu.VMEM((n_lt, SLABS, rows, LANES), F32)],
        compiler_params=_params("parallel", "arbitrary"),
        name="conformer_conv",
    )(*([z4] * (4 * n_col)), w_tiles, conv_b.reshape(n_lt, 1, LANES), ln_g.reshape(1, cw), ln_b.reshape(1, cw))
    return out.reshape(batch * seq, cw)


def _merge_kernel(a_ref, y_ref, c_ref, g0_ref, g1_ref, g2_ref, wao_ref, wa_ref, wb_ref, wco_ref, o_ref):
    dot = functools.partial(jnp.dot, preferred_element_type=F32)
    sig = lambda r: jax.nn.sigmoid(r[...].astype(F32))
    y = y_ref[...]
    o_a = dot(a_ref[...], wao_ref[...])
    o_s = dot(y, wa_ref[...]) * jax.nn.sigmoid(dot(y, wb_ref[...]))
    o_c = dot(c_ref[...], wco_ref[...])
    o_ref[...] = (sig(g0_ref) * o_a + sig(g1_ref) * o_s + sig(g2_ref) * o_c).astype(o_ref.dtype)


def _merge(attn, y, conv, z, w_attn_out, w_glu_a, w_glu_b, w_conv_out, layer, tm=1024, tn=512):
    m = attn.shape[0]
    d = w_attn_out.shape[2]
    nt = d // tn
    col0 = COL_GATES // tn
    tok = lambda a: pl.BlockSpec((tm, a.shape[1]), lambda i, j: (i, 0))
    gate = lambda br: pl.BlockSpec((tm, tn), lambda i, j: (i, col0 + br * nt + j))
    wgt = lambda w: pl.BlockSpec((None, w.shape[1], tn), lambda i, j: (layer, 0, j))
    return pl.pallas_call(
        _merge_kernel,
        grid=(m // tm, nt),
        in_specs=[tok(attn), tok(y), tok(conv), gate(0), gate(1), gate(2),
                  wgt(w_attn_out), wgt(w_glu_a), wgt(w_glu_b), wgt(w_conv_out)],
        out_specs=pl.BlockSpec((tm, tn), lambda i, j: (i, j)),
        out_shape=jax.ShapeDtypeStruct((m, d), BF16),
        compiler_params=_params("parallel", "arbitrary"),
        name="merge",
    )(attn, y, conv, z, z, z, w_attn_out, w_glu_a, w_glu_b, w_conv_out)


def _residual_proj_kernel(a_ref, w_ref, x_ref, g_ref, xo_ref, ho_ref, acc_ref):
    k = pl.program_id(1)

    @pl.when(k == 0)
    def _():
        acc_ref[...] = x_ref[...]

    acc_ref[...] += jnp.dot(a_ref[...], w_ref[...], preferred_element_type=F32)

    @pl.when(k == pl.num_programs(1) - 1)
    def _():
        x = acc_ref[...]
        xo_ref[...] = x
        ho_ref[...] = _rmsnorm_rows(x, g_ref[...]).astype(ho_ref.dtype)


def _residual_proj(a, w_all, layer, x, g, h_dtype, tm, tk, name):
    m, kdim = a.shape
    d = w_all.shape[2]
    row = lambda i, k: (i, 0)
    return pl.pallas_call(
        _residual_proj_kernel,
        grid=(m // tm, kdim // tk),
        in_specs=[pl.BlockSpec((tm, tk), lambda i, k: (i, k)),
                  pl.BlockSpec((None, tk, d), lambda i, k: (layer, k, 0)),
                  pl.BlockSpec((tm, d), row),
                  pl.BlockSpec((1, d), lambda i, k: (0, 0))],
        out_specs=[pl.BlockSpec((tm, d), row), pl.BlockSpec((tm, d), row)],
        out_shape=[jax.ShapeDtypeStruct((m, d), F32), jax.ShapeDtypeStruct((m, d), h_dtype)],
        scratch_shapes=[pltpu.VMEM((tm, d), F32)],
        compiler_params=_params("parallel", "arbitrary"),
        name=name,
    )(a, w_all, x, g.reshape(1, d))


def _ffn_down_kernel(a_ref, w_ref, g_ref, x_hbm, xo_hbm, ho_hbm, acc_ref, ho_buf, sem_x, sem_out):
    i, k = pl.program_id(0), pl.program_id(1)
    n_i, n_k = pl.num_programs(0), pl.num_programs(1)
    slot = lax.rem(i, 2)
    x_copy = lambda tile, s: pltpu.make_async_copy(x_hbm.at[tile], acc_ref.at[s], sem_x.at[s])
    xo_copy = lambda tile, s: pltpu.make_async_copy(acc_ref.at[s], xo_hbm.at[tile], sem_out.at[0])
    ho_copy = lambda tile: pltpu.make_async_copy(ho_buf, ho_hbm.at[tile], sem_out.at[1])

    @pl.when((i == 0) & (k == 0))
    def _():
        x_copy(0, 0).start()

    @pl.when(k == 0)
    def _():
        x_copy(i, slot).wait()

    @pl.when((k == 1) & (i > 0))
    def _():
        xo_copy(i - 1, 1 - slot).wait()
        ho_copy(i - 1).wait()

    @pl.when((k == 1) & (i + 1 < n_i))
    def _():
        x_copy(i + 1, 1 - slot).start()

    acc_ref[slot] += jnp.dot(a_ref[...], w_ref[...], preferred_element_type=F32)

    @pl.when(k == n_k - 1)
    def _():
        ho_buf[...] = _rmsnorm_rows(acc_ref[slot], g_ref[...]).astype(ho_buf.dtype)
        xo_copy(i, slot).start()
        ho_copy(i).start()

    @pl.when((k == n_k - 1) & (i == n_i - 1))
    def _():
        xo_copy(i, slot).wait()
        ho_copy(i).wait()


def _ffn_down(a, w_all, layer, x, g, h_dtype, tm=1024, tk=D_FF // 4):
    m, kdim = a.shape
    d = w_all.shape[2]
    n_k = kdim // tk
    assert n_k >= 2, "the write-back wait and the prefetch are issued at contraction step 1"
    tiles = lambda dtype: jax.ShapeDtypeStruct((m // tm, tm, d), dtype)
    hbm = pl.BlockSpec(memory_space=pl.ANY)
    xo, ho = pl.pallas_call(
        _ffn_down_kernel,
        grid=(m // tm, n_k),
        in_specs=[pl.BlockSpec((tm, tk), lambda i, k: (i, k)),
                  pl.BlockSpec((None, tk, d), lambda i, k: (layer, k, 0)),
                  pl.BlockSpec((1, d), lambda i, k: (0, 0)),
                  hbm],
        out_specs=[hbm, hbm],
        out_shape=[tiles(F32), tiles(h_dtype)],
        scratch_shapes=[pltpu.VMEM((2, tm, d), F32), pltpu.VMEM((tm, d), h_dtype),
                        pltpu.SemaphoreType.DMA((2,)), pltpu.SemaphoreType.DMA((2,))],
        compiler_params=_params("arbitrary", "arbitrary"),
        name="ffn_down",
    )(a, w_all, g.reshape(1, d), x.reshape(m // tm, tm, d))
    return xo.reshape(m, d), ho.reshape(m, d)


def _ffn_up_kernel(h_ref, wg_ref, wu_ref, o_ref, wgb_ref, wub_ref):
    @pl.when(pl.program_id(1) == 0)
    def _():
        wgb_ref[...] = wg_ref[...].astype(BF16)
        wub_ref[...] = wu_ref[...].astype(BF16)

    h = h_ref[...]
    gate = jnp.dot(h, wgb_ref[...], preferred_element_type=F32)
    up = jnp.dot(h, wub_ref[...], preferred_element_type=F32)
    o_ref[...] = (jax.nn.silu(gate) * up).astype(o_ref.dtype)


def _ffn_up(h, wg_all, wu_all, layer, tm=1024, tn=512):
    m, k = h.shape
    n = wg_all.shape[2]
    wspec = pl.BlockSpec((None, k, tn), lambda j, i: (layer, 0, j))
    return pl.pallas_call(
        _ffn_up_kernel,
        grid=(n // tn, m // tm),
        in_specs=[pl.BlockSpec((tm, k), lambda j, i: (i, 0)), wspec, wspec],
        out_specs=pl.BlockSpec((tm, tn), lambda j, i: (i, j)),
        out_shape=jax.ShapeDtypeStruct((m, n), BF16),
        scratch_shapes=[pltpu.VMEM((k, tn), BF16), pltpu.VMEM((k, tn), BF16)],
        compiler_params=_params("arbitrary", "arbitrary"),
        name="ffn_up",
    )(h, wg_all, wu_all)


def kernel(x, w_in, rel_bias, lam_re, lam_im, log_dt, b_re, b_im, c_re, c_im, d_skip, w_ssm_glu_a, w_ssm_glu_b, conv_w, conv_b, conv_ln_g, conv_ln_b, w_conv_out, w_attn_out, w_out, norm_mix_g, norm_ffn_g, w_ffn_gate, w_ffn_up, w_ffn_down, norm_final_g):
    batch, seq, d_model = x.shape
    m = batch * seq
    si = seq // SLABS
    xf = x.reshape(batch, si, SLABS, d_model).transpose(0, 2, 1, 3).reshape(m, d_model)
    bands = [_band_bias(rel_bias, g) for g in range(N_GROUPS)]
    ssm_ops = _ssm_operators(lam_re, lam_im, log_dt, b_re, b_im, c_re, c_im, d_skip)
    col_scale = jnp.where(jnp.arange(w_in.shape[2]) < ATTN_QW, ATTN_SCALE, 1.0).astype(F32)[None, :]
    bf = lambda w: w.astype(BF16)
    w_attn_out, w_ssm_glu_a, w_ssm_glu_b, w_conv_out = bf(w_attn_out), bf(w_ssm_glu_a), bf(w_ssm_glu_b), bf(w_conv_out)
    w_out, w_ffn_down = bf(w_out), bf(w_ffn_down)

    h = _rmsnorm(xf, norm_mix_g[0], BF16)
    for l in range(DEPTH):
        z = _inproj(h, w_in, l, col_scale)
        attn = _attention(z, bands, batch, seq)
        y = _ssm(z, ssm_ops, l, batch, seq)
        conv = _conformer_conv(z, conv_w[l], conv_b[l], conv_ln_g[l], conv_ln_b[l], batch, seq)
        merged = _merge(attn, y, conv, z, w_attn_out, w_ssm_glu_a, w_ssm_glu_b, w_conv_out, l)
        xf, h = _residual_proj(merged, w_out, l, xf, norm_ffn_g[l], BF16, 512, d_model, "out_proj")
        act = _ffn_up(h, w_ffn_gate, w_ffn_up, l)
        last = l == DEPTH - 1
        g_next = norm_final_g if last else norm_mix_g[l + 1]
        xf, h = _ffn_down(act, w_ffn_down, l, xf, g_next, F32 if last else BF16)
    return h.reshape(batch, SLABS, si, d_model).transpose(0, 2, 1, 3).reshape(batch, seq, d_model)
```

```python
import functools
import math

import jax
import jax.numpy as jnp
import numpy as np
from jax import lax
from jax.experimental import pallas as pl
from jax.experimental.pallas import tpu as pltpu

F32 = jnp.float32
BF16 = jnp.bfloat16

D_MODEL = 2048
DEPTH = 4
HEAD_DIM = 64
HEADS_PER_GROUP = 8
ATTN_GROUPS = ((128, 1), (512, 4), (2048, 16))
N_GROUPS = len(ATTN_GROUPS)
N_ATTN_HEADS = HEADS_PER_GROUP * N_GROUPS
ATTN_QW = N_ATTN_HEADS * HEAD_DIM
ATTN_OUT_W = HEADS_PER_GROUP * HEAD_DIM
ATTN_LAGS = ATTN_GROUPS[0][0] // ATTN_GROUPS[0][1]
NUM_BUCKETS = 32
MAX_DISTANCE = max(w for w, _ in ATTN_GROUPS)
ATTN_SCALE = HEAD_DIM ** -0.5
NEG_INF = -1e30
SSM_WIDTH = 1024
SSM_GROUP = 16
SSM_STATE = 64
CONV_WIDTH = 1024
CONV_K = 31
N_BRANCHES = 3
D_FF = -(-8 * D_MODEL // (3 * 256)) * 256
EPS = 1e-6
COL_U = 3 * ATTN_QW
COL_CONV = COL_U + SSM_WIDTH
COL_GATES = COL_CONV + 2 * CONV_WIDTH

LANES = 128
BF16_ROWS = 16
VMEM_LIMIT = 56 * 2 ** 20

SLABS = max(d for _, d in ATTN_GROUPS)
ATTN_TILES = ((8, 8, 64, 1), (32, 32, 128, 2), (128, 128, 256, 4))
HEAD_PAIR_W = 2 * HEAD_DIM
SSM_CHUNK = SLABS
SSM_BLOCKS = SSM_WIDTH // LANES
SSM_GROUPS_PER_BLOCK = LANES // SSM_GROUP
SSM_STATE_LANES = SSM_GROUPS_PER_BLOCK * SSM_STATE
SSM_BATCH_PER_STEP = 4
CONV_ROWS = 32
CONV_HALO = BF16_ROWS
CONV_COLS = 512
CONV_RC = 64
CONV_PAD = 8
CONV_LOOP_ROWS = 16
CONV_LOOP_SLABS = 8


def _params(*sem):
    return pltpu.CompilerParams(dimension_semantics=sem, vmem_limit_bytes=VMEM_LIMIT)


def _rmsnorm_rows(x, g):
    return x * lax.rsqrt(jnp.mean(x * x, axis=-1, keepdims=True) + EPS) * g


def _rmsnorm_kernel(x_ref, g_ref, o_ref):
    o_ref[...] = _rmsnorm_rows(x_ref[...], g_ref[...]).astype(o_ref.dtype)


def _rmsnorm(x, g, out_dtype, tm=512):
    m, d = x.shape
    return pl.pallas_call(
        _rmsnorm_kernel,
        grid=(m // tm,),
        in_specs=[pl.BlockSpec((tm, d), lambda i: (i, 0)),
                  pl.BlockSpec((1, d), lambda i: (0, 0))],
        out_specs=pl.BlockSpec((tm, d), lambda i: (i, 0)),
        out_shape=jax.ShapeDtypeStruct((m, d), out_dtype),
        compiler_params=_params("parallel"),
        name="rmsnorm",
    )(x, g.reshape(1, d))


def _inproj_kernel(a_ref, w_ref, scale_ref, o_ref, wb_ref):
    @pl.when(pl.program_id(1) == 0)
    def _():
        wb_ref[...] = (w_ref[...] * scale_ref[...]).astype(BF16)

    o_ref[...] = jnp.dot(a_ref[...], wb_ref[...], preferred_element_type=F32).astype(o_ref.dtype)


def _inproj(h, w_all, layer, col_scale, tm=1024, tn=1536):
    m, k = h.shape
    n = w_all.shape[2]
    return pl.pallas_call(
        _inproj_kernel,
        grid=(n // tn, m // tm),
        in_specs=[pl.BlockSpec((tm, k), lambda j, i: (i, 0)),
                  pl.BlockSpec((None, k, tn), lambda j, i: (layer, 0, j)),
                  pl.BlockSpec((1, tn), lambda j, i: (0, j))],
        out_specs=pl.BlockSpec((tm, tn), lambda j, i: (i, j)),
        out_shape=jax.ShapeDtypeStruct((m, n), BF16),
        scratch_shapes=[pltpu.VMEM((k, tn), BF16)],
        compiler_params=_params("arbitrary", "arbitrary"),
        name="inproj",
    )(h, w_all, col_scale)


def _t5_bucket_causal(dist):
    max_exact = NUM_BUCKETS // 2
    d = np.maximum(dist, 1).astype(np.float32)
    large = max_exact + (np.log(d / max_exact) / math.log(MAX_DISTANCE / max_exact)
                         * (NUM_BUCKETS - max_exact)).astype(np.int32)
    large = np.minimum(large, NUM_BUCKETS - 1)
    return np.where(dist < max_exact, dist, large).astype(np.int32)


def _band_bias(rel_bias, g):
    _, d = ATTN_GROUPS[g]
    n_c = SLABS // d
    ni, n_prev = ATTN_TILES[g][:2]
    qc, qi = np.divmod(np.arange(n_c * ni), ni)
    kc, ki = np.divmod(np.arange(n_c * (n_prev + ni)), n_prev + ni)
    lag = n_c * (qi[:, None] - ki[None, :] + n_prev) + (qc[:, None] - kc[None, :])
    on_band = (lag >= 0) & (lag <= ATTN_LAGS)
    bucket = np.where(on_band, _t5_bucket_causal(np.clip(lag, 0, ATTN_LAGS) * d), -1).astype(np.int32)

    def band_kernel(rel_ref, bucket_ref, out_ref):
        bkt = bucket_ref[...]
        for h in range(HEADS_PER_GROUP):
            acc = jnp.full(bkt.shape, NEG_INF, F32)
            for b in range(NUM_BUCKETS):
                acc = jnp.where(bkt == b, rel_ref[b, g * HEADS_PER_GROUP + h], acc)
            out_ref[h] = acc

    return pl.pallas_call(
        band_kernel,
        in_specs=[pl.BlockSpec(memory_space=pltpu.SMEM), pl.BlockSpec(memory_space=pltpu.VMEM)],
        out_specs=pl.BlockSpec(memory_space=pltpu.VMEM),
        out_shape=jax.ShapeDtypeStruct((HEADS_PER_GROUP,) + bucket.shape, F32),
        name=f"band_bias_g{g}",
    )(rel_bias, jnp.asarray(bucket))


def _attn_kernel(*refs, ni, n_prev, chained, last):
    q_ref, kp_ref, kc_ref, vp_ref, vc_ref, bm_ref = refs[:6]
    po_ref, pl_ref = refs[6:8] if chained else (None, None)
    o_ref = refs[8 if chained else 6]
    lse_ref = None if last else refs[-1]
    tile = pl.program_id(2)
    n_c, n_cls, rows, _ = q_ref.shape
    hist = kp_ref.shape[2]
    win = n_prev + ni
    n_pair = HEADS_PER_GROUP // 2
    pair = lambda hp: slice(hp * HEAD_PAIR_W, (hp + 1) * HEAD_PAIR_W)
    stage = BF16 if ni % BF16_ROWS == 0 else F32
    per_store = max(1, BF16_ROWS // ni)
    upper = lax.broadcasted_iota(jnp.int32, (1, HEAD_PAIR_W), 1) >= HEAD_DIM
    nt_dims = (((1,), (1,)), ((), ()))

    def sub_tile(qs, kwin, vwin, t):
        r0, k0 = t * ni, hist - n_prev + t * ni
        q = jnp.concatenate([qs[c][r0:r0 + ni] for c in range(n_c)], axis=0).astype(BF16)
        kk = jnp.concatenate([kwin[c][k0:k0 + win] for c in range(n_c)], axis=0).astype(BF16)
        vv = jnp.concatenate([vwin[c][k0:k0 + win] for c in range(n_c)], axis=0).astype(BF16)
        logits = []
        for h in range(HEADS_PER_GROUP):
            q2 = q[:, pair(h // 2)]
            qm = jnp.where(upper if h % 2 else ~upper, q2, jnp.zeros_like(q2))
            s = lax.dot_general(qm, kk[:, pair(h // 2)], nt_dims, preferred_element_type=F32) + bm_ref[h]
            if t == 0:
                col = lax.broadcasted_iota(jnp.int32, s.shape, 1)
                s = jnp.where((col % win >= n_prev) | (tile > 0), s, NEG_INF)
            logits.append(s)
        mx = [jnp.max(s, axis=-1, keepdims=True) for s in logits]
        probs = [jnp.exp(s - m).astype(BF16) for s, m in zip(logits, mx)]
        out = []
        for hp in range(n_pair):
            v2 = vv[:, pair(hp)]
            one = jnp.ones_like(v2)
            o0 = jnp.dot(probs[2 * hp], jnp.where(upper, one, v2), preferred_element_type=F32)
            o1 = jnp.dot(probs[2 * hp + 1], jnp.where(upper, v2, one), preferred_element_type=F32)
            den = pltpu.roll(jnp.where(upper, o0, o1), HEAD_DIM, axis=1)
            out.append((jnp.where(upper, o1, o0) * (1.0 / den),
                        jnp.where(upper, mx[2 * hp + 1], mx[2 * hp]) + jnp.log(den)))
        return out

    for cls in range(n_cls):
        qs = [q_ref[c, cls].astype(stage) for c in range(n_c)]
        kwin = [jnp.concatenate([kp_ref[c, cls], kc_ref[c, cls]], axis=0).astype(stage) for c in range(n_c)]
        vwin = [jnp.concatenate([vp_ref[c, cls], vc_ref[c, cls]], axis=0).astype(stage) for c in range(n_c)]
        for t0 in range(0, rows // ni, per_store):
            done = [sub_tile(qs, kwin, vwin, t) for t in range(t0, t0 + per_store)]
            rs = slice(t0 * ni, (t0 + per_store) * ni)
            for hp in range(n_pair):
                for c in range(n_c):
                    cs = slice(c * ni, (c + 1) * ni)
                    o = jnp.concatenate([d[hp][0][cs] for d in done], axis=0)
                    lse = jnp.concatenate([d[hp][1][cs] for d in done], axis=0)
                    if chained:
                        o_p, lse_p = po_ref[c, cls, rs, pair(hp)].astype(F32), pl_ref[c, cls, rs, pair(hp)]
                        top = jnp.maximum(lse_p, lse)
                        e_p, e_c = jnp.exp(lse_p - top), jnp.exp(lse - top)
                        o = (e_p * o_p + e_c * o) / (e_p + e_c)
                        lse = top + jnp.log(e_p + e_c)
                    o_ref[c, cls, rs, pair(hp)] = o.astype(o_ref.dtype)
                    if not last:
                        lse_ref[c, cls, rs, pair(hp)] = lse


def _attention_group(z, bm, g, batch, seq, done=None):
    _, d = ATTN_GROUPS[g]
    n_c, n_r = SLABS // d, d
    ni, n_prev, rows, n_cls = ATTN_TILES[g]
    si = seq // SLABS
    rows = min(rows, si)
    z5 = z.reshape(batch, n_c, n_r, si, z.shape[1])
    kb = ATTN_QW // ATTN_OUT_W
    hist = max(n_prev, BF16_ROWS)
    prev_per_step = rows // hist

    def cur(off):
        return pl.BlockSpec((None, n_c, n_cls, rows, ATTN_OUT_W), lambda b, r, i: (b, 0, r, i, off + g))

    def prev(off):
        return pl.BlockSpec((None, n_c, n_cls, hist, ATTN_OUT_W),
                            lambda b, r, i: (b, 0, r, jnp.maximum(i * prev_per_step - 1, 0), off + g))

    out_spec = pl.BlockSpec((None, n_c, n_cls, rows, ATTN_OUT_W), lambda b, r, i: (b, 0, r, i, 0))
    last = g == N_GROUPS - 1
    shape5 = (batch, n_c, n_r, si, ATTN_OUT_W)
    prev_args = [] if done is None else [a.reshape(shape5) for a in done]
    res = pl.pallas_call(
        functools.partial(_attn_kernel, ni=ni, n_prev=n_prev, chained=done is not None, last=last),
        grid=(batch, n_r // n_cls, si // rows),
        in_specs=[cur(0), prev(kb), cur(kb), prev(2 * kb), cur(2 * kb),
                  pl.BlockSpec(bm.shape, lambda b, r, i: (0, 0, 0))] + [out_spec] * len(prev_args),
        out_specs=[out_spec] * (1 if last else 2),
        out_shape=[jax.ShapeDtypeStruct(shape5, BF16)] + ([] if last else [jax.ShapeDtypeStruct(shape5, F32)]),
        compiler_params=_params("parallel", "parallel", "arbitrary"),
        name=f"attn_g{g}",
    )(z5, z5, z5, z5, z5, bm, *prev_args)
    return tuple(a.reshape(batch * seq, ATTN_OUT_W) for a in res)


def _attention(z, bands, batch, seq):
    done = None
    for g in range(N_GROUPS):
        done = _attention_group(z, bands[g], g, batch, seq, done)
    return done[0]


def _ssm_ops_kernel(v0r_ref, v0i_ref, cmr_ref, cmi_ref, pwr_ref, pwi_ref, pwrt_ref, pwit_ref,
                    wd_ref, wx_ref, v_ref, z_ref):
    t_len, half = SSM_CHUNK, SSM_CHUNK // 2
    blk = lambda i: slice(i * LANES, (i + 1) * LANES)
    v0r, v0i = v0r_ref[...], v0i_ref[...]
    cmr, cmi = cmr_ref[...], cmi_ref[...]
    cm = jnp.concatenate([cmr, -cmi], axis=0)
    kern = []
    for tau in range(t_len):
        pr, pi = pwr_ref[tau:tau + 1, :], pwi_ref[tau:tau + 1, :]
        vt = jnp.concatenate([v0r * pr - v0i * pi, v0r * pi + v0i * pr], axis=1)
        v_ref[blk(t_len - 1 - tau), :] = vt.astype(v_ref.dtype)
        kern.append(jnp.dot(vt, cm, precision=lax.Precision.HIGHEST, preferred_element_type=F32))
    zero = jnp.zeros((LANES, LANES), F32)
    for j in range(half):
        wd_ref[blk(j), :] = jnp.concatenate([kern[i - j] if i >= j else zero for i in range(half)],
                                            axis=1).astype(wd_ref.dtype)
        wx_ref[blk(j), :] = jnp.concatenate([kern[half + i - j] for i in range(half)], axis=1).astype(wx_ref.dtype)
    ns = SSM_STATE_LANES
    for i in range(t_len):
        pr, pi = pwrt_ref[:, i + 1:i + 2], pwit_ref[:, i + 1:i + 2]
        z_ref[0:ns, blk(i)] = (cmr * pr - cmi * pi).astype(z_ref.dtype)
        z_ref[ns:2 * ns, blk(i)] = (-(cmr * pi + cmi * pr)).astype(z_ref.dtype)


def _ssm_operators(lam_re, lam_im, log_dt, b_re, b_im, c_re, c_im, d_skip):
    t_len = SSM_CHUNK
    n_l, n_g, n_p = lam_re.shape
    nb, gl, ns = SSM_BLOCKS, SSM_GROUPS_PER_BLOCK, SSM_STATE_LANES
    dt = jnp.exp(log_dt)[..., None]
    lr, li = lam_re, lam_im
    mag = jnp.exp(lr * dt)
    ab_re, ab_im = mag * jnp.cos(li * dt), mag * jnp.sin(li * dt)
    nr, ni = ab_re - 1.0, ab_im
    den = lr * lr + li * li
    z_re, z_im = (nr * lr + ni * li) / den, (ni * lr - nr * li) / den
    bb_re = z_re[..., None] * b_re - z_im[..., None] * b_im
    bb_im = z_re[..., None] * b_im + z_im[..., None] * b_re
    tau = jnp.arange(t_len + 1, dtype=F32)[:, None, None, None]
    pmag = jnp.exp(lr * dt * tau)
    pw_re, pw_im = pmag * jnp.cos(li * dt * tau), pmag * jnp.sin(li * dt * tau)

    eye = jnp.eye(gl, dtype=F32)
    nlb = n_l * nb
    seed_in = lambda b: jnp.einsum('xgpc,gh->xgchp', b.reshape(nlb, gl, n_p, SSM_GROUP), eye).reshape(nlb, LANES, ns)
    seed_out = lambda c: jnp.einsum('xgop,gh->xgpho', c.reshape(nlb, gl, SSM_GROUP, n_p), eye).reshape(nlb, ns, LANES)
    pw_lanes = lambda p: p.reshape(t_len + 1, nlb, ns).transpose(1, 0, 2)
    pw_rows = lambda p: p.reshape(t_len + 1, nlb, ns).transpose(1, 2, 0)

    per_blk = lambda *shape: pl.BlockSpec((None,) + shape, lambda x: (x, 0, 0))
    cw = t_len * LANES
    wd, wx, v, z = pl.pallas_call(
        _ssm_ops_kernel,
        grid=(nlb,),
        in_specs=[per_blk(LANES, ns), per_blk(LANES, ns), per_blk(ns, LANES), per_blk(ns, LANES),
                  per_blk(t_len + 1, ns), per_blk(t_len + 1, ns), per_blk(ns, t_len + 1), per_blk(ns, t_len + 1)],
        out_specs=[per_blk(cw // 2, cw // 2), per_blk(cw // 2, cw // 2), per_blk(cw, 2 * ns), per_blk(2 * ns, cw)],
        out_shape=[jax.ShapeDtypeStruct((nlb, cw // 2, cw // 2), BF16),
                   jax.ShapeDtypeStruct((nlb, cw // 2, cw // 2), BF16),
                   jax.ShapeDtypeStruct((nlb, cw, 2 * ns), BF16),
                   jax.ShapeDtypeStruct((nlb, 2 * ns, cw), BF16)],
        compiler_params=_params("parallel"),
        name="ssm_operators",
    )(seed_in(bb_re), seed_in(bb_im), seed_out(c_re), seed_out(c_im),
      pw_lanes(pw_re), pw_lanes(pw_im), pw_rows(pw_re), pw_rows(pw_im))

    apow = jnp.concatenate([pw_re[t_len].reshape(nlb, 1, ns), pw_im[t_len].reshape(nlb, 1, ns)], axis=-1)
    dsk = jnp.tile(d_skip.reshape(nlb, 1, LANES), (1, 1, t_len))
    return wd, wx, v, z, apow, dsk


def _ssm_kernel(u_ref, wd_ref, wx_ref, v_ref, z_ref, ap_ref, dsk_ref, y_ref, s_ref):
    n_batch, n_slab, n_chunks, _ = u_ref.shape
    nt = SSM_STATE_LANES // LANES
    half = n_slab // 2 * LANES
    lane = lambda t: slice(t * LANES, (t + 1) * LANES)
    dot = functools.partial(jnp.dot, preferred_element_type=F32)
    chunk_rows = lambda b: jnp.concatenate([u_ref[b, r] for r in range(n_slab)], axis=1)

    for b in range(n_batch):
        s = dot(chunk_rows(b), v_ref[...])
        for t in range(2 * nt):
            s_ref[t, b * n_chunks:(b + 1) * n_chunks, :] = s[:, lane(t)]

    ar = [jnp.broadcast_to(ap_ref[:, lane(t)], (n_batch, LANES)) for t in range(nt)]
    ai = [jnp.broadcast_to(ap_ref[:, lane(nt + t)], (n_batch, LANES)) for t in range(nt)]

    def step(k, carry):
        idx = pl.ds(k, n_batch, stride=n_chunks)
        new = []
        for t in range(nt):
            xr, xi = carry[t]
            re_ref, im_ref = s_ref.at[t], s_ref.at[nt + t]
            sr, si = re_ref[idx, :], im_ref[idx, :]
            re_ref[idx, :] = xr
            im_ref[idx, :] = xi
            new.append((ar[t] * xr - ai[t] * xi + sr, ar[t] * xi + ai[t] * xr + si))
        return tuple(new)

    zero = jnp.zeros((n_batch, LANES), F32)
    lax.fori_loop(0, n_chunks, step, tuple((zero, zero) for _ in range(nt)), unroll=8)

    for b in range(n_batch):
        u = chunk_rows(b)
        rs = slice(b * n_chunks, (b + 1) * n_chunks)
        state = jnp.concatenate([s_ref[t, rs, :] for t in range(2 * nt)], axis=1).astype(BF16)
        y_lo = dot(u[:, :half], wd_ref[...]) + dot(state, z_ref[:, :half])
        y_hi = dot(u[:, :half], wx_ref[...]) + dot(u[:, half:], wd_ref[...]) + dot(state, z_ref[:, half:])
        y = jnp.concatenate([y_lo, y_hi], axis=1) + dsk_ref[...] * u.astype(F32)
        y = jax.nn.gelu(y).astype(y_ref.dtype)
        for r in range(n_slab):
            y_ref[b, r] = y[:, lane(r)]


def _ssm(z, ops, layer, batch, seq):
    wd, wx, v, zo, apow, dsk = ops
    si = seq // SLABS
    nbs = min(SSM_BATCH_PER_STEP, batch)
    z4 = z.reshape(batch, SLABS, si, z.shape[1])
    col0 = COL_U // LANES
    op = lambda a: pl.BlockSpec((None,) + a.shape[1:], lambda c, b: (layer * SSM_BLOCKS + c, 0, 0))
    y = pl.pallas_call(
        _ssm_kernel,
        grid=(SSM_BLOCKS, batch // nbs),
        in_specs=[pl.BlockSpec((nbs, SLABS, si, LANES), lambda c, b: (b, 0, 0, col0 + c)),
                  op(wd), op(wx), op(v), op(zo), op(apow), op(dsk)],
        out_specs=pl.BlockSpec((nbs, SLABS, si, LANES), lambda c, b: (b, 0, 0, c)),
        out_shape=jax.ShapeDtypeStruct((batch, SLABS, si, SSM_WIDTH), BF16),
        scratch_shapes=[pltpu.VMEM((2 * SSM_STATE_LANES // LANES, nbs * si, LANES), F32)],
        compiler_params=_params("parallel", "arbitrary"),
        name="ssm",
    )(z4, wd, wx, v, zo, apow, dsk)
    return y.reshape(batch * seq, SSM_WIDTH)


def _tap_source(r, back, n_slab):
    return (r - back) % n_slab, (back - r + n_slab - 1) // n_slab


def _ssm_conv_kernel(u_ref, val_ref, gate_ref, wd_ref, wx_ref, v_ref, z_ref, ap_ref, dsk_ref, cw_ref, cb_ref,
                     y_ref, c_ref, s_ref, buf_ref):
    n_batch, n_slab, n_rows, _ = u_ref.shape
    assert n_batch == 2
    nt = SSM_STATE_LANES // LANES
    half = n_slab // 2 * LANES
    lane = lambda t: slice(t * LANES, (t + 1) * LANES)
    dot = functools.partial(jnp.dot, preferred_element_type=F32)
    chunk_rows = lambda b: jnp.concatenate([u_ref[b, r] for r in range(n_slab)], axis=1)

    def fill_shifted(b):
        for r in range(n_slab):
            glu = val_ref[b, r].astype(F32) * jax.nn.sigmoid(gate_ref[b, r].astype(F32))
            for w in range(buf_ref.shape[0]):
                buf_ref[w, r, 0:CONV_PAD, :] = jnp.zeros((CONV_PAD, LANES), F32)
                buf_ref[w, r, w:w + n_rows, :] = glu

    def tap(acc, r, back, rows):
        src, wrap = _tap_source(r, back, n_slab)
        k = CONV_K - 1 - back
        return acc + cw_ref[k:k + 1, :] * buf_ref[wrap, src, rows, :]

    for b in range(n_batch):
        s = dot(chunk_rows(b), v_ref[...])
        for t in range(2 * nt):
            s_ref[t, b * n_rows:(b + 1) * n_rows, :] = s[:, lane(t)]

    ar = [jnp.broadcast_to(ap_ref[:, lane(t)], (n_batch, LANES)) for t in range(nt)]
    ai = [jnp.broadcast_to(ap_ref[:, lane(nt + t)], (n_batch, LANES)) for t in range(nt)]

    def recur(k, carry):
        idx = pl.ds(k, n_batch, stride=n_rows)
        new = []
        for t in range(nt):
            xr, xi = carry[t]
            re_ref, im_ref = s_ref.at[t], s_ref.at[nt + t]
            sr, si = re_ref[idx, :], im_ref[idx, :]
            re_ref[idx, :] = xr
            im_ref[idx, :] = xi
            new.append((ar[t] * xr - ai[t] * xi + sr, ar[t] * xi + ai[t] * xr + si))
        return tuple(new)

    fill_shifted(0)
    bias = cb_ref[...]

    def super_step(j, carry):
        for q in range(CONV_LOOP_ROWS):
            carry = recur(j * CONV_LOOP_ROWS + q, carry)
        rows = pl.ds(pl.multiple_of(j * CONV_LOOP_ROWS, CONV_LOOP_ROWS), CONV_LOOP_ROWS)
        for r_lo in range(0, n_slab, CONV_LOOP_SLABS):
            slabs = range(r_lo, r_lo + CONV_LOOP_SLABS)
            accs = {r: jnp.broadcast_to(bias, (CONV_LOOP_ROWS, LANES)) for r in slabs}
            for back in range(CONV_K):
                for r in slabs:
                    accs[r] = tap(accs[r], r, back, rows)
            for r in slabs:
                c_ref[0, r, rows, :] = accs[r].astype(c_ref.dtype)
        return carry

    zero = jnp.zeros((n_batch, LANES), F32)
    lax.fori_loop(0, n_rows // CONV_LOOP_ROWS, super_step, tuple((zero, zero) for _ in range(nt)))

    def ssm_items(b):
        env = {}

        def load():
            env["u"] = chunk_rows(b)
            rs = slice(b * n_rows, (b + 1) * n_rows)
            env["x"] = jnp.concatenate([s_ref[t, rs, :] for t in range(2 * nt)], axis=1).astype(BF16)
            env["lo"] = dot(env["u"][:, :half], wd_ref[...])

        def finish():
            u = env["u"]
            y = jnp.concatenate([env["lo"], env["hi"]], axis=1) + dsk_ref[...] * u.astype(F32)
            y = jax.nn.gelu(y).astype(y_ref.dtype)
            for r in range(n_slab):
                y_ref[b, r] = y[:, lane(r)]

        return [load,
                lambda: env.__setitem__("lo", env["lo"] + dot(env["x"], z_ref[:, :half])),
                lambda: env.__setitem__("hi", dot(env["u"][:, :half], wx_ref[...])),
                lambda: env.__setitem__("hi", env["hi"] + dot(env["u"][:, half:], wd_ref[...])),
                lambda: env.__setitem__("hi", env["hi"] + dot(env["x"], z_ref[:, half:])),
                finish]

    def conv_item(b, r, r0):
        def run():
            acc = jnp.broadcast_to(bias, (CONV_RC, LANES))
            for back in range(CONV_K):
                acc = tap(acc, r, back, slice(r0, r0 + CONV_RC))
            c_ref[b, r, r0:r0 + CONV_RC, :] = acc.astype(c_ref.dtype)
        return run

    fill_shifted(1)
    mxu_items = ssm_items(0) + ssm_items(1)
    vpu_items = [conv_item(1, r, r0) for r in range(n_slab) for r0 in range(0, n_rows, CONV_RC)]
    per_mxu = -(-len(vpu_items) // len(mxu_items))
    for n, item in enumerate(mxu_items):
        item()
        for vpu_item in vpu_items[n * per_mxu:(n + 1) * per_mxu]:
            vpu_item()


def _ssm_conv(z, ops, conv_w, conv_b, layer, batch, seq):
    wd, wx, v, zo, apow, dsk = ops
    si = seq // SLABS
    nbs = 2
    z4 = z.reshape(batch, SLABS, si, z.shape[1])
    n_lt = CONV_WIDTH // LANES
    assert n_lt == SSM_BLOCKS and batch % nbs == 0 and si % CONV_RC == 0
    tok = lambda col0: pl.BlockSpec((nbs, SLABS, si, LANES), lambda c, b: (b, 0, 0, col0 + c))
    op = lambda a: pl.BlockSpec((None,) + a.shape[1:], lambda c, b: (layer * SSM_BLOCKS + c, 0, 0))
    per_tile = lambda n: pl.BlockSpec((None, n, LANES), lambda c, b: (c, 0, 0))
    w_tiles = conv_w.reshape(CONV_K, n_lt, LANES).transpose(1, 0, 2)
    out = jax.ShapeDtypeStruct((batch, SLABS, si, SSM_WIDTH), BF16)
    y, c = pl.pallas_call(
        _ssm_conv_kernel,
        grid=(SSM_BLOCKS, batch // nbs),
        in_specs=[tok(COL_U // LANES), tok(COL_CONV // LANES), tok((COL_CONV + CONV_WIDTH) // LANES),
                  op(wd), op(wx), op(v), op(zo), op(apow), op(dsk), per_tile(CONV_K), per_tile(1)],
        out_specs=[tok(0), tok(0)],
        out_shape=[out, out],
        scratch_shapes=[pltpu.VMEM((2 * SSM_STATE_LANES // LANES, nbs * si, LANES), F32),
                        pltpu.VMEM((3, SLABS, si + CONV_PAD, LANES), F32)],
        compiler_params=_params("parallel", "arbitrary"),
        name="ssm_conv",
    )(z4, z4, z4, wd, wx, v, zo, apow, dsk, w_tiles, conv_b.reshape(n_lt, 1, LANES))
    return y.reshape(batch * seq, SSM_WIDTH), c.reshape(batch * seq, CONV_WIDTH)


def _conv_kernel(*refs, n_col):
    vc, gc = refs[0:n_col], refs[n_col:2 * n_col]
    vh, gh = refs[2 * n_col:3 * n_col], refs[3 * n_col:4 * n_col]
    w_ref, cb_ref, lg_ref, lb_ref, o_ref, buf_ref, acc_ref = refs[4 * n_col:]
    tile = pl.program_id(1)
    n_slab, rows, cols = vc[0].shape
    lt_per_col = cols // LANES
    n_lt = n_col * lt_per_col
    lane = lambda t: slice(t * LANES, (t + 1) * LANES)
    for r in range(n_slab):
        for c in range(n_col):
            halo = vh[c][r].astype(F32) * jax.nn.sigmoid(gh[c][r].astype(F32))
            halo = jnp.where(tile > 0, halo, 0.0)
            glu = vc[c][r].astype(F32) * jax.nn.sigmoid(gc[c][r].astype(F32))
            for t in range(lt_per_col):
                buf_ref[c * lt_per_col + t, r, 0:CONV_HALO, :] = halo[:, lane(t)]
                buf_ref[c * lt_per_col + t, r, CONV_HALO:, :] = glu[:, lane(t)]

    def lane_tile(lt, carry):
        for r in range(n_slab):
            acc = jnp.broadcast_to(cb_ref[lt], (rows, LANES))
            for back in range(CONV_K):
                src = (r - back) % n_slab
                wrap = (back - r + n_slab - 1) // n_slab
                k = CONV_K - 1 - back
                acc = acc + w_ref[lt, k:k + 1, :] * buf_ref[lt, src, CONV_HALO - wrap:CONV_HALO - wrap + rows, :]
            acc_ref[lt, r] = acc
        return carry

    lax.fori_loop(0, n_lt, lane_tile, 0)
    for r in range(n_slab):
        c = jnp.concatenate([acc_ref[lt, r] for lt in range(n_lt)], axis=1)
        mu = jnp.mean(c, axis=-1, keepdims=True)
        var = jnp.mean(jnp.square(c - mu), axis=-1, keepdims=True)
        y = (c - mu) * lax.rsqrt(var + EPS) * lg_ref[...] + lb_ref[...]
        o_ref[r] = jax.nn.silu(y).astype(o_ref.dtype)


def _conformer_conv(z, conv_w, conv_b, ln_g, ln_b, batch, seq):
    cw = CONV_WIDTH
    n_lt = cw // LANES
    n_col = cw // CONV_COLS
    si = seq // SLABS
    rows = min(CONV_ROWS, si)
    z4 = z.reshape(batch, SLABS, si, z.shape[1])
    hb = rows // CONV_HALO
    col0 = COL_CONV // CONV_COLS
    cur = lambda c: pl.BlockSpec((None, SLABS, rows, CONV_COLS), lambda b, i: (b, 0, i, col0 + c))
    halo = lambda c: pl.BlockSpec((None, SLABS, CONV_HALO, CONV_COLS),
                                  lambda b, i: (b, 0, jnp.maximum(i * hb - 1, 0), col0 + c))
    vec = pl.BlockSpec((1, cw), lambda b, i: (0, 0))
    tiled = lambda n: pl.BlockSpec((n_lt, n, LANES), lambda b, i: (0, 0, 0))
    w_tiles = conv_w.reshape(CONV_K, n_lt, LANES).transpose(1, 0, 2)
    cols = range(2 * n_col)
    out = pl.pallas_call(
        functools.partial(_conv_kernel, n_col=n_col),
        grid=(batch, si // rows),
        in_specs=[cur(c) for c in cols] + [halo(c) for c in cols] + [tiled(CONV_K), tiled(1), vec, vec],
        out_specs=pl.BlockSpec((None, SLABS, rows, cw), lambda b, i: (b, 0, i, 0)),
        out_shape=jax.ShapeDtypeStruct((batch, SLABS, si, cw), BF16),
        scratch_shapes=[pltpu.VMEM((n_lt, SLABS, CONV_HALO + rows, LANES), F32),
                        pltpu.VMEM((n_lt, SLABS, rows, LANES), F32)],
        compiler_params=_params("parallel", "arbitrary"),
        name="conformer_conv",
    )(*([z4] * (4 * n_col)), w_tiles, conv_b.reshape(n_lt, 1, LANES), ln_g.reshape(1, cw), ln_b.reshape(1, cw))
    return out.reshape(batch * seq, cw)


def _merge_kernel(a_ref, y_ref, c_ref, lg_ref, lb_ref, g0_ref, g1_ref, g2_ref, wao_ref, wa_ref, wb_ref, wco_ref,
                  o_ref, cact_ref):
    @pl.when(pl.program_id(1) == 0)
    def _():
        c = c_ref[...].astype(F32)
        mu = jnp.mean(c, axis=-1, keepdims=True)
        var = jnp.mean(jnp.square(c - mu), axis=-1, keepdims=True)
        ln = (c - mu) * lax.rsqrt(var + EPS) * lg_ref[...] + lb_ref[...]
        cact_ref[...] = jax.nn.silu(ln).astype(cact_ref.dtype)

    dot = functools.partial(jnp.dot, preferred_element_type=F32)
    sig = lambda r: jax.nn.sigmoid(r[...].astype(F32))
    y = y_ref[...]
    o_a = dot(a_ref[...], wao_ref[...])
    o_s = dot(y, wa_ref[...]) * jax.nn.sigmoid(dot(y, wb_ref[...]))
    o_c = dot(cact_ref[...], wco_ref[...])
    o_ref[...] = (sig(g0_ref) * o_a + sig(g1_ref) * o_s + sig(g2_ref) * o_c).astype(o_ref.dtype)


def _merge(attn, y, conv, ln_g, ln_b, z, w_attn_out, w_glu_a, w_glu_b, w_conv_out, layer, tm=1024, tn=512):
    m = attn.shape[0]
    d = w_attn_out.shape[2]
    nt = d // tn
    col0 = COL_GATES // tn
    tok = lambda a: pl.BlockSpec((tm, a.shape[1]), lambda i, j: (i, 0))
    gate = lambda br: pl.BlockSpec((tm, tn), lambda i, j: (i, col0 + br * nt + j))
    wgt = lambda w: pl.BlockSpec((None, w.shape[1], tn), lambda i, j: (layer, 0, j))
    cw = conv.shape[1]
    vec = pl.BlockSpec((1, cw), lambda i, j: (0, 0))
    return pl.pallas_call(
        _merge_kernel,
        grid=(m // tm, nt),
        in_specs=[tok(attn), tok(y), tok(conv), vec, vec, gate(0), gate(1), gate(2),
                  wgt(w_attn_out), wgt(w_glu_a), wgt(w_glu_b), wgt(w_conv_out)],
        out_specs=pl.BlockSpec((tm, tn), lambda i, j: (i, j)),
        out_shape=jax.ShapeDtypeStruct((m, d), BF16),
        scratch_shapes=[pltpu.VMEM((tm, cw), BF16)],
        compiler_params=_params("parallel", "arbitrary"),
        name="merge",
    )(attn, y, conv, ln_g.reshape(1, cw), ln_b.reshape(1, cw), z, z, z, w_attn_out, w_glu_a, w_glu_b, w_conv_out)


def _residual_proj_kernel(a_ref, w_ref, x_ref, g_ref, xo_ref, ho_ref, acc_ref):
    k = pl.program_id(1)

    @pl.when(k == 0)
    def _():
        acc_ref[...] = x_ref[...]

    acc_ref[...] += jnp.dot(a_ref[...], w_ref[...], preferred_element_type=F32)

    @pl.when(k == pl.num_programs(1) - 1)
    def _():
        x = acc_ref[...]
        xo_ref[...] = x
        ho_ref[...] = _rmsnorm_rows(x, g_ref[...]).astype(ho_ref.dtype)


def _residual_proj(a, w_all, layer, x, g, h_dtype, tm, tk, name):
    m, kdim = a.shape
    d = w_all.shape[2]
    row = lambda i, k: (i, 0)
    return pl.pallas_call(
        _residual_proj_kernel,
        grid=(m // tm, kdim // tk),
        in_specs=[pl.BlockSpec((tm, tk), lambda i, k: (i, k)),
                  pl.BlockSpec((None, tk, d), lambda i, k: (layer, k, 0)),
                  pl.BlockSpec((tm, d), row),
                  pl.BlockSpec((1, d), lambda i, k: (0, 0))],
        out_specs=[pl.BlockSpec((tm, d), row), pl.BlockSpec((tm, d), row)],
        out_shape=[jax.ShapeDtypeStruct((m, d), F32), jax.ShapeDtypeStruct((m, d), h_dtype)],
        scratch_shapes=[pltpu.VMEM((tm, d), F32)],
        compiler_params=_params("parallel", "arbitrary"),
        name=name,
    )(a, w_all, x, g.reshape(1, d))


def _ffn_down_kernel(a_ref, w_ref, g_ref, x_hbm, xo_hbm, ho_hbm, acc_ref, ho_buf, sem_x, sem_out):
    i, k = pl.program_id(0), pl.program_id(1)
    n_i, n_k = pl.num_programs(0), pl.num_programs(1)
    slot = lax.rem(i, 2)
    x_copy = lambda tile, s: pltpu.make_async_copy(x_hbm.at[tile], acc_ref.at[s], sem_x.at[s])
    xo_copy = lambda tile, s: pltpu.make_async_copy(acc_ref.at[s], xo_hbm.at[tile], sem_out.at[0])
    ho_copy = lambda tile: pltpu.make_async_copy(ho_buf, ho_hbm.at[tile], sem_out.at[1])

    @pl.when((i == 0) & (k == 0))
    def _():
        x_copy(0, 0).start()

    @pl.when(k == 0)
    def _():
        x_copy(i, slot).wait()

    @pl.when((k == 1) & (i > 0))
    def _():
        xo_copy(i - 1, 1 - slot).wait()
        ho_copy(i - 1).wait()

    @pl.when((k == 1) & (i + 1 < n_i))
    def _():
        x_copy(i + 1, 1 - slot).start()

    acc_ref[slot] += jnp.dot(a_ref[...], w_ref[...], preferred_element_type=F32)

    @pl.when(k == n_k - 1)
    def _():
        ho_buf[...] = _rmsnorm_rows(acc_ref[slot], g_ref[...]).astype(ho_buf.dtype)
        xo_copy(i, slot).start()
        ho_copy(i).start()

    @pl.when((k == n_k - 1) & (i == n_i - 1))
    def _():
        xo_copy(i, slot).wait()
        ho_copy(i).wait()


def _ffn_down(a, w_all, layer, x, g, h_dtype, tm=1024, tk=D_FF // 4):
    m, kdim = a.shape
    d = w_all.shape[2]
    n_k = kdim // tk
    assert n_k >= 2, "the write-back wait and the prefetch are issued at contraction step 1"
    tiles = lambda dtype: jax.ShapeDtypeStruct((m // tm, tm, d), dtype)
    hbm = pl.BlockSpec(memory_space=pl.ANY)
    xo, ho = pl.pallas_call(
        _ffn_down_kernel,
        grid=(m // tm, n_k),
        in_specs=[pl.BlockSpec((tm, tk), lambda i, k: (i, k)),
                  pl.BlockSpec((None, tk, d), lambda i, k: (layer, k, 0)),
                  pl.BlockSpec((1, d), lambda i, k: (0, 0)),
                  hbm],
        out_specs=[hbm, hbm],
        out_shape=[tiles(F32), tiles(h_dtype)],
        scratch_shapes=[pltpu.VMEM((2, tm, d), F32), pltpu.VMEM((tm, d), h_dtype),
                        pltpu.SemaphoreType.DMA((2,)), pltpu.SemaphoreType.DMA((2,))],
        compiler_params=_params("arbitrary", "arbitrary"),
        name="ffn_down",
    )(a, w_all, g.reshape(1, d), x.reshape(m // tm, tm, d))
    return xo.reshape(m, d), ho.reshape(m, d)


def _ffn_up_kernel(h_ref, wg_ref, wu_ref, o_ref, wgb_ref, wub_ref):
    @pl.when(pl.program_id(1) == 0)
    def _():
        wgb_ref[...] = wg_ref[...].astype(BF16)
        wub_ref[...] = wu_ref[...].astype(BF16)

    h = h_ref[...]
    gate = jnp.dot(h, wgb_ref[...], preferred_element_type=F32)
    up = jnp.dot(h, wub_ref[...], preferred_element_type=F32)
    o_ref[...] = (jax.nn.silu(gate) * up).astype(o_ref.dtype)


def _ffn_up(h, wg_all, wu_all, layer, tm=1024, tn=512):
    m, k = h.shape
    n = wg_all.shape[2]
    wspec = pl.BlockSpec((None, k, tn), lambda j, i: (layer, 0, j))
    return pl.pallas_call(
        _ffn_up_kernel,
        grid=(n // tn, m // tm),
        in_specs=[pl.BlockSpec((tm, k), lambda j, i: (i, 0)), wspec, wspec],
        out_specs=pl.BlockSpec((tm, tn), lambda j, i: (i, j)),
        out_shape=jax.ShapeDtypeStruct((m, n), BF16),
        scratch_shapes=[pltpu.VMEM((k, tn), BF16), pltpu.VMEM((k, tn), BF16)],
        compiler_params=_params("arbitrary", "arbitrary"),
        name="ffn_up",
    )(h, wg_all, wu_all)


def kernel(x, w_in, rel_bias, lam_re, lam_im, log_dt, b_re, b_im, c_re, c_im, d_skip, w_ssm_glu_a, w_ssm_glu_b, conv_w, conv_b, conv_ln_g, conv_ln_b, w_conv_out, w_attn_out, w_out, norm_mix_g, norm_ffn_g, w_ffn_gate, w_ffn_up, w_ffn_down, norm_final_g):
    batch, seq, d_model = x.shape
    m = batch * seq
    si = seq // SLABS
    xf = x.reshape(batch, si, SLABS, d_model).transpose(0, 2, 1, 3).reshape(m, d_model)
    bands = [_band_bias(rel_bias, g) for g in range(N_GROUPS)]
    ssm_ops = _ssm_operators(lam_re, lam_im, log_dt, b_re, b_im, c_re, c_im, d_skip)
    col_scale = jnp.where(jnp.arange(w_in.shape[2]) < ATTN_QW, ATTN_SCALE, 1.0).astype(F32)[None, :]
    bf = lambda w: w.astype(BF16)
    w_attn_out, w_ssm_glu_a, w_ssm_glu_b, w_conv_out = bf(w_attn_out), bf(w_ssm_glu_a), bf(w_ssm_glu_b), bf(w_conv_out)
    w_out, w_ffn_down = bf(w_out), bf(w_ffn_down)

    h = _rmsnorm(xf, norm_mix_g[0], BF16)
    for l in range(DEPTH):
        z = _inproj(h, w_in, l, col_scale)
        attn = _attention(z, bands, batch, seq)
        y, conv = _ssm_conv(z, ssm_ops, conv_w[l], conv_b[l], l, batch, seq)
        merged = _merge(attn, y, conv, conv_ln_g[l], conv_ln_b[l], z, w_attn_out, w_ssm_glu_a, w_ssm_glu_b,
                        w_conv_out, l)
        xf, h = _residual_proj(merged, w_out, l, xf, norm_ffn_g[l], BF16, 512, d_model, "out_proj")
        act = _ffn_up(h, w_ffn_gate, w_ffn_up, l)
        last = l == DEPTH - 1
        g_next = norm_final_g if last else norm_mix_g[l + 1]
        xf, h = _ffn_down(act, w_ffn_down, l, xf, g_next, F32 if last else BF16)
    return h.reshape(batch, SLABS, si, d_model).transpose(0, 2, 1, 3).reshape(batch, seq, d_model)
```

```python
import functools
import math

import jax
import jax.numpy as jnp
import numpy as np
from jax import lax
from jax.experimental import pallas as pl
from jax.experimental.pallas import tpu as pltpu

F32 = jnp.float32
BF16 = jnp.bfloat16

D_MODEL = 2048
DEPTH = 4
HEAD_DIM = 64
HEADS_PER_GROUP = 8
ATTN_GROUPS = ((128, 1), (512, 4), (2048, 16))
N_GROUPS = len(ATTN_GROUPS)
N_ATTN_HEADS = HEADS_PER_GROUP * N_GROUPS
ATTN_QW = N_ATTN_HEADS * HEAD_DIM
ATTN_OUT_W = HEADS_PER_GROUP * HEAD_DIM
ATTN_LAGS = ATTN_GROUPS[0][0] // ATTN_GROUPS[0][1]
NUM_BUCKETS = 32
MAX_DISTANCE = max(w for w, _ in ATTN_GROUPS)
ATTN_SCALE = HEAD_DIM ** -0.5
NEG_INF = -1e30
SSM_WIDTH = 1024
SSM_GROUP = 16
SSM_STATE = 64
CONV_WIDTH = 1024
CONV_K = 31
N_BRANCHES = 3
D_FF = -(-8 * D_MODEL // (3 * 256)) * 256
EPS = 1e-6
COL_U = 3 * ATTN_QW
COL_CONV = COL_U + SSM_WIDTH
COL_GATES = COL_CONV + 2 * CONV_WIDTH

LANES = 128
BF16_ROWS = 16
VMEM_LIMIT = 56 * 2 ** 20

SLABS = max(d for _, d in ATTN_GROUPS)
ATTN_TILES = ((8, 8, 64, 1), (32, 32, 128, 2), (128, 128, 256, 4))
HEAD_PAIR_W = 2 * HEAD_DIM
SSM_CHUNK = SLABS
SSM_BLOCKS = SSM_WIDTH // LANES
SSM_GROUPS_PER_BLOCK = LANES // SSM_GROUP
SSM_STATE_LANES = SSM_GROUPS_PER_BLOCK * SSM_STATE
CONV_RC = 64
CONV_PAD = 8
CONV_LOOP_ROWS = 16
CONV_LOOP_SLABS = 8


def _params(*sem):
    return pltpu.CompilerParams(dimension_semantics=sem, vmem_limit_bytes=VMEM_LIMIT)


def _rmsnorm_rows(x, g):
    return x * lax.rsqrt(jnp.mean(x * x, axis=-1, keepdims=True) + EPS) * g


def _rmsnorm_kernel(x_ref, g_ref, o_ref):
    o_ref[...] = _rmsnorm_rows(x_ref[...], g_ref[...]).astype(o_ref.dtype)


def _rmsnorm(x, g, out_dtype, tm=512):
    m, d = x.shape
    return pl.pallas_call(
        _rmsnorm_kernel,
        grid=(m // tm,),
        in_specs=[pl.BlockSpec((tm, d), lambda i: (i, 0)),
                  pl.BlockSpec((1, d), lambda i: (0, 0))],
        out_specs=pl.BlockSpec((tm, d), lambda i: (i, 0)),
        out_shape=jax.ShapeDtypeStruct((m, d), out_dtype),
        compiler_params=_params("parallel"),
        name="rmsnorm",
    )(x, g.reshape(1, d))


def _inproj_kernel(a_ref, w_ref, scale_ref, o_ref, wb_ref):
    @pl.when(pl.program_id(1) == 0)
    def _():
        wb_ref[...] = (w_ref[...] * scale_ref[...]).astype(BF16)

    o_ref[...] = jnp.dot(a_ref[...], wb_ref[...], preferred_element_type=F32).astype(o_ref.dtype)


def _inproj(h, w_all, layer, col_scale, tm=1024, tn=1536):
    m, k = h.shape
    n = w_all.shape[2]
    return pl.pallas_call(
        _inproj_kernel,
        grid=(n // tn, m // tm),
        in_specs=[pl.BlockSpec((tm, k), lambda j, i: (i, 0)),
                  pl.BlockSpec((None, k, tn), lambda j, i: (layer, 0, j)),
                  pl.BlockSpec((1, tn), lambda j, i: (0, j))],
        out_specs=pl.BlockSpec((tm, tn), lambda j, i: (i, j)),
        out_shape=jax.ShapeDtypeStruct((m, n), BF16),
        scratch_shapes=[pltpu.VMEM((k, tn), BF16)],
        compiler_params=_params("arbitrary", "arbitrary"),
        name="inproj",
    )(h, w_all, col_scale)


def _t5_bucket_causal(dist):
    max_exact = NUM_BUCKETS // 2
    d = np.maximum(dist, 1).astype(np.float32)
    large = max_exact + (np.log(d / max_exact) / math.log(MAX_DISTANCE / max_exact)
                         * (NUM_BUCKETS - max_exact)).astype(np.int32)
    large = np.minimum(large, NUM_BUCKETS - 1)
    return np.where(dist < max_exact, dist, large).astype(np.int32)


def _band_bias(rel_bias, g):
    _, d = ATTN_GROUPS[g]
    n_c = SLABS // d
    ni, n_prev = ATTN_TILES[g][:2]
    qc, qi = np.divmod(np.arange(n_c * ni), ni)
    kc, ki = np.divmod(np.arange(n_c * (n_prev + ni)), n_prev + ni)
    lag = n_c * (qi[:, None] - ki[None, :] + n_prev) + (qc[:, None] - kc[None, :])
    on_band = (lag >= 0) & (lag <= ATTN_LAGS)
    bucket = np.where(on_band, _t5_bucket_causal(np.clip(lag, 0, ATTN_LAGS) * d), -1).astype(np.int32)

    def band_kernel(rel_ref, bucket_ref, out_ref):
        bkt = bucket_ref[...]
        for h in range(HEADS_PER_GROUP):
            acc = jnp.full(bkt.shape, NEG_INF, F32)
            for b in range(NUM_BUCKETS):
                acc = jnp.where(bkt == b, rel_ref[b, g * HEADS_PER_GROUP + h], acc)
            out_ref[h] = acc

    return pl.pallas_call(
        band_kernel,
        in_specs=[pl.BlockSpec(memory_space=pltpu.SMEM), pl.BlockSpec(memory_space=pltpu.VMEM)],
        out_specs=pl.BlockSpec(memory_space=pltpu.VMEM),
        out_shape=jax.ShapeDtypeStruct((HEADS_PER_GROUP,) + bucket.shape, F32),
        name=f"band_bias_g{g}",
    )(rel_bias, jnp.asarray(bucket))


def _attn_kernel(*refs, ni, n_prev, chained, last):
    q_ref, kp_ref, kc_ref, vp_ref, vc_ref, bm_ref = refs[:6]
    po_ref, pl_ref = refs[6:8] if chained else (None, None)
    o_ref = refs[8 if chained else 6]
    lse_ref = None if last else refs[-1]
    tile = pl.program_id(2)
    n_c, n_cls, rows, _ = q_ref.shape
    hist = kp_ref.shape[2]
    win = n_prev + ni
    n_pair = HEADS_PER_GROUP // 2
    pair = lambda hp: slice(hp * HEAD_PAIR_W, (hp + 1) * HEAD_PAIR_W)
    stage = BF16 if ni % BF16_ROWS == 0 else F32
    per_store = max(1, BF16_ROWS // ni)
    upper = lax.broadcasted_iota(jnp.int32, (1, HEAD_PAIR_W), 1) >= HEAD_DIM
    nt_dims = (((1,), (1,)), ((), ()))

    def sub_tile(qs, kwin, vwin, t):
        r0, k0 = t * ni, hist - n_prev + t * ni
        q = jnp.concatenate([qs[c][r0:r0 + ni] for c in range(n_c)], axis=0).astype(BF16)
        kk = jnp.concatenate([kwin[c][k0:k0 + win] for c in range(n_c)], axis=0).astype(BF16)
        vv = jnp.concatenate([vwin[c][k0:k0 + win] for c in range(n_c)], axis=0).astype(BF16)
        logits = []
        for h in range(HEADS_PER_GROUP):
            q2 = q[:, pair(h // 2)]
            qm = jnp.where(upper if h % 2 else ~upper, q2, jnp.zeros_like(q2))
            s = lax.dot_general(qm, kk[:, pair(h // 2)], nt_dims, preferred_element_type=F32) + bm_ref[h]
            if t == 0:
                col = lax.broadcasted_iota(jnp.int32, s.shape, 1)
                s = jnp.where((col % win >= n_prev) | (tile > 0), s, NEG_INF)
            logits.append(s)
        mx = [jnp.max(s, axis=-1, keepdims=True) for s in logits]
        probs = [jnp.exp(s - m).astype(BF16) for s, m in zip(logits, mx)]
        out = []
        for hp in range(n_pair):
            v2 = vv[:, pair(hp)]
            one = jnp.ones_like(v2)
            o0 = jnp.dot(probs[2 * hp], jnp.where(upper, one, v2), preferred_element_type=F32)
            o1 = jnp.dot(probs[2 * hp + 1], jnp.where(upper, v2, one), preferred_element_type=F32)
            den = pltpu.roll(jnp.where(upper, o0, o1), HEAD_DIM, axis=1)
            out.append((jnp.where(upper, o1, o0) * (1.0 / den),
                        jnp.where(upper, mx[2 * hp + 1], mx[2 * hp]) + jnp.log(den)))
        return out

    for cls in range(n_cls):
        qs = [q_ref[c, cls].astype(stage) for c in range(n_c)]
        kwin = [jnp.concatenate([kp_ref[c, cls], kc_ref[c, cls]], axis=0).astype(stage) for c in range(n_c)]
        vwin = [jnp.concatenate([vp_ref[c, cls], vc_ref[c, cls]], axis=0).astype(stage) for c in range(n_c)]
        for t0 in range(0, rows // ni, per_store):
            done = [sub_tile(qs, kwin, vwin, t) for t in range(t0, t0 + per_store)]
            rs = slice(t0 * ni, (t0 + per_store) * ni)
            for hp in range(n_pair):
                for c in range(n_c):
                    cs = slice(c * ni, (c + 1) * ni)
                    o = jnp.concatenate([d[hp][0][cs] for d in done], axis=0)
                    lse = jnp.concatenate([d[hp][1][cs] for d in done], axis=0)
                    if chained:
                        o_p, lse_p = po_ref[c, cls, rs, pair(hp)].astype(F32), pl_ref[c, cls, rs, pair(hp)]
                        top = jnp.maximum(lse_p, lse)
                        e_p, e_c = jnp.exp(lse_p - top), jnp.exp(lse - top)
                        o = (e_p * o_p + e_c * o) / (e_p + e_c)
                        lse = top + jnp.log(e_p + e_c)
                    o_ref[c, cls, rs, pair(hp)] = o.astype(o_ref.dtype)
                    if not last:
                        lse_ref[c, cls, rs, pair(hp)] = lse


def _attention_group(z, bm, g, batch, seq, done=None):
    _, d = ATTN_GROUPS[g]
    n_c, n_r = SLABS // d, d
    ni, n_prev, rows, n_cls = ATTN_TILES[g]
    si = seq // SLABS
    rows = min(rows, si)
    z5 = z.reshape(batch, n_c, n_r, si, z.shape[1])
    kb = ATTN_QW // ATTN_OUT_W
    hist = max(n_prev, BF16_ROWS)
    prev_per_step = rows // hist

    def cur(off):
        return pl.BlockSpec((None, n_c, n_cls, rows, ATTN_OUT_W), lambda b, r, i: (b, 0, r, i, off + g))

    def prev(off):
        return pl.BlockSpec((None, n_c, n_cls, hist, ATTN_OUT_W),
                            lambda b, r, i: (b, 0, r, jnp.maximum(i * prev_per_step - 1, 0), off + g))

    out_spec = pl.BlockSpec((None, n_c, n_cls, rows, ATTN_OUT_W), lambda b, r, i: (b, 0, r, i, 0))
    last = g == N_GROUPS - 1
    shape5 = (batch, n_c, n_r, si, ATTN_OUT_W)
    prev_args = [] if done is None else [a.reshape(shape5) for a in done]
    res = pl.pallas_call(
        functools.partial(_attn_kernel, ni=ni, n_prev=n_prev, chained=done is not None, last=last),
        grid=(batch, n_r // n_cls, si // rows),
        in_specs=[cur(0), prev(kb), cur(kb), prev(2 * kb), cur(2 * kb),
                  pl.BlockSpec(bm.shape, lambda b, r, i: (0, 0, 0))] + [out_spec] * len(prev_args),
        out_specs=[out_spec] * (1 if last else 2),
        out_shape=[jax.ShapeDtypeStruct(shape5, BF16)] + ([] if last else [jax.ShapeDtypeStruct(shape5, F32)]),
        compiler_params=_params("parallel", "parallel", "arbitrary"),
        name=f"attn_g{g}",
    )(z5, z5, z5, z5, z5, bm, *prev_args)
    return tuple(a.reshape(batch * seq, ATTN_OUT_W) for a in res)


def _attention(z, bands, batch, seq):
    done = None
    for g in range(N_GROUPS):
        done = _attention_group(z, bands[g], g, batch, seq, done)
    return done[0]


def _ssm_ops_kernel(v0r_ref, v0i_ref, cmr_ref, cmi_ref, pwr_ref, pwi_ref, pwrt_ref, pwit_ref,
                    wd_ref, wx_ref, v_ref, z_ref):
    t_len, half = SSM_CHUNK, SSM_CHUNK // 2
    blk = lambda i: slice(i * LANES, (i + 1) * LANES)
    v0r, v0i = v0r_ref[...], v0i_ref[...]
    cmr, cmi = cmr_ref[...], cmi_ref[...]
    split = lambda a: (a.astype(BF16), (a - a.astype(BF16).astype(F32)).astype(BF16))
    cm = jnp.concatenate([cmr, -cmi], axis=0)
    cm_hi, cm_lo = split(cm)
    cm3 = jnp.concatenate([cm_hi, cm_lo, cm_hi], axis=0)
    kern = []
    for tau in range(t_len):
        pr, pi = pwr_ref[tau:tau + 1, :], pwi_ref[tau:tau + 1, :]
        vt = jnp.concatenate([v0r * pr - v0i * pi, v0r * pi + v0i * pr], axis=1)
        v_ref[blk(t_len - 1 - tau), :] = vt.astype(v_ref.dtype)
        vt_hi, vt_lo = split(vt)
        kern.append(jnp.dot(jnp.concatenate([vt_hi, vt_hi, vt_lo], axis=1), cm3, preferred_element_type=F32))
    zero = jnp.zeros((LANES, LANES), F32)
    for j in range(half):
        wd_ref[blk(j), :] = jnp.concatenate([kern[i - j] if i >= j else zero for i in range(half)],
                                            axis=1).astype(wd_ref.dtype)
        wx_ref[blk(j), :] = jnp.concatenate([kern[half + i - j] for i in range(half)], axis=1).astype(wx_ref.dtype)
    ns = SSM_STATE_LANES
    for i in range(t_len):
        pr, pi = pwrt_ref[:, i + 1:i + 2], pwit_ref[:, i + 1:i + 2]
        z_ref[0:ns, blk(i)] = (cmr * pr - cmi * pi).astype(z_ref.dtype)
        z_ref[ns:2 * ns, blk(i)] = (-(cmr * pi + cmi * pr)).astype(z_ref.dtype)


def _ssm_operators(lam_re, lam_im, log_dt, b_re, b_im, c_re, c_im, d_skip):
    t_len = SSM_CHUNK
    n_l, n_g, n_p = lam_re.shape
    nb, gl, ns = SSM_BLOCKS, SSM_GROUPS_PER_BLOCK, SSM_STATE_LANES
    dt = jnp.exp(log_dt)[..., None]
    lr, li = lam_re, lam_im
    mag = jnp.exp(lr * dt)
    ab_re, ab_im = mag * jnp.cos(li * dt), mag * jnp.sin(li * dt)
    nr, ni = ab_re - 1.0, ab_im
    den = lr * lr + li * li
    z_re, z_im = (nr * lr + ni * li) / den, (ni * lr - nr * li) / den
    bb_re = z_re[..., None] * b_re - z_im[..., None] * b_im
    bb_im = z_re[..., None] * b_im + z_im[..., None] * b_re
    tau = jnp.arange(t_len + 1, dtype=F32)[:, None, None, None]
    pmag = jnp.exp(lr * dt * tau)
    pw_re, pw_im = pmag * jnp.cos(li * dt * tau), pmag * jnp.sin(li * dt * tau)

    eye = jnp.eye(gl, dtype=F32)
    nlb = n_l * nb
    seed_in = lambda b: jnp.einsum('xgpc,gh->xgchp', b.reshape(nlb, gl, n_p, SSM_GROUP), eye).reshape(nlb, LANES, ns)
    seed_out = lambda c: jnp.einsum('xgop,gh->xgpho', c.reshape(nlb, gl, SSM_GROUP, n_p), eye).reshape(nlb, ns, LANES)
    pw_lanes = lambda p: p.reshape(t_len + 1, nlb, ns).transpose(1, 0, 2)
    pw_rows = lambda p: p.reshape(t_len + 1, nlb, ns).transpose(1, 2, 0)

    per_blk = lambda *shape: pl.BlockSpec((None,) + shape, lambda x: (x, 0, 0))
    cw = t_len * LANES
    wd, wx, v, z = pl.pallas_call(
        _ssm_ops_kernel,
        grid=(nlb,),
        in_specs=[per_blk(LANES, ns), per_blk(LANES, ns), per_blk(ns, LANES), per_blk(ns, LANES),
                  per_blk(t_len + 1, ns), per_blk(t_len + 1, ns), per_blk(ns, t_len + 1), per_blk(ns, t_len + 1)],
        out_specs=[per_blk(cw // 2, cw // 2), per_blk(cw // 2, cw // 2), per_blk(cw, 2 * ns), per_blk(2 * ns, cw)],
        out_shape=[jax.ShapeDtypeStruct((nlb, cw // 2, cw // 2), BF16),
                   jax.ShapeDtypeStruct((nlb, cw // 2, cw // 2), BF16),
                   jax.ShapeDtypeStruct((nlb, cw, 2 * ns), BF16),
                   jax.ShapeDtypeStruct((nlb, 2 * ns, cw), BF16)],
        compiler_params=_params("parallel"),
        name="ssm_operators",
    )(seed_in(bb_re), seed_in(bb_im), seed_out(c_re), seed_out(c_im),
      pw_lanes(pw_re), pw_lanes(pw_im), pw_rows(pw_re), pw_rows(pw_im))

    apow = jnp.concatenate([pw_re[t_len].reshape(nlb, 1, ns), pw_im[t_len].reshape(nlb, 1, ns)], axis=-1)
    dsk = jnp.tile(d_skip.reshape(nlb, 1, LANES), (1, 1, t_len))
    return wd, wx, v, z, apow, dsk


def _tap_source(r, back, n_slab):
    return (r - back) % n_slab, (back - r + n_slab - 1) // n_slab


def _ssm_conv_kernel(u_ref, val_ref, gate_ref, wd_ref, wx_ref, v_ref, z_ref, ap_ref, dsk_ref, cw_ref, cb_ref,
                     y_ref, c_ref, s_ref, buf_ref):
    n_batch, n_slab, n_rows, _ = u_ref.shape
    assert n_batch == 2
    nt = SSM_STATE_LANES // LANES
    half = n_slab // 2 * LANES
    lane = lambda t: slice(t * LANES, (t + 1) * LANES)
    dot = functools.partial(jnp.dot, preferred_element_type=F32)
    chunk_rows = lambda b: jnp.concatenate([u_ref[b, r] for r in range(n_slab)], axis=1)

    def fill_shifted(b):
        for r in range(n_slab):
            glu = val_ref[b, r].astype(F32) * jax.nn.sigmoid(gate_ref[b, r].astype(F32))
            for w in range(buf_ref.shape[0]):
                buf_ref[w, r, 0:CONV_PAD, :] = jnp.zeros((CONV_PAD, LANES), F32)
                buf_ref[w, r, w:w + n_rows, :] = glu

    def tap(acc, r, back, rows):
        src, wrap = _tap_source(r, back, n_slab)
        k = CONV_K - 1 - back
        return acc + cw_ref[k:k + 1, :] * buf_ref[wrap, src, rows, :]

    for b in range(n_batch):
        s = dot(chunk_rows(b), v_ref[...])
        for t in range(2 * nt):
            s_ref[t, b * n_rows:(b + 1) * n_rows, :] = s[:, lane(t)]

    ar = [jnp.broadcast_to(ap_ref[:, lane(t)], (n_batch, LANES)) for t in range(nt)]
    ai = [jnp.broadcast_to(ap_ref[:, lane(nt + t)], (n_batch, LANES)) for t in range(nt)]

    def recur(k, carry):
        idx = pl.ds(k, n_batch, stride=n_rows)
        new = []
        for t in range(nt):
            xr, xi = carry[t]
            re_ref, im_ref = s_ref.at[t], s_ref.at[nt + t]
            sr, si = re_ref[idx, :], im_ref[idx, :]
            re_ref[idx, :] = xr
            im_ref[idx, :] = xi
            new.append((ar[t] * xr - ai[t] * xi + sr, ar[t] * xi + ai[t] * xr + si))
        return tuple(new)

    fill_shifted(0)
    bias = cb_ref[...]

    def super_step(j, carry):
        for q in range(CONV_LOOP_ROWS):
            carry = recur(j * CONV_LOOP_ROWS + q, carry)
        rows = pl.ds(pl.multiple_of(j * CONV_LOOP_ROWS, CONV_LOOP_ROWS), CONV_LOOP_ROWS)
        for r_lo in range(0, n_slab, CONV_LOOP_SLABS):
            slabs = range(r_lo, r_lo + CONV_LOOP_SLABS)
            accs = {r: jnp.broadcast_to(bias, (CONV_LOOP_ROWS, LANES)) for r in slabs}
            for back in range(CONV_K):
                for r in slabs:
                    accs[r] = tap(accs[r], r, back, rows)
            for r in slabs:
                c_ref[0, r, rows, :] = accs[r].astype(c_ref.dtype)
        return carry

    zero = jnp.zeros((n_batch, LANES), F32)
    lax.fori_loop(0, n_rows // CONV_LOOP_ROWS, super_step, tuple((zero, zero) for _ in range(nt)))

    def ssm_items(b):
        env = {}

        def load():
            env["u"] = chunk_rows(b)
            rs = slice(b * n_rows, (b + 1) * n_rows)
            env["x"] = jnp.concatenate([s_ref[t, rs, :] for t in range(2 * nt)], axis=1).astype(BF16)
            env["lo"] = dot(env["u"][:, :half], wd_ref[...])

        def finish():
            u = env["u"]
            y = jnp.concatenate([env["lo"], env["hi"]], axis=1) + dsk_ref[...] * u.astype(F32)
            y = jax.nn.gelu(y).astype(y_ref.dtype)
            for r in range(n_slab):
                y_ref[b, r] = y[:, lane(r)]

        return [load,
                lambda: env.__setitem__("lo", env["lo"] + dot(env["x"], z_ref[:, :half])),
                lambda: env.__setitem__("hi", dot(env["u"][:, :half], wx_ref[...])),
                lambda: env.__setitem__("hi", env["hi"] + dot(env["u"][:, half:], wd_ref[...])),
                lambda: env.__setitem__("hi", env["hi"] + dot(env["x"], z_ref[:, half:])),
                finish]

    def conv_item(b, r, r0):
        def run():
            acc = jnp.broadcast_to(bias, (CONV_RC, LANES))
            for back in range(CONV_K):
                acc = tap(acc, r, back, slice(r0, r0 + CONV_RC))
            c_ref[b, r, r0:r0 + CONV_RC, :] = acc.astype(c_ref.dtype)
        return run

    fill_shifted(1)
    mxu_items = ssm_items(0) + ssm_items(1)
    vpu_items = [conv_item(1, r, r0) for r in range(n_slab) for r0 in range(0, n_rows, CONV_RC)]
    per_mxu = -(-len(vpu_items) // len(mxu_items))
    for n, item in enumerate(mxu_items):
        item()
        for vpu_item in vpu_items[n * per_mxu:(n + 1) * per_mxu]:
            vpu_item()


def _ssm_conv(z, ops, conv_w, conv_b, layer, batch, seq):
    wd, wx, v, zo, apow, dsk = ops
    si = seq // SLABS
    nbs = 2
    z4 = z.reshape(batch, SLABS, si, z.shape[1])
    n_lt = CONV_WIDTH // LANES
    assert n_lt == SSM_BLOCKS and batch % nbs == 0 and si % CONV_RC == 0
    tok = lambda col0: pl.BlockSpec((nbs, SLABS, si, LANES), lambda c, b: (b, 0, 0, col0 + c))
    op = lambda a: pl.BlockSpec((None,) + a.shape[1:], lambda c, b: (layer * SSM_BLOCKS + c, 0, 0))
    per_tile = lambda n: pl.BlockSpec((None, n, LANES), lambda c, b: (c, 0, 0))
    w_tiles = conv_w.reshape(CONV_K, n_lt, LANES).transpose(1, 0, 2)
    out = jax.ShapeDtypeStruct((batch, SLABS, si, SSM_WIDTH), BF16)
    y, c = pl.pallas_call(
        _ssm_conv_kernel,
        grid=(SSM_BLOCKS, batch // nbs),
        in_specs=[tok(COL_U // LANES), tok(COL_CONV // LANES), tok((COL_CONV + CONV_WIDTH) // LANES),
                  op(wd), op(wx), op(v), op(zo), op(apow), op(dsk), per_tile(CONV_K), per_tile(1)],
        out_specs=[tok(0), tok(0)],
        out_shape=[out, out],
        scratch_shapes=[pltpu.VMEM((2 * SSM_STATE_LANES // LANES, nbs * si, LANES), F32),
                        pltpu.VMEM((3, SLABS, si + CONV_PAD, LANES), F32)],
        compiler_params=_params("parallel", "arbitrary"),
        name="ssm_conv",
    )(z4, z4, z4, wd, wx, v, zo, apow, dsk, w_tiles, conv_b.reshape(n_lt, 1, LANES))
    return y.reshape(batch * seq, SSM_WIDTH), c.reshape(batch * seq, CONV_WIDTH)


def _merge_kernel(a_ref, y_ref, c_ref, lg_ref, lb_ref, g0_ref, g1_ref, g2_ref, wao_ref, wa_ref, wb_ref, wco_ref,
                  o_ref, cact_ref):
    dot = functools.partial(jnp.dot, preferred_element_type=F32)
    sig = lambda r: jax.nn.sigmoid(r[...].astype(F32))

    def gated_sum(normalise):
        y = y_ref[...]
        o_a = dot(a_ref[...], wao_ref[...])
        o_s = dot(y, wa_ref[...]) * jax.nn.sigmoid(dot(y, wb_ref[...]))
        if normalise:
            c = c_ref[...].astype(F32)
            mu = jnp.mean(c, axis=-1, keepdims=True)
            var = jnp.mean(jnp.square(c - mu), axis=-1, keepdims=True)
            ln = (c - mu) * lax.rsqrt(var + EPS) * lg_ref[...] + lb_ref[...]
            cact_ref[...] = jax.nn.silu(ln).astype(cact_ref.dtype)
        o_c = dot(cact_ref[...], wco_ref[...])
        o_ref[...] = (sig(g0_ref) * o_a + sig(g1_ref) * o_s + sig(g2_ref) * o_c).astype(o_ref.dtype)

    first = pl.program_id(1) == 0
    pl.when(first)(lambda: gated_sum(True))
    pl.when(jnp.logical_not(first))(lambda: gated_sum(False))


def _merge(attn, y, conv, ln_g, ln_b, z, w_attn_out, w_glu_a, w_glu_b, w_conv_out, layer, tm=1024, tn=512):
    m = attn.shape[0]
    d = w_attn_out.shape[2]
    nt = d // tn
    col0 = COL_GATES // tn
    tok = lambda a: pl.BlockSpec((tm, a.shape[1]), lambda i, j: (i, 0))
    gate = lambda br: pl.BlockSpec((tm, tn), lambda i, j: (i, col0 + br * nt + j))
    wgt = lambda w: pl.BlockSpec((None, w.shape[1], tn), lambda i, j: (layer, 0, j))
    cw = conv.shape[1]
    vec = pl.BlockSpec((1, cw), lambda i, j: (0, 0))
    return pl.pallas_call(
        _merge_kernel,
        grid=(m // tm, nt),
        in_specs=[tok(attn), tok(y), tok(conv), vec, vec, gate(0), gate(1), gate(2),
                  wgt(w_attn_out), wgt(w_glu_a), wgt(w_glu_b), wgt(w_conv_out)],
        out_specs=pl.BlockSpec((tm, tn), lambda i, j: (i, j)),
        out_shape=jax.ShapeDtypeStruct((m, d), BF16),
        scratch_shapes=[pltpu.VMEM((tm, cw), BF16)],
        compiler_params=_params("parallel", "arbitrary"),
        name="merge",
    )(attn, y, conv, ln_g.reshape(1, cw), ln_b.reshape(1, cw), z, z, z, w_attn_out, w_glu_a, w_glu_b, w_conv_out)


def _residual_proj_kernel(a_ref, w_ref, x_ref, g_ref, xo_ref, ho_ref, acc_ref):
    k = pl.program_id(1)

    @pl.when(k == 0)
    def _():
        acc_ref[...] = x_ref[...]

    acc_ref[...] += jnp.dot(a_ref[...], w_ref[...], preferred_element_type=F32)

    @pl.when(k == pl.num_programs(1) - 1)
    def _():
        x = acc_ref[...]
        xo_ref[...] = x
        ho_ref[...] = _rmsnorm_rows(x, g_ref[...]).astype(ho_ref.dtype)


def _residual_proj(a, w_all, layer, x, g, h_dtype, tm, tk, name):
    m, kdim = a.shape
    d = w_all.shape[2]
    row = lambda i, k: (i, 0)
    return pl.pallas_call(
        _residual_proj_kernel,
        grid=(m // tm, kdim // tk),
        in_specs=[pl.BlockSpec((tm, tk), lambda i, k: (i, k)),
                  pl.BlockSpec((None, tk, d), lambda i, k: (layer, k, 0)),
                  pl.BlockSpec((tm, d), row),
                  pl.BlockSpec((1, d), lambda i, k: (0, 0))],
        out_specs=[pl.BlockSpec((tm, d), row), pl.BlockSpec((tm, d), row)],
        out_shape=[jax.ShapeDtypeStruct((m, d), F32), jax.ShapeDtypeStruct((m, d), h_dtype)],
        scratch_shapes=[pltpu.VMEM((tm, d), F32)],
        compiler_params=_params("parallel", "arbitrary"),
        name=name,
    )(a, w_all, x, g.reshape(1, d))


def _ffn_down_kernel(a_ref, w_ref, g_ref, x_hbm, xo_hbm, ho_hbm, acc_ref, ho_buf, sem_x, sem_out):
    i, k = pl.program_id(0), pl.program_id(1)
    n_i, n_k = pl.num_programs(0), pl.num_programs(1)
    slot = lax.rem(i, 2)
    x_copy = lambda tile, s: pltpu.make_async_copy(x_hbm.at[tile], acc_ref.at[s], sem_x.at[s])
    xo_copy = lambda tile, s: pltpu.make_async_copy(acc_ref.at[s], xo_hbm.at[tile], sem_out.at[0])
    ho_copy = lambda tile: pltpu.make_async_copy(ho_buf, ho_hbm.at[tile], sem_out.at[1])

    @pl.when((i == 0) & (k == 0))
    def _():
        x_copy(0, 0).start()

    @pl.when(k == 0)
    def _():
        x_copy(i, slot).wait()

    @pl.when((k == 1) & (i > 0))
    def _():
        xo_copy(i - 1, 1 - slot).wait()
        ho_copy(i - 1).wait()

    @pl.when((k == 1) & (i + 1 < n_i))
    def _():
        x_copy(i + 1, 1 - slot).start()

    acc_ref[slot] += jnp.dot(a_ref[...], w_ref[...], preferred_element_type=F32)

    @pl.when(k == n_k - 1)
    def _():
        ho_buf[...] = _rmsnorm_rows(acc_ref[slot], g_ref[...]).astype(ho_buf.dtype)
        xo_copy(i, slot).start()
        ho_copy(i).start()

    @pl.when((k == n_k - 1) & (i == n_i - 1))
    def _():
        xo_copy(i, slot).wait()
        ho_copy(i).wait()


def _ffn_down(a, w_all, layer, x, g, h_dtype, tm=1024, tk=D_FF // 4):
    m, kdim = a.shape
    d = w_all.shape[2]
    n_k = kdim // tk
    assert n_k >= 2, "the write-back wait and the prefetch are issued at contraction step 1"
    tiles = lambda dtype: jax.ShapeDtypeStruct((m // tm, tm, d), dtype)
    hbm = pl.BlockSpec(memory_space=pl.ANY)
    xo, ho = pl.pallas_call(
        _ffn_down_kernel,
        grid=(m // tm, n_k),
        in_specs=[pl.BlockSpec((tm, tk), lambda i, k: (i, k)),
                  pl.BlockSpec((None, tk, d), lambda i, k: (layer, k, 0)),
                  pl.BlockSpec((1, d), lambda i, k: (0, 0)),
                  hbm],
        out_specs=[hbm, hbm],
        out_shape=[tiles(F32), tiles(h_dtype)],
        scratch_shapes=[pltpu.VMEM((2, tm, d), F32), pltpu.VMEM((tm, d), h_dtype),
                        pltpu.SemaphoreType.DMA((2,)), pltpu.SemaphoreType.DMA((2,))],
        compiler_params=_params("arbitrary", "arbitrary"),
        name="ffn_down",
    )(a, w_all, g.reshape(1, d), x.reshape(m // tm, tm, d))
    return xo.reshape(m, d), ho.reshape(m, d)


def _ffn_up_kernel(h_ref, wg_ref, wu_ref, o_ref, wgb_ref, wub_ref):
    @pl.when(pl.program_id(1) == 0)
    def _():
        wgb_ref[...] = wg_ref[...].astype(BF16)
        wub_ref[...] = wu_ref[...].astype(BF16)

    h = h_ref[...]
    gate = jnp.dot(h, wgb_ref[...], preferred_element_type=F32)
    up = jnp.dot(h, wub_ref[...], preferred_element_type=F32)
    o_ref[...] = (jax.nn.silu(gate) * up).astype(o_ref.dtype)


def _ffn_up(h, wg_all, wu_all, layer, tm=1024, tn=512):
    m, k = h.shape
    n = wg_all.shape[2]
    wspec = pl.BlockSpec((None, k, tn), lambda j, i: (layer, 0, j))
    return pl.pallas_call(
        _ffn_up_kernel,
        grid=(n // tn, m // tm),
        in_specs=[pl.BlockSpec((tm, k), lambda j, i: (i, 0)), wspec, wspec],
        out_specs=pl.BlockSpec((tm, tn), lambda j, i: (i, j)),
        out_shape=jax.ShapeDtypeStruct((m, n), BF16),
        scratch_shapes=[pltpu.VMEM((k, tn), BF16), pltpu.VMEM((k, tn), BF16)],
        compiler_params=_params("arbitrary", "arbitrary"),
        name="ffn_up",
    )(h, wg_all, wu_all)


def kernel(x, w_in, rel_bias, lam_re, lam_im, log_dt, b_re, b_im, c_re, c_im, d_skip, w_ssm_glu_a, w_ssm_glu_b, conv_w, conv_b, conv_ln_g, conv_ln_b, w_conv_out, w_attn_out, w_out, norm_mix_g, norm_ffn_g, w_ffn_gate, w_ffn_up, w_ffn_down, norm_final_g):
    batch, seq, d_model = x.shape
    m = batch * seq
    si = seq // SLABS
    xf = x.reshape(batch, si, SLABS, d_model).transpose(0, 2, 1, 3).reshape(m, d_model)
    bands = [_band_bias(rel_bias, g) for g in range(N_GROUPS)]
    ssm_ops = _ssm_operators(lam_re, lam_im, log_dt, b_re, b_im, c_re, c_im, d_skip)
    col_scale = jnp.where(jnp.arange(w_in.shape[2]) < ATTN_QW, ATTN_SCALE, 1.0).astype(F32)[None, :]
    bf = lambda w: w.astype(BF16)
    w_attn_out, w_ssm_glu_a, w_ssm_glu_b, w_conv_out = bf(w_attn_out), bf(w_ssm_glu_a), bf(w_ssm_glu_b), bf(w_conv_out)
    w_out, w_ffn_down = bf(w_out), bf(w_ffn_down)

    h = _rmsnorm(xf, norm_mix_g[0], BF16)
    for l in range(DEPTH):
        z = _inproj(h, w_in, l, col_scale)
        attn = _attention(z, bands, batch, seq)
        y, conv = _ssm_conv(z, ssm_ops, conv_w[l], conv_b[l], l, batch, seq)
        merged = _merge(attn, y, conv, conv_ln_g[l], conv_ln_b[l], z, w_attn_out, w_ssm_glu_a, w_ssm_glu_b,
                        w_conv_out, l)
        xf, h = _residual_proj(merged, w_out, l, xf, norm_ffn_g[l], BF16, 512, d_model, "out_proj")
        act = _ffn_up(h, w_ffn_gate, w_ffn_up, l)
        last = l == DEPTH - 1
        g_next = norm_final_g if last else norm_mix_g[l + 1]
        xf, h = _ffn_down(act, w_ffn_down, l, xf, g_next, F32 if last else BF16)
    return h.reshape(batch, SLABS, si, d_model).transpose(0, 2, 1, 3).reshape(batch, seq, d_model)
```

```python
import functools
import math

import jax
import jax.numpy as jnp
import numpy as np
from jax import lax
from jax.experimental import pallas as pl
from jax.experimental.pallas import tpu as pltpu

F32 = jnp.float32
BF16 = jnp.bfloat16

D_MODEL = 2048
DEPTH = 4
HEAD_DIM = 64
HEADS_PER_GROUP = 8
ATTN_GROUPS = ((128, 1), (512, 4), (2048, 16))
N_GROUPS = len(ATTN_GROUPS)
N_ATTN_HEADS = HEADS_PER_GROUP * N_GROUPS
ATTN_QW = N_ATTN_HEADS * HEAD_DIM
ATTN_OUT_W = HEADS_PER_GROUP * HEAD_DIM
ATTN_LAGS = ATTN_GROUPS[0][0] // ATTN_GROUPS[0][1]
NUM_BUCKETS = 32
MAX_DISTANCE = max(w for w, _ in ATTN_GROUPS)
ATTN_SCALE = HEAD_DIM ** -0.5
NEG_INF = -1e30
SSM_WIDTH = 1024
SSM_GROUP = 16
SSM_STATE = 64
CONV_WIDTH = 1024
CONV_K = 31
N_BRANCHES = 3
D_FF = -(-8 * D_MODEL // (3 * 256)) * 256
EPS = 1e-6
COL_U = 3 * ATTN_QW
COL_CONV = COL_U + SSM_WIDTH
COL_GATES = COL_CONV + 2 * CONV_WIDTH

LANES = 128
BF16_ROWS = 16
VMEM_LIMIT = 56 * 2 ** 20

SLABS = max(d for _, d in ATTN_GROUPS)
ATTN_TILES = ((8, 8, 64, 1), (32, 32, 128, 2), (128, 128, 256, 4))
HEAD_PAIR_W = 2 * HEAD_DIM
SSM_CHUNK = SLABS
SSM_BLOCKS = SSM_WIDTH // LANES
SSM_GROUPS_PER_BLOCK = LANES // SSM_GROUP
SSM_STATE_LANES = SSM_GROUPS_PER_BLOCK * SSM_STATE
FFN_ROW_CHUNK = 1024
CONV_RC = 64
CONV_PAD = 8
CONV_LOOP_ROWS = 16
CONV_LOOP_SLABS = 8


def _sigmoid(x):
    return 0.5 * jnp.tanh(0.5 * x) + 0.5


def _params(*sem):
    return pltpu.CompilerParams(dimension_semantics=sem, vmem_limit_bytes=VMEM_LIMIT)


def _rmsnorm_rows(x, g):
    return x * lax.rsqrt(jnp.mean(x * x, axis=-1, keepdims=True) + EPS) * g


def _rmsnorm_kernel(x_ref, g_ref, o_ref):
    o_ref[...] = _rmsnorm_rows(x_ref[...], g_ref[...]).astype(o_ref.dtype)


def _rmsnorm(x, g, out_dtype, tm=512):
    m, d = x.shape
    return pl.pallas_call(
        _rmsnorm_kernel,
        grid=(m // tm,),
        in_specs=[pl.BlockSpec((tm, d), lambda i: (i, 0)),
                  pl.BlockSpec((1, d), lambda i: (0, 0))],
        out_specs=pl.BlockSpec((tm, d), lambda i: (i, 0)),
        out_shape=jax.ShapeDtypeStruct((m, d), out_dtype),
        compiler_params=_params("parallel"),
        name="rmsnorm",
    )(x, g.reshape(1, d))


def _inproj_kernel(a_ref, w_ref, scale_ref, o_ref, wb_ref):
    @pl.when(pl.program_id(1) == 0)
    def _():
        wb_ref[...] = (w_ref[...] * scale_ref[...]).astype(BF16)

    o_ref[...] = jnp.dot(a_ref[...], wb_ref[...], preferred_element_type=F32).astype(o_ref.dtype)


def _inproj(h, w_all, layer, col_scale, tm=1024, tn=1536):
    m, k = h.shape
    n = w_all.shape[2]
    return pl.pallas_call(
        _inproj_kernel,
        grid=(n // tn, m // tm),
        in_specs=[pl.BlockSpec((tm, k), lambda j, i: (i, 0)),
                  pl.BlockSpec((None, k, tn), lambda j, i: (layer, 0, j)),
                  pl.BlockSpec((1, tn), lambda j, i: (0, j))],
        out_specs=pl.BlockSpec((tm, tn), lambda j, i: (i, j)),
        out_shape=jax.ShapeDtypeStruct((m, n), BF16),
        scratch_shapes=[pltpu.VMEM((k, tn), BF16)],
        compiler_params=_params("arbitrary", "arbitrary"),
        name="inproj",
    )(h, w_all, col_scale)


def _t5_bucket_causal(dist):
    max_exact = NUM_BUCKETS // 2
    d = np.maximum(dist, 1).astype(np.float32)
    large = max_exact + (np.log(d / max_exact) / math.log(MAX_DISTANCE / max_exact)
                         * (NUM_BUCKETS - max_exact)).astype(np.int32)
    large = np.minimum(large, NUM_BUCKETS - 1)
    return np.where(dist < max_exact, dist, large).astype(np.int32)


def _band_bias(rel_bias, g):
    _, d = ATTN_GROUPS[g]
    n_c = SLABS // d
    ni, n_prev = ATTN_TILES[g][:2]
    qc, qi = np.divmod(np.arange(n_c * ni), ni)
    kc, ki = np.divmod(np.arange(n_c * (n_prev + ni)), n_prev + ni)
    lag = n_c * (qi[:, None] - ki[None, :] + n_prev) + (qc[:, None] - kc[None, :])
    on_band = (lag >= 0) & (lag <= ATTN_LAGS)
    bucket = np.where(on_band, _t5_bucket_causal(np.clip(lag, 0, ATTN_LAGS) * d), -1).astype(np.int32)

    def band_kernel(rel_ref, bucket_ref, out_ref):
        bkt = bucket_ref[...]
        for h in range(HEADS_PER_GROUP):
            acc = jnp.full(bkt.shape, NEG_INF, F32)
            for b in range(NUM_BUCKETS):
                acc = jnp.where(bkt == b, rel_ref[b, g * HEADS_PER_GROUP + h], acc)
            out_ref[h] = acc

    return pl.pallas_call(
        band_kernel,
        in_specs=[pl.BlockSpec(memory_space=pltpu.SMEM), pl.BlockSpec(memory_space=pltpu.VMEM)],
        out_specs=pl.BlockSpec(memory_space=pltpu.VMEM),
        out_shape=jax.ShapeDtypeStruct((HEADS_PER_GROUP,) + bucket.shape, F32),
        name=f"band_bias_g{g}",
    )(rel_bias, jnp.asarray(bucket))


def _attn_kernel(*refs, ni, n_prev, chained, last):
    q_ref, kp_ref, kc_ref, vp_ref, vc_ref, bm_ref = refs[:6]
    po_ref, pl_ref = refs[6:8] if chained else (None, None)
    o_ref = refs[8 if chained else 6]
    lse_ref = None if last else refs[-1]
    tile = pl.program_id(2)
    n_c, n_cls, rows, _ = q_ref.shape
    hist = kp_ref.shape[2]
    win = n_prev + ni
    n_pair = HEADS_PER_GROUP // 2
    pair = lambda hp: slice(hp * HEAD_PAIR_W, (hp + 1) * HEAD_PAIR_W)
    stage = BF16 if ni % BF16_ROWS == 0 else F32
    per_store = max(1, BF16_ROWS // ni)
    upper = lax.broadcasted_iota(jnp.int32, (1, HEAD_PAIR_W), 1) >= HEAD_DIM
    nt_dims = (((1,), (1,)), ((), ()))

    def sub_tile(qs, kwin, vwin, t):
        r0, k0 = t * ni, hist - n_prev + t * ni
        q = jnp.concatenate([qs[c][r0:r0 + ni] for c in range(n_c)], axis=0).astype(BF16)
        kk = jnp.concatenate([kwin[c][k0:k0 + win] for c in range(n_c)], axis=0).astype(BF16)
        vv = jnp.concatenate([vwin[c][k0:k0 + win] for c in range(n_c)], axis=0).astype(BF16)
        logits = []
        for h in range(HEADS_PER_GROUP):
            q2 = q[:, pair(h // 2)]
            qm = jnp.where(upper if h % 2 else ~upper, q2, jnp.zeros_like(q2))
            s = lax.dot_general(qm, kk[:, pair(h // 2)], nt_dims, preferred_element_type=F32) + bm_ref[h]
            if t == 0:
                col = lax.broadcasted_iota(jnp.int32, s.shape, 1)
                s = jnp.where((col % win >= n_prev) | (tile > 0), s, NEG_INF)
            logits.append(s)
        mx = [jnp.max(s, axis=-1, keepdims=True) for s in logits]
        probs = [jnp.exp(s - m).astype(BF16) for s, m in zip(logits, mx)]
        out = []
        for hp in range(n_pair):
            v2 = vv[:, pair(hp)]
            one = jnp.ones_like(v2)
            o0 = jnp.dot(probs[2 * hp], jnp.where(upper, one, v2), preferred_element_type=F32)
            o1 = jnp.dot(probs[2 * hp + 1], jnp.where(upper, v2, one), preferred_element_type=F32)
            den = pltpu.roll(jnp.where(upper, o0, o1), HEAD_DIM, axis=1)
            out.append((jnp.where(upper, o1, o0) * (1.0 / den),
                        jnp.where(upper, mx[2 * hp + 1], mx[2 * hp]) + jnp.log(den)))
        return out

    for cls in range(n_cls):
        qs = [q_ref[c, cls].astype(stage) for c in range(n_c)]
        kwin = [jnp.concatenate([kp_ref[c, cls], kc_ref[c, cls]], axis=0).astype(stage) for c in range(n_c)]
        vwin = [jnp.concatenate([vp_ref[c, cls], vc_ref[c, cls]], axis=0).astype(stage) for c in range(n_c)]
        for t0 in range(0, rows // ni, per_store):
            done = [sub_tile(qs, kwin, vwin, t) for t in range(t0, t0 + per_store)]
            rs = slice(t0 * ni, (t0 + per_store) * ni)
            for hp in range(n_pair):
                for c in range(n_c):
                    cs = slice(c * ni, (c + 1) * ni)
                    o = jnp.concatenate([d[hp][0][cs] for d in done], axis=0)
                    lse = jnp.concatenate([d[hp][1][cs] for d in done], axis=0)
                    if chained:
                        o_p, lse_p = po_ref[c, cls, rs, pair(hp)].astype(F32), pl_ref[c, cls, rs, pair(hp)]
                        top = jnp.maximum(lse_p, lse)
                        e_p, e_c = jnp.exp(lse_p - top), jnp.exp(lse - top)
                        o = (e_p * o_p + e_c * o) / (e_p + e_c)
                        lse = top + jnp.log(e_p + e_c)
                    o_ref[c, cls, rs, pair(hp)] = o.astype(o_ref.dtype)
                    if not last:
                        lse_ref[c, cls, rs, pair(hp)] = lse


def _attention_group(z, bm, g, batch, seq, done=None):
    _, d = ATTN_GROUPS[g]
    n_c, n_r = SLABS // d, d
    ni, n_prev, rows, n_cls = ATTN_TILES[g]
    si = seq // SLABS
    rows = min(rows, si)
    z5 = z.reshape(batch, n_c, n_r, si, z.shape[1])
    kb = ATTN_QW // ATTN_OUT_W
    hist = max(n_prev, BF16_ROWS)
    prev_per_step = rows // hist

    def cur(off):
        return pl.BlockSpec((None, n_c, n_cls, rows, ATTN_OUT_W), lambda b, r, i: (b, 0, r, i, off + g))

    def prev(off):
        return pl.BlockSpec((None, n_c, n_cls, hist, ATTN_OUT_W),
                            lambda b, r, i: (b, 0, r, jnp.maximum(i * prev_per_step - 1, 0), off + g))

    out_spec = pl.BlockSpec((None, n_c, n_cls, rows, ATTN_OUT_W), lambda b, r, i: (b, 0, r, i, 0))
    last = g == N_GROUPS - 1
    shape5 = (batch, n_c, n_r, si, ATTN_OUT_W)
    prev_args = [] if done is None else [a.reshape(shape5) for a in done]
    res = pl.pallas_call(
        functools.partial(_attn_kernel, ni=ni, n_prev=n_prev, chained=done is not None, last=last),
        grid=(batch, n_r // n_cls, si // rows),
        in_specs=[cur(0), prev(kb), cur(kb), prev(2 * kb), cur(2 * kb),
                  pl.BlockSpec(bm.shape, lambda b, r, i: (0, 0, 0))] + [out_spec] * len(prev_args),
        out_specs=[out_spec] * (1 if last else 2),
        out_shape=[jax.ShapeDtypeStruct(shape5, BF16)] + ([] if last else [jax.ShapeDtypeStruct(shape5, F32)]),
        compiler_params=_params("parallel", "parallel", "arbitrary"),
        name=f"attn_g{g}",
    )(z5, z5, z5, z5, z5, bm, *prev_args)
    return tuple(a.reshape(batch * seq, ATTN_OUT_W) for a in res)


def _attention(z, bands, batch, seq):
    done = None
    for g in range(N_GROUPS):
        done = _attention_group(z, bands[g], g, batch, seq, done)
    return done[0]


def _ssm_ops_kernel(v0r_ref, v0i_ref, cmr_ref, cmi_ref, pwr_ref, pwi_ref, pwrt_ref, pwit_ref,
                    wd_ref, wx_ref, v_ref, z_ref):
    t_len, half = SSM_CHUNK, SSM_CHUNK // 2
    blk = lambda i: slice(i * LANES, (i + 1) * LANES)
    v0r, v0i = v0r_ref[...], v0i_ref[...]
    cmr, cmi = cmr_ref[...], cmi_ref[...]
    split = lambda a: (a.astype(BF16), (a - a.astype(BF16).astype(F32)).astype(BF16))
    cm = jnp.concatenate([cmr, -cmi], axis=0)
    cm_hi, cm_lo = split(cm)
    cm3 = jnp.concatenate([cm_hi, cm_lo, cm_hi], axis=0)
    kern = []
    for tau in range(t_len):
        pr, pi = pwr_ref[tau:tau + 1, :], pwi_ref[tau:tau + 1, :]
        vt = jnp.concatenate([v0r * pr - v0i * pi, v0r * pi + v0i * pr], axis=1)
        v_ref[blk(t_len - 1 - tau), :] = vt.astype(v_ref.dtype)
        vt_hi, vt_lo = split(vt)
        kern.append(jnp.dot(jnp.concatenate([vt_hi, vt_hi, vt_lo], axis=1), cm3, preferred_element_type=F32))
    zero = jnp.zeros((LANES, LANES), F32)
    for j in range(half):
        wd_ref[blk(j), :] = jnp.concatenate([kern[i - j] if i >= j else zero for i in range(half)],
                                            axis=1).astype(wd_ref.dtype)
        wx_ref[blk(j), :] = jnp.concatenate([kern[half + i - j] for i in range(half)], axis=1).astype(wx_ref.dtype)
    ns = SSM_STATE_LANES
    for i in range(t_len):
        pr, pi = pwrt_ref[:, i + 1:i + 2], pwit_ref[:, i + 1:i + 2]
        z_ref[0:ns, blk(i)] = (cmr * pr - cmi * pi).astype(z_ref.dtype)
        z_ref[ns:2 * ns, blk(i)] = (-(cmr * pi + cmi * pr)).astype(z_ref.dtype)


def _ssm_operators(lam_re, lam_im, log_dt, b_re, b_im, c_re, c_im, d_skip):
    t_len = SSM_CHUNK
    n_l, n_g, n_p = lam_re.shape
    nb, gl, ns = SSM_BLOCKS, SSM_GROUPS_PER_BLOCK, SSM_STATE_LANES
    dt = jnp.exp(log_dt)[..., None]
    lr, li = lam_re, lam_im
    mag = jnp.exp(lr * dt)
    ab_re, ab_im = mag * jnp.cos(li * dt), mag * jnp.sin(li * dt)
    nr, ni = ab_re - 1.0, ab_im
    den = lr * lr + li * li
    z_re, z_im = (nr * lr + ni * li) / den, (ni * lr - nr * li) / den
    bb_re = z_re[..., None] * b_re - z_im[..., None] * b_im
    bb_im = z_re[..., None] * b_im + z_im[..., None] * b_re
    tau = jnp.arange(t_len + 1, dtype=F32)[:, None, None, None]
    pmag = jnp.exp(lr * dt * tau)
    pw_re, pw_im = pmag * jnp.cos(li * dt * tau), pmag * jnp.sin(li * dt * tau)

    eye = jnp.eye(gl, dtype=F32)
    nlb = n_l * nb
    seed_in = lambda b: jnp.einsum('xgpc,gh->xgchp', b.reshape(nlb, gl, n_p, SSM_GROUP), eye).reshape(nlb, LANES, ns)
    seed_out = lambda c: jnp.einsum('xgop,gh->xgpho', c.reshape(nlb, gl, SSM_GROUP, n_p), eye).reshape(nlb, ns, LANES)
    pw_lanes = lambda p: p.reshape(t_len + 1, nlb, ns).transpose(1, 0, 2)
    pw_rows = lambda p: p.reshape(t_len + 1, nlb, ns).transpose(1, 2, 0)

    per_blk = lambda *shape: pl.BlockSpec((None,) + shape, lambda x: (x, 0, 0))
    cw = t_len * LANES
    wd, wx, v, z = pl.pallas_call(
        _ssm_ops_kernel,
        grid=(nlb,),
        in_specs=[per_blk(LANES, ns), per_blk(LANES, ns), per_blk(ns, LANES), per_blk(ns, LANES),
                  per_blk(t_len + 1, ns), per_blk(t_len + 1, ns), per_blk(ns, t_len + 1), per_blk(ns, t_len + 1)],
        out_specs=[per_blk(cw // 2, cw // 2), per_blk(cw // 2, cw // 2), per_blk(cw, 2 * ns), per_blk(2 * ns, cw)],
        out_shape=[jax.ShapeDtypeStruct((nlb, cw // 2, cw // 2), BF16),
                   jax.ShapeDtypeStruct((nlb, cw // 2, cw // 2), BF16),
                   jax.ShapeDtypeStruct((nlb, cw, 2 * ns), BF16),
                   jax.ShapeDtypeStruct((nlb, 2 * ns, cw), BF16)],
        compiler_params=_params("parallel"),
        name="ssm_operators",
    )(seed_in(bb_re), seed_in(bb_im), seed_out(c_re), seed_out(c_im),
      pw_lanes(pw_re), pw_lanes(pw_im), pw_rows(pw_re), pw_rows(pw_im))

    apow = jnp.concatenate([pw_re[t_len].reshape(nlb, 1, ns), pw_im[t_len].reshape(nlb, 1, ns)], axis=-1)
    dsk = jnp.tile(d_skip.reshape(nlb, 1, LANES), (1, 1, t_len))
    return wd, wx, v, z, apow, dsk


def _tap_source(r, back, n_slab):
    return (r - back) % n_slab, (back - r + n_slab - 1) // n_slab


def _ssm_conv_kernel(u_ref, val_ref, gate_ref, wd_ref, wx_ref, v_ref, z_ref, ap_ref, dsk_ref, cw_ref, cb_ref,
                     y_ref, c_ref, s_ref, buf_ref):
    n_batch, n_slab, n_rows, _ = u_ref.shape
    assert n_batch == 2
    nt = SSM_STATE_LANES // LANES
    half = n_slab // 2 * LANES
    lane = lambda t: slice(t * LANES, (t + 1) * LANES)
    dot = functools.partial(jnp.dot, preferred_element_type=F32)
    chunk_rows = lambda b: jnp.concatenate([u_ref[b, r] for r in range(n_slab)], axis=1)

    def fill_shifted(b):
        for r in range(n_slab):
            glu = val_ref[b, r].astype(F32) * jax.nn.sigmoid(gate_ref[b, r].astype(F32))
            for w in range(buf_ref.shape[0]):
                buf_ref[w, r, 0:CONV_PAD, :] = jnp.zeros((CONV_PAD, LANES), F32)
                buf_ref[w, r, w:w + n_rows, :] = glu

    def tap(acc, r, back, rows):
        src, wrap = _tap_source(r, back, n_slab)
        k = CONV_K - 1 - back
        return acc + cw_ref[k:k + 1, :] * buf_ref[wrap, src, rows, :]

    for b in range(n_batch):
        s = dot(chunk_rows(b), v_ref[...])
        for t in range(2 * nt):
            s_ref[t, b * n_rows:(b + 1) * n_rows, :] = s[:, lane(t)]

    ar = [jnp.broadcast_to(ap_ref[:, lane(t)], (n_batch, LANES)) for t in range(nt)]
    ai = [jnp.broadcast_to(ap_ref[:, lane(nt + t)], (n_batch, LANES)) for t in range(nt)]

    def recur(k, carry):
        idx = pl.ds(k, n_batch, stride=n_rows)
        new = []
        for t in range(nt):
            xr, xi = carry[t]
            re_ref, im_ref = s_ref.at[t], s_ref.at[nt + t]
            sr, si = re_ref[idx, :], im_ref[idx, :]
            re_ref[idx, :] = xr
            im_ref[idx, :] = xi
            new.append((ar[t] * xr - ai[t] * xi + sr, ar[t] * xi + ai[t] * xr + si))
        return tuple(new)

    fill_shifted(0)
    bias = cb_ref[...]

    def super_step(j, carry):
        for q in range(CONV_LOOP_ROWS):
            carry = recur(j * CONV_LOOP_ROWS + q, carry)
        rows = pl.ds(pl.multiple_of(j * CONV_LOOP_ROWS, CONV_LOOP_ROWS), CONV_LOOP_ROWS)
        for r_lo in range(0, n_slab, CONV_LOOP_SLABS):
            slabs = range(r_lo, r_lo + CONV_LOOP_SLABS)
            accs = {r: jnp.broadcast_to(bias, (CONV_LOOP_ROWS, LANES)) for r in slabs}
            for back in range(CONV_K):
                for r in slabs:
                    accs[r] = tap(accs[r], r, back, rows)
            for r in slabs:
                c_ref[0, r, rows, :] = accs[r].astype(c_ref.dtype)
        return carry

    zero = jnp.zeros((n_batch, LANES), F32)
    lax.fori_loop(0, n_rows // CONV_LOOP_ROWS, super_step, tuple((zero, zero) for _ in range(nt)))

    def ssm_items(b):
        env = {}

        def load():
            env["u"] = chunk_rows(b)
            rs = slice(b * n_rows, (b + 1) * n_rows)
            env["x"] = jnp.concatenate([s_ref[t, rs, :] for t in range(2 * nt)], axis=1).astype(BF16)
            env["lo"] = dot(env["u"][:, :half], wd_ref[...])

        def finish():
            u = env["u"]
            y = jnp.concatenate([env["lo"], env["hi"]], axis=1) + dsk_ref[...] * u.astype(F32)
            y = jax.nn.gelu(y).astype(y_ref.dtype)
            for r in range(n_slab):
                y_ref[b, r] = y[:, lane(r)]

        return [load,
                lambda: env.__setitem__("lo", env["lo"] + dot(env["x"], z_ref[:, :half])),
                lambda: env.__setitem__("hi", dot(env["u"][:, :half], wx_ref[...])),
                lambda: env.__setitem__("hi", env["hi"] + dot(env["u"][:, half:], wd_ref[...])),
                lambda: env.__setitem__("hi", env["hi"] + dot(env["x"], z_ref[:, half:])),
                finish]

    def conv_item(b, r, r0):
        def run():
            acc = jnp.broadcast_to(bias, (CONV_RC, LANES))
            for back in range(CONV_K):
                acc = tap(acc, r, back, slice(r0, r0 + CONV_RC))
            c_ref[b, r, r0:r0 + CONV_RC, :] = acc.astype(c_ref.dtype)
        return run

    fill_shifted(1)
    mxu_items = ssm_items(0) + ssm_items(1)
    vpu_items = [conv_item(1, r, r0) for r in range(n_slab) for r0 in range(0, n_rows, CONV_RC)]
    per_mxu = -(-len(vpu_items) // len(mxu_items))
    for n, item in enumerate(mxu_items):
        item()
        for vpu_item in vpu_items[n * per_mxu:(n + 1) * per_mxu]:
            vpu_item()


def _ssm_conv(z, ops, conv_w, conv_b, layer, batch, seq):
    wd, wx, v, zo, apow, dsk = ops
    si = seq // SLABS
    nbs = 2
    z4 = z.reshape(batch, SLABS, si, z.shape[1])
    n_lt = CONV_WIDTH // LANES
    assert n_lt == SSM_BLOCKS and batch % nbs == 0 and si % CONV_RC == 0
    tok = lambda col0: pl.BlockSpec((nbs, SLABS, si, LANES), lambda c, b: (b, 0, 0, col0 + c))
    op = lambda a: pl.BlockSpec((None,) + a.shape[1:], lambda c, b: (layer * SSM_BLOCKS + c, 0, 0))
    per_tile = lambda n: pl.BlockSpec((None, n, LANES), lambda c, b: (c, 0, 0))
    w_tiles = conv_w.reshape(CONV_K, n_lt, LANES).transpose(1, 0, 2)
    out = jax.ShapeDtypeStruct((batch, SLABS, si, SSM_WIDTH), BF16)
    y, c = pl.pallas_call(
        _ssm_conv_kernel,
        grid=(SSM_BLOCKS, batch // nbs),
        in_specs=[tok(COL_U // LANES), tok(COL_CONV // LANES), tok((COL_CONV + CONV_WIDTH) // LANES),
                  op(wd), op(wx), op(v), op(zo), op(apow), op(dsk), per_tile(CONV_K), per_tile(1)],
        out_specs=[tok(0), tok(0)],
        out_shape=[out, out],
        scratch_shapes=[pltpu.VMEM((2 * SSM_STATE_LANES // LANES, nbs * si, LANES), F32),
                        pltpu.VMEM((3, SLABS, si + CONV_PAD, LANES), F32)],
        compiler_params=_params("parallel", "arbitrary"),
        name="ssm_conv",
    )(z4, z4, z4, wd, wx, v, zo, apow, dsk, w_tiles, conv_b.reshape(n_lt, 1, LANES))
    return y.reshape(batch * seq, SSM_WIDTH), c.reshape(batch * seq, CONV_WIDTH)


def _merge_kernel(a_ref, y_ref, c_ref, lg_ref, lb_ref, g0_ref, g1_ref, g2_ref, wao_ref, wa_ref, wb_ref, wco_ref,
                  o_ref, cact_ref):
    dot = functools.partial(jnp.dot, preferred_element_type=F32)
    sig = lambda r: _sigmoid(r[...].astype(F32))

    def gated_sum(normalise):
        y = y_ref[...]
        o_a = dot(a_ref[...], wao_ref[...])
        o_s = dot(y, wa_ref[...]) * _sigmoid(dot(y, wb_ref[...]))
        if normalise:
            c = c_ref[...].astype(F32)
            mu = jnp.mean(c, axis=-1, keepdims=True)
            var = jnp.mean(jnp.square(c - mu), axis=-1, keepdims=True)
            ln = (c - mu) * lax.rsqrt(var + EPS) * lg_ref[...] + lb_ref[...]
            cact_ref[...] = (ln * _sigmoid(ln)).astype(cact_ref.dtype)
        o_c = dot(cact_ref[...], wco_ref[...])
        o_ref[...] = (sig(g0_ref) * o_a + sig(g1_ref) * o_s + sig(g2_ref) * o_c).astype(o_ref.dtype)

    first = pl.program_id(1) == 0
    pl.when(first)(lambda: gated_sum(True))
    pl.when(jnp.logical_not(first))(lambda: gated_sum(False))


def _merge(attn, y, conv, ln_g, ln_b, z, w_attn_out, w_glu_a, w_glu_b, w_conv_out, layer, tm=1024, tn=512):
    m = attn.shape[0]
    d = w_attn_out.shape[2]
    nt = d // tn
    col0 = COL_GATES // tn
    tok = lambda a: pl.BlockSpec((tm, a.shape[1]), lambda i, j: (i, 0))
    gate = lambda br: pl.BlockSpec((tm, tn), lambda i, j: (i, col0 + br * nt + j))
    wgt = lambda w: pl.BlockSpec((None, w.shape[1], tn), lambda i, j: (layer, 0, j))
    cw = conv.shape[1]
    vec = pl.BlockSpec((1, cw), lambda i, j: (0, 0))
    return pl.pallas_call(
        _merge_kernel,
        grid=(m // tm, nt),
        in_specs=[tok(attn), tok(y), tok(conv), vec, vec, gate(0), gate(1), gate(2),
                  wgt(w_attn_out), wgt(w_glu_a), wgt(w_glu_b), wgt(w_conv_out)],
        out_specs=pl.BlockSpec((tm, tn), lambda i, j: (i, j)),
        out_shape=jax.ShapeDtypeStruct((m, d), BF16),
        scratch_shapes=[pltpu.VMEM((tm, cw), BF16)],
        compiler_params=_params("parallel", "arbitrary"),
        name="merge",
    )(attn, y, conv, ln_g.reshape(1, cw), ln_b.reshape(1, cw), z, z, z, w_attn_out, w_glu_a, w_glu_b, w_conv_out)


def _residual_proj_kernel(a_ref, w_ref, x_ref, g_ref, xo_ref, ho_ref, acc_ref):
    k = pl.program_id(1)

    @pl.when(k == 0)
    def _():
        acc_ref[...] = x_ref[...]

    acc_ref[...] += jnp.dot(a_ref[...], w_ref[...], preferred_element_type=F32)

    @pl.when(k == pl.num_programs(1) - 1)
    def _():
        x = acc_ref[...]
        xo_ref[...] = x
        ho_ref[...] = _rmsnorm_rows(x, g_ref[...]).astype(ho_ref.dtype)


def _residual_proj(a, w_all, layer, x, g, h_dtype, tm, tk, name):
    m, kdim = a.shape
    d = w_all.shape[2]
    row = lambda i, k: (i, 0)
    return pl.pallas_call(
        _residual_proj_kernel,
        grid=(m // tm, kdim // tk),
        in_specs=[pl.BlockSpec((tm, tk), lambda i, k: (i, k)),
                  pl.BlockSpec((None, tk, d), lambda i, k: (layer, k, 0)),
                  pl.BlockSpec((tm, d), row),
                  pl.BlockSpec((1, d), lambda i, k: (0, 0))],
        out_specs=[pl.BlockSpec((tm, d), row), pl.BlockSpec((tm, d), row)],
        out_shape=[jax.ShapeDtypeStruct((m, d), F32), jax.ShapeDtypeStruct((m, d), h_dtype)],
        scratch_shapes=[pltpu.VMEM((tm, d), F32)],
        compiler_params=_params("parallel", "arbitrary"),
        name=name,
    )(a, w_all, x, g.reshape(1, d))


def _ffn_down_kernel(a_ref, w_ref, g_ref, x_hbm, xo_hbm, ho_hbm, acc_ref, ho_buf, sem_x, sem_out):
    i, k = pl.program_id(0), pl.program_id(1)
    n_i, n_k = pl.num_programs(0), pl.num_programs(1)
    slot = lax.rem(i, 2)
    x_copy = lambda tile, s: pltpu.make_async_copy(x_hbm.at[tile], acc_ref.at[s], sem_x.at[s])
    xo_copy = lambda tile, s: pltpu.make_async_copy(acc_ref.at[s], xo_hbm.at[tile], sem_out.at[0])
    ho_copy = lambda tile: pltpu.make_async_copy(ho_buf, ho_hbm.at[tile], sem_out.at[1])

    @pl.when((i == 0) & (k == 0))
    def _():
        x_copy(0, 0).start()

    @pl.when(k == 0)
    def _():
        x_copy(i, slot).wait()

    @pl.when((k == 1) & (i > 0))
    def _():
        xo_copy(i - 1, 1 - slot).wait()
        ho_copy(i - 1).wait()

    @pl.when((k == 1) & (i + 1 < n_i))
    def _():
        x_copy(i + 1, 1 - slot).start()

    acc_ref[slot] += jnp.dot(a_ref[...], w_ref[...], preferred_element_type=F32)

    @pl.when(k == n_k - 1)
    def _():
        ho_buf[...] = _rmsnorm_rows(acc_ref[slot], g_ref[...]).astype(ho_buf.dtype)
        xo_copy(i, slot).start()
        ho_copy(i).start()

    @pl.when((k == n_k - 1) & (i == n_i - 1))
    def _():
        xo_copy(i, slot).wait()
        ho_copy(i).wait()


def _ffn_down(a, w_all, layer, x, g, h_dtype, tm=1024, tk=D_FF // 4):
    m, kdim = a.shape
    d = w_all.shape[2]
    n_k = kdim // tk
    assert n_k >= 2, "the write-back wait and the prefetch are issued at contraction step 1"
    tiles = lambda dtype: jax.ShapeDtypeStruct((m // tm, tm, d), dtype)
    hbm = pl.BlockSpec(memory_space=pl.ANY)
    xo, ho = pl.pallas_call(
        _ffn_down_kernel,
        grid=(m // tm, n_k),
        in_specs=[pl.BlockSpec((tm, tk), lambda i, k: (i, k)),
                  pl.BlockSpec((None, tk, d), lambda i, k: (layer, k, 0)),
                  pl.BlockSpec((1, d), lambda i, k: (0, 0)),
                  hbm],
        out_specs=[hbm, hbm],
        out_shape=[tiles(F32), tiles(h_dtype)],
        scratch_shapes=[pltpu.VMEM((2, tm, d), F32), pltpu.VMEM((tm, d), h_dtype),
                        pltpu.SemaphoreType.DMA((2,)), pltpu.SemaphoreType.DMA((2,))],
        compiler_params=_params("arbitrary", "arbitrary"),
        name="ffn_down",
    )(a, w_all, g.reshape(1, d), x.reshape(m // tm, tm, d))
    return xo.reshape(m, d), ho.reshape(m, d)


def _ffn_up_kernel(h_ref, wg_ref, wu_ref, o_ref, wgb_ref, wub_ref):
    @pl.when(pl.program_id(1) == 0)
    def _():
        wgb_ref[...] = wg_ref[...].astype(BF16)
        wub_ref[...] = wu_ref[...].astype(BF16)

    for r0 in range(0, h_ref.shape[0], FFN_ROW_CHUNK):
        rows = slice(r0, r0 + FFN_ROW_CHUNK)
        h = h_ref[rows, :]
        gate = jnp.dot(h, wgb_ref[...], preferred_element_type=F32)
        up = jnp.dot(h, wub_ref[...], preferred_element_type=F32)
        o_ref[rows, :] = (jax.nn.silu(gate) * up).astype(o_ref.dtype)


def _ffn_up(h, wg_all, wu_all, layer, tm=2048, tn=512):
    m, k = h.shape
    n = wg_all.shape[2]
    wspec = pl.BlockSpec((None, k, tn), lambda j, i: (layer, 0, j))
    return pl.pallas_call(
        _ffn_up_kernel,
        grid=(n // tn, m // tm),
        in_specs=[pl.BlockSpec((tm, k), lambda j, i: (i, 0)), wspec, wspec],
        out_specs=pl.BlockSpec((tm, tn), lambda j, i: (i, j)),
        out_shape=jax.ShapeDtypeStruct((m, n), BF16),
        scratch_shapes=[pltpu.VMEM((k, tn), BF16), pltpu.VMEM((k, tn), BF16)],
        compiler_params=_params("arbitrary", "arbitrary"),
        name="ffn_up",
    )(h, wg_all, wu_all)


def kernel(x, w_in, rel_bias, lam_re, lam_im, log_dt, b_re, b_im, c_re, c_im, d_skip, w_ssm_glu_a, w_ssm_glu_b, conv_w, conv_b, conv_ln_g, conv_ln_b, w_conv_out, w_attn_out, w_out, norm_mix_g, norm_ffn_g, w_ffn_gate, w_ffn_up, w_ffn_down, norm_final_g):
    batch, seq, d_model = x.shape
    m = batch * seq
    si = seq // SLABS
    xf = x.reshape(batch, si, SLABS, d_model).transpose(0, 2, 1, 3).reshape(m, d_model)
    bands = [_band_bias(rel_bias, g) for g in range(N_GROUPS)]
    ssm_ops = _ssm_operators(lam_re, lam_im, log_dt, b_re, b_im, c_re, c_im, d_skip)
    col_scale = jnp.where(jnp.arange(w_in.shape[2]) < ATTN_QW, ATTN_SCALE, 1.0).astype(F32)[None, :]
    bf = lambda w: w.astype(BF16)
    w_attn_out, w_ssm_glu_a, w_ssm_glu_b, w_conv_out = bf(w_attn_out), bf(w_ssm_glu_a), bf(w_ssm_glu_b), bf(w_conv_out)
    w_out, w_ffn_down = bf(w_out), bf(w_ffn_down)

    h = _rmsnorm(xf, norm_mix_g[0], BF16)
    for l in range(DEPTH):
        z = _inproj(h, w_in, l, col_scale)
        attn = _attention(z, bands, batch, seq)
        y, conv = _ssm_conv(z, ssm_ops, conv_w[l], conv_b[l], l, batch, seq)
        merged = _merge(attn, y, conv, conv_ln_g[l], conv_ln_b[l], z, w_attn_out, w_ssm_glu_a, w_ssm_glu_b,
                        w_conv_out, l)
        xf, h = _residual_proj(merged, w_out, l, xf, norm_ffn_g[l], BF16, 512, d_model, "out_proj")
        act = _ffn_up(h, w_ffn_gate, w_ffn_up, l)
        last = l == DEPTH - 1
        g_next = norm_final_g if last else norm_mix_g[l + 1]
        xf, h = _ffn_down(act, w_ffn_down, l, xf, g_next, F32 if last else BF16)
    return h.reshape(batch, SLABS, si, d_model).transpose(0, 2, 1, 3).reshape(batch, seq, d_model)
```

```python
import functools
import math

import jax
import jax.numpy as jnp
import numpy as np
from jax import lax
from jax.experimental import pallas as pl
from jax.experimental.pallas import tpu as pltpu

F32 = jnp.float32
BF16 = jnp.bfloat16

D_MODEL = 2048
DEPTH = 4
HEAD_DIM = 64
HEADS_PER_GROUP = 8
ATTN_GROUPS = ((128, 1), (512, 4), (2048, 16))
N_GROUPS = len(ATTN_GROUPS)
N_ATTN_HEADS = HEADS_PER_GROUP * N_GROUPS
ATTN_QW = N_ATTN_HEADS * HEAD_DIM
ATTN_OUT_W = HEADS_PER_GROUP * HEAD_DIM
ATTN_LAGS = ATTN_GROUPS[0][0] // ATTN_GROUPS[0][1]
NUM_BUCKETS = 32
MAX_DISTANCE = max(w for w, _ in ATTN_GROUPS)
ATTN_SCALE = HEAD_DIM ** -0.5
NEG_INF = -1e30
SSM_WIDTH = 1024
SSM_GROUP = 16
SSM_STATE = 64
CONV_WIDTH = 1024
CONV_K = 31
N_BRANCHES = 3
D_FF = -(-8 * D_MODEL // (3 * 256)) * 256
EPS = 1e-6
COL_U = 3 * ATTN_QW
COL_CONV = COL_U + SSM_WIDTH
COL_GATES = COL_CONV + 2 * CONV_WIDTH

LANES = 128
BF16_ROWS = 16
VMEM_LIMIT = 56 * 2 ** 20
VMEM_COMPILER_SCRATCH = 2 ** 20

SLABS = max(d for _, d in ATTN_GROUPS)
ATTN_TILES = ((8, 8, 64, 1), (32, 32, 128, 2), (128, 128, 256, 4))
HEAD_PAIR_W = 2 * HEAD_DIM
SSM_CHUNK = SLABS
SSM_BLOCKS = SSM_WIDTH // LANES
SSM_GROUPS_PER_BLOCK = LANES // SSM_GROUP
SSM_STATE_LANES = SSM_GROUPS_PER_BLOCK * SSM_STATE
FFN_ROW_CHUNK = 1024
CONV_RC = 64
CONV_PAD = 8
CONV_LOOP_ROWS = 16
CONV_LOOP_SLABS = 8


def _sigmoid(x):
    return 0.5 * jnp.tanh(0.5 * x) + 0.5


def _params(*sem):
    return pltpu.CompilerParams(dimension_semantics=sem, vmem_limit_bytes=VMEM_LIMIT)


def _rmsnorm_rows(x, g):
    return x * lax.rsqrt(jnp.mean(x * x, axis=-1, keepdims=True) + EPS) * g


def _rmsnorm_kernel(x_ref, g_ref, o_ref):
    o_ref[...] = _rmsnorm_rows(x_ref[...], g_ref[...]).astype(o_ref.dtype)


def _rmsnorm(x, g, out_dtype, tm=512):
    m, d = x.shape
    return pl.pallas_call(
        _rmsnorm_kernel,
        grid=(m // tm,),
        in_specs=[pl.BlockSpec((tm, d), lambda i: (i, 0)),
                  pl.BlockSpec((1, d), lambda i: (0, 0))],
        out_specs=pl.BlockSpec((tm, d), lambda i: (i, 0)),
        out_shape=jax.ShapeDtypeStruct((m, d), out_dtype),
        compiler_params=_params("parallel"),
        name="rmsnorm",
    )(x, g.reshape(1, d))


def _inproj_kernel(a_ref, w_ref, scale_ref, o_ref, wb_ref):
    @pl.when(pl.program_id(1) == 0)
    def _():
        wb_ref[...] = (w_ref[...] * scale_ref[...]).astype(BF16)

    o_ref[...] = jnp.dot(a_ref[...], wb_ref[...], preferred_element_type=F32).astype(o_ref.dtype)


def _inproj(h, w_all, layer, col_scale, tm=1024, tn=1536):
    m, k = h.shape
    n = w_all.shape[2]
    return pl.pallas_call(
        _inproj_kernel,
        grid=(n // tn, m // tm),
        in_specs=[pl.BlockSpec((tm, k), lambda j, i: (i, 0)),
                  pl.BlockSpec((None, k, tn), lambda j, i: (layer, 0, j)),
                  pl.BlockSpec((1, tn), lambda j, i: (0, j))],
        out_specs=pl.BlockSpec((tm, tn), lambda j, i: (i, j)),
        out_shape=jax.ShapeDtypeStruct((m, n), BF16),
        scratch_shapes=[pltpu.VMEM((k, tn), BF16)],
        compiler_params=_params("arbitrary", "arbitrary"),
        name="inproj",
    )(h, w_all, col_scale)


def _t5_bucket_causal(dist):
    max_exact = NUM_BUCKETS // 2
    d = np.maximum(dist, 1).astype(np.float32)
    large = max_exact + (np.log(d / max_exact) / math.log(MAX_DISTANCE / max_exact)
                         * (NUM_BUCKETS - max_exact)).astype(np.int32)
    large = np.minimum(large, NUM_BUCKETS - 1)
    return np.where(dist < max_exact, dist, large).astype(np.int32)


def _band_bias(rel_bias, g):
    _, d = ATTN_GROUPS[g]
    n_c = SLABS // d
    ni, n_prev = ATTN_TILES[g][:2]
    qc, qi = np.divmod(np.arange(n_c * ni), ni)
    kc, ki = np.divmod(np.arange(n_c * (n_prev + ni)), n_prev + ni)
    lag = n_c * (qi[:, None] - ki[None, :] + n_prev) + (qc[:, None] - kc[None, :])
    on_band = (lag >= 0) & (lag <= ATTN_LAGS)
    bucket = np.where(on_band, _t5_bucket_causal(np.clip(lag, 0, ATTN_LAGS) * d), -1).astype(np.int32)

    def band_kernel(rel_ref, bucket_ref, out_ref):
        bkt = bucket_ref[...]
        for h in range(HEADS_PER_GROUP):
            acc = jnp.full(bkt.shape, NEG_INF, F32)
            for b in range(NUM_BUCKETS):
                acc = jnp.where(bkt == b, rel_ref[b, g * HEADS_PER_GROUP + h], acc)
            out_ref[h] = acc

    return pl.pallas_call(
        band_kernel,
        in_specs=[pl.BlockSpec(memory_space=pltpu.SMEM), pl.BlockSpec(memory_space=pltpu.VMEM)],
        out_specs=pl.BlockSpec(memory_space=pltpu.VMEM),
        out_shape=jax.ShapeDtypeStruct((HEADS_PER_GROUP,) + bucket.shape, F32),
        name=f"band_bias_g{g}",
    )(rel_bias, jnp.asarray(bucket))


def _attn_kernel(*refs, ni, n_prev, chained, last):
    q_ref, kp_ref, kc_ref, vp_ref, vc_ref, bm_ref = refs[:6]
    po_ref, pl_ref = refs[6:8] if chained else (None, None)
    o_ref = refs[8 if chained else 6]
    lse_ref = None if last else refs[-1]
    tile = pl.program_id(2)
    n_c, n_cls, rows, _ = q_ref.shape
    hist = kp_ref.shape[2]
    win = n_prev + ni
    n_pair = HEADS_PER_GROUP // 2
    pair = lambda hp: slice(hp * HEAD_PAIR_W, (hp + 1) * HEAD_PAIR_W)
    stage = BF16 if ni % BF16_ROWS == 0 else F32
    per_store = max(1, BF16_ROWS // ni)
    upper = lax.broadcasted_iota(jnp.int32, (1, HEAD_PAIR_W), 1) >= HEAD_DIM
    nt_dims = (((1,), (1,)), ((), ()))

    def sub_tile(qs, kwin, vwin, t):
        r0, k0 = t * ni, hist - n_prev + t * ni
        q = jnp.concatenate([qs[c][r0:r0 + ni] for c in range(n_c)], axis=0).astype(BF16)
        kk = jnp.concatenate([kwin[c][k0:k0 + win] for c in range(n_c)], axis=0).astype(BF16)
        vv = jnp.concatenate([vwin[c][k0:k0 + win] for c in range(n_c)], axis=0).astype(BF16)
        logits = []
        for h in range(HEADS_PER_GROUP):
            q2 = q[:, pair(h // 2)]
            qm = jnp.where(upper if h % 2 else ~upper, q2, jnp.zeros_like(q2))
            s = lax.dot_general(qm, kk[:, pair(h // 2)], nt_dims, preferred_element_type=F32) + bm_ref[h]
            if t == 0:
                col = lax.broadcasted_iota(jnp.int32, s.shape, 1)
                s = jnp.where((col % win >= n_prev) | (tile > 0), s, NEG_INF)
            logits.append(s)
        mx = [jnp.max(s, axis=-1, keepdims=True) for s in logits]
        probs = [jnp.exp(s - m).astype(BF16) for s, m in zip(logits, mx)]
        out = []
        for hp in range(n_pair):
            v2 = vv[:, pair(hp)]
            one = jnp.ones_like(v2)
            o0 = jnp.dot(probs[2 * hp], jnp.where(upper, one, v2), preferred_element_type=F32)
            o1 = jnp.dot(probs[2 * hp + 1], jnp.where(upper, v2, one), preferred_element_type=F32)
            den = pltpu.roll(jnp.where(upper, o0, o1), HEAD_DIM, axis=1)
            out.append((jnp.where(upper, o1, o0) * (1.0 / den),
                        jnp.where(upper, mx[2 * hp + 1], mx[2 * hp]) + jnp.log(den)))
        return out

    for cls in range(n_cls):
        qs = [q_ref[c, cls].astype(stage) for c in range(n_c)]
        kwin = [jnp.concatenate([kp_ref[c, cls], kc_ref[c, cls]], axis=0).astype(stage) for c in range(n_c)]
        vwin = [jnp.concatenate([vp_ref[c, cls], vc_ref[c, cls]], axis=0).astype(stage) for c in range(n_c)]
        for t0 in range(0, rows // ni, per_store):
            done = [sub_tile(qs, kwin, vwin, t) for t in range(t0, t0 + per_store)]
            rs = slice(t0 * ni, (t0 + per_store) * ni)
            for hp in range(n_pair):
                for c in range(n_c):
                    cs = slice(c * ni, (c + 1) * ni)
                    o = jnp.concatenate([d[hp][0][cs] for d in done], axis=0)
                    lse = jnp.concatenate([d[hp][1][cs] for d in done], axis=0)
                    if chained:
                        o_p, lse_p = po_ref[c, cls, rs, pair(hp)].astype(F32), pl_ref[c, cls, rs, pair(hp)]
                        top = jnp.maximum(lse_p, lse)
                        e_p, e_c = jnp.exp(lse_p - top), jnp.exp(lse - top)
                        o = (e_p * o_p + e_c * o) / (e_p + e_c)
                        lse = top + jnp.log(e_p + e_c)
                    o_ref[c, cls, rs, pair(hp)] = o.astype(o_ref.dtype)
                    if not last:
                        lse_ref[c, cls, rs, pair(hp)] = lse


def _attention_group(z, bm, g, batch, seq, done=None):
    _, d = ATTN_GROUPS[g]
    n_c, n_r = SLABS // d, d
    ni, n_prev, rows, n_cls = ATTN_TILES[g]
    si = seq // SLABS
    rows = min(rows, si)
    z5 = z.reshape(batch, n_c, n_r, si, z.shape[1])
    kb = ATTN_QW // ATTN_OUT_W
    hist = max(n_prev, BF16_ROWS)
    prev_per_step = rows // hist

    def cur(off):
        return pl.BlockSpec((None, n_c, n_cls, rows, ATTN_OUT_W), lambda b, r, i: (b, 0, r, i, off + g))

    def prev(off):
        return pl.BlockSpec((None, n_c, n_cls, hist, ATTN_OUT_W),
                            lambda b, r, i: (b, 0, r, jnp.maximum(i * prev_per_step - 1, 0), off + g))

    out_spec = pl.BlockSpec((None, n_c, n_cls, rows, ATTN_OUT_W), lambda b, r, i: (b, 0, r, i, 0))
    last = g == N_GROUPS - 1
    shape5 = (batch, n_c, n_r, si, ATTN_OUT_W)
    prev_args = [] if done is None else [a.reshape(shape5) for a in done]
    res = pl.pallas_call(
        functools.partial(_attn_kernel, ni=ni, n_prev=n_prev, chained=done is not None, last=last),
        grid=(batch, n_r // n_cls, si // rows),
        in_specs=[cur(0), prev(kb), cur(kb), prev(2 * kb), cur(2 * kb),
                  pl.BlockSpec(bm.shape, lambda b, r, i: (0, 0, 0))] + [out_spec] * len(prev_args),
        out_specs=[out_spec] * (1 if last else 2),
        out_shape=[jax.ShapeDtypeStruct(shape5, BF16)] + ([] if last else [jax.ShapeDtypeStruct(shape5, F32)]),
        compiler_params=_params("parallel", "parallel", "arbitrary"),
        name=f"attn_g{g}",
    )(z5, z5, z5, z5, z5, bm, *prev_args)
    return tuple(a.reshape(batch * seq, ATTN_OUT_W) for a in res)


def _attention(z, bands, batch, seq):
    done = None
    for g in range(N_GROUPS):
        done = _attention_group(z, bands[g], g, batch, seq, done)
    return done[0]


def _ssm_ops_kernel(v0r_ref, v0i_ref, cmr_ref, cmi_ref, pwr_ref, pwi_ref, pwrt_ref, pwit_ref,
                    wd_ref, wx_ref, v_ref, z_ref):
    t_len, half = SSM_CHUNK, SSM_CHUNK // 2
    blk = lambda i: slice(i * LANES, (i + 1) * LANES)
    v0r, v0i = v0r_ref[...], v0i_ref[...]
    cmr, cmi = cmr_ref[...], cmi_ref[...]
    split = lambda a: (a.astype(BF16), (a - a.astype(BF16).astype(F32)).astype(BF16))
    cm = jnp.concatenate([cmr, -cmi], axis=0)
    cm_hi, cm_lo = split(cm)
    cm3 = jnp.concatenate([cm_hi, cm_lo, cm_hi], axis=0)
    kern = []
    for tau in range(t_len):
        pr, pi = pwr_ref[tau:tau + 1, :], pwi_ref[tau:tau + 1, :]
        vt = jnp.concatenate([v0r * pr - v0i * pi, v0r * pi + v0i * pr], axis=1)
        v_ref[blk(t_len - 1 - tau), :] = vt.astype(v_ref.dtype)
        vt_hi, vt_lo = split(vt)
        kern.append(jnp.dot(jnp.concatenate([vt_hi, vt_hi, vt_lo], axis=1), cm3, preferred_element_type=F32))
    zero = jnp.zeros((LANES, LANES), F32)
    for j in range(half):
        wd_ref[blk(j), :] = jnp.concatenate([kern[i - j] if i >= j else zero for i in range(half)],
                                            axis=1).astype(wd_ref.dtype)
        wx_ref[blk(j), :] = jnp.concatenate([kern[half + i - j] for i in range(half)], axis=1).astype(wx_ref.dtype)
    ns = SSM_STATE_LANES
    for i in range(t_len):
        pr, pi = pwrt_ref[:, i + 1:i + 2], pwit_ref[:, i + 1:i + 2]
        z_ref[0:ns, blk(i)] = (cmr * pr - cmi * pi).astype(z_ref.dtype)
        z_ref[ns:2 * ns, blk(i)] = (-(cmr * pi + cmi * pr)).astype(z_ref.dtype)


def _ssm_operators(lam_re, lam_im, log_dt, b_re, b_im, c_re, c_im, d_skip):
    t_len = SSM_CHUNK
    n_l, n_g, n_p = lam_re.shape
    nb, gl, ns = SSM_BLOCKS, SSM_GROUPS_PER_BLOCK, SSM_STATE_LANES
    dt = jnp.exp(log_dt)[..., None]
    lr, li = lam_re, lam_im
    mag = jnp.exp(lr * dt)
    ab_re, ab_im = mag * jnp.cos(li * dt), mag * jnp.sin(li * dt)
    nr, ni = ab_re - 1.0, ab_im
    den = lr * lr + li * li
    z_re, z_im = (nr * lr + ni * li) / den, (ni * lr - nr * li) / den
    bb_re = z_re[..., None] * b_re - z_im[..., None] * b_im
    bb_im = z_re[..., None] * b_im + z_im[..., None] * b_re
    tau = jnp.arange(t_len + 1, dtype=F32)[:, None, None, None]
    pmag = jnp.exp(lr * dt * tau)
    pw_re, pw_im = pmag * jnp.cos(li * dt * tau), pmag * jnp.sin(li * dt * tau)

    eye = jnp.eye(gl, dtype=F32)
    nlb = n_l * nb
    seed_in = lambda b: jnp.einsum('xgpc,gh->xgchp', b.reshape(nlb, gl, n_p, SSM_GROUP), eye).reshape(nlb, LANES, ns)
    seed_out = lambda c: jnp.einsum('xgop,gh->xgpho', c.reshape(nlb, gl, SSM_GROUP, n_p), eye).reshape(nlb, ns, LANES)
    pw_lanes = lambda p: p.reshape(t_len + 1, nlb, ns).transpose(1, 0, 2)
    pw_rows = lambda p: p.reshape(t_len + 1, nlb, ns).transpose(1, 2, 0)

    per_blk = lambda *shape: pl.BlockSpec((None,) + shape, lambda x: (x, 0, 0))
    cw = t_len * LANES
    wd, wx, v, z = pl.pallas_call(
        _ssm_ops_kernel,
        grid=(nlb,),
        in_specs=[per_blk(LANES, ns), per_blk(LANES, ns), per_blk(ns, LANES), per_blk(ns, LANES),
                  per_blk(t_len + 1, ns), per_blk(t_len + 1, ns), per_blk(ns, t_len + 1), per_blk(ns, t_len + 1)],
        out_specs=[per_blk(cw // 2, cw // 2), per_blk(cw // 2, cw // 2), per_blk(cw, 2 * ns), per_blk(2 * ns, cw)],
        out_shape=[jax.ShapeDtypeStruct((nlb, cw // 2, cw // 2), BF16),
                   jax.ShapeDtypeStruct((nlb, cw // 2, cw // 2), BF16),
                   jax.ShapeDtypeStruct((nlb, cw, 2 * ns), BF16),
                   jax.ShapeDtypeStruct((nlb, 2 * ns, cw), BF16)],
        compiler_params=_params("parallel"),
        name="ssm_operators",
    )(seed_in(bb_re), seed_in(bb_im), seed_out(c_re), seed_out(c_im),
      pw_lanes(pw_re), pw_lanes(pw_im), pw_rows(pw_re), pw_rows(pw_im))

    apow = jnp.concatenate([pw_re[t_len].reshape(nlb, 1, ns), pw_im[t_len].reshape(nlb, 1, ns)], axis=-1)
    dsk = jnp.tile(d_skip.reshape(nlb, 1, LANES), (1, 1, t_len))
    return wd, wx, v, z, apow, dsk


def _tap_source(r, back, n_slab):
    return (r - back) % n_slab, (back - r + n_slab - 1) // n_slab


def _ssm_conv_kernel(u_ref, val_ref, gate_ref, wd_ref, wx_ref, v_ref, z_ref, ap_ref, dsk_ref, cw_ref, cb_ref,
                     y_ref, c_ref, s_ref, buf_ref):
    n_batch, n_slab, n_rows, _ = u_ref.shape
    assert n_batch == 2
    nt = SSM_STATE_LANES // LANES
    half = n_slab // 2 * LANES
    lane = lambda t: slice(t * LANES, (t + 1) * LANES)
    dot = functools.partial(jnp.dot, preferred_element_type=F32)
    chunk_rows = lambda b: jnp.concatenate([u_ref[b, r] for r in range(n_slab)], axis=1)

    def fill_shifted(b):
        for r in range(n_slab):
            glu = val_ref[b, r].astype(F32) * jax.nn.sigmoid(gate_ref[b, r].astype(F32))
            for w in range(buf_ref.shape[0]):
                buf_ref[w, r, 0:CONV_PAD, :] = jnp.zeros((CONV_PAD, LANES), F32)
                buf_ref[w, r, w:w + n_rows, :] = glu

    def tap(acc, r, back, rows):
        src, wrap = _tap_source(r, back, n_slab)
        k = CONV_K - 1 - back
        return acc + cw_ref[k:k + 1, :] * buf_ref[wrap, src, rows, :]

    for b in range(n_batch):
        s = dot(chunk_rows(b), v_ref[...])
        for t in range(2 * nt):
            s_ref[t, b * n_rows:(b + 1) * n_rows, :] = s[:, lane(t)]

    ar = [jnp.broadcast_to(ap_ref[:, lane(t)], (n_batch, LANES)) for t in range(nt)]
    ai = [jnp.broadcast_to(ap_ref[:, lane(nt + t)], (n_batch, LANES)) for t in range(nt)]

    def recur(k, carry):
        idx = pl.ds(k, n_batch, stride=n_rows)
        new = []
        for t in range(nt):
            xr, xi = carry[t]
            re_ref, im_ref = s_ref.at[t], s_ref.at[nt + t]
            sr, si = re_ref[idx, :], im_ref[idx, :]
            re_ref[idx, :] = xr
            im_ref[idx, :] = xi
            new.append((ar[t] * xr - ai[t] * xi + sr, ar[t] * xi + ai[t] * xr + si))
        return tuple(new)

    fill_shifted(0)
    bias = cb_ref[...]

    def super_step(j, carry):
        for q in range(CONV_LOOP_ROWS):
            carry = recur(j * CONV_LOOP_ROWS + q, carry)
        rows = pl.ds(pl.multiple_of(j * CONV_LOOP_ROWS, CONV_LOOP_ROWS), CONV_LOOP_ROWS)
        for r_lo in range(0, n_slab, CONV_LOOP_SLABS):
            slabs = range(r_lo, r_lo + CONV_LOOP_SLABS)
            accs = {r: jnp.broadcast_to(bias, (CONV_LOOP_ROWS, LANES)) for r in slabs}
            for back in range(CONV_K):
                for r in slabs:
                    accs[r] = tap(accs[r], r, back, rows)
            for r in slabs:
                c_ref[0, r, rows, :] = accs[r].astype(c_ref.dtype)
        return carry

    zero = jnp.zeros((n_batch, LANES), F32)
    lax.fori_loop(0, n_rows // CONV_LOOP_ROWS, super_step, tuple((zero, zero) for _ in range(nt)))

    def ssm_items(b):
        env = {}

        def load():
            env["u"] = chunk_rows(b)
            rs = slice(b * n_rows, (b + 1) * n_rows)
            env["x"] = jnp.concatenate([s_ref[t, rs, :] for t in range(2 * nt)], axis=1).astype(BF16)
            env["lo"] = dot(env["u"][:, :half], wd_ref[...])

        def finish():
            u = env["u"]
            y = jnp.concatenate([env["lo"], env["hi"]], axis=1) + dsk_ref[...] * u.astype(F32)
            y = jax.nn.gelu(y).astype(y_ref.dtype)
            for r in range(n_slab):
                y_ref[b, r] = y[:, lane(r)]

        return [load,
                lambda: env.__setitem__("lo", env["lo"] + dot(env["x"], z_ref[:, :half])),
                lambda: env.__setitem__("hi", dot(env["u"][:, :half], wx_ref[...])),
                lambda: env.__setitem__("hi", env["hi"] + dot(env["u"][:, half:], wd_ref[...])),
                lambda: env.__setitem__("hi", env["hi"] + dot(env["x"], z_ref[:, half:])),
                finish]

    def conv_item(b, r, r0):
        def run():
            acc = jnp.broadcast_to(bias, (CONV_RC, LANES))
            for back in range(CONV_K):
                acc = tap(acc, r, back, slice(r0, r0 + CONV_RC))
            c_ref[b, r, r0:r0 + CONV_RC, :] = acc.astype(c_ref.dtype)
        return run

    fill_shifted(1)
    mxu_items = ssm_items(0) + ssm_items(1)
    vpu_items = [conv_item(1, r, r0) for r in range(n_slab) for r0 in range(0, n_rows, CONV_RC)]
    per_mxu = -(-len(vpu_items) // len(mxu_items))
    for n, item in enumerate(mxu_items):
        item()
        for vpu_item in vpu_items[n * per_mxu:(n + 1) * per_mxu]:
            vpu_item()


def _ssm_conv(z, ops, conv_w, conv_b, layer, batch, seq):
    wd, wx, v, zo, apow, dsk = ops
    si = seq // SLABS
    nbs = 2
    z4 = z.reshape(batch, SLABS, si, z.shape[1])
    n_lt = CONV_WIDTH // LANES
    assert n_lt == SSM_BLOCKS and batch % nbs == 0 and si % CONV_RC == 0
    tok = lambda col0: pl.BlockSpec((nbs, SLABS, si, LANES), lambda c, b: (b, 0, 0, col0 + c))
    op = lambda a: pl.BlockSpec((None,) + a.shape[1:], lambda c, b: (layer * SSM_BLOCKS + c, 0, 0))
    per_tile = lambda n: pl.BlockSpec((None, n, LANES), lambda c, b: (c, 0, 0))
    w_tiles = conv_w.reshape(CONV_K, n_lt, LANES).transpose(1, 0, 2)
    out = jax.ShapeDtypeStruct((batch, SLABS, si, SSM_WIDTH), BF16)
    y, c = pl.pallas_call(
        _ssm_conv_kernel,
        grid=(SSM_BLOCKS, batch // nbs),
        in_specs=[tok(COL_U // LANES), tok(COL_CONV // LANES), tok((COL_CONV + CONV_WIDTH) // LANES),
                  op(wd), op(wx), op(v), op(zo), op(apow), op(dsk), per_tile(CONV_K), per_tile(1)],
        out_specs=[tok(0), tok(0)],
        out_shape=[out, out],
        scratch_shapes=[pltpu.VMEM((2 * SSM_STATE_LANES // LANES, nbs * si, LANES), F32),
                        pltpu.VMEM((3, SLABS, si + CONV_PAD, LANES), F32)],
        compiler_params=_params("parallel", "arbitrary"),
        name="ssm_conv",
    )(z4, z4, z4, wd, wx, v, zo, apow, dsk, w_tiles, conv_b.reshape(n_lt, 1, LANES))
    return y.reshape(batch * seq, SSM_WIDTH), c.reshape(batch * seq, CONV_WIDTH)


def _merge_kernel(a_ref, y_ref, c_ref, lg_ref, lb_ref, g0_ref, g1_ref, g2_ref, wao_ref, wa_ref, wb_ref, wco_ref,
                  o_ref, cact_ref):
    dot = functools.partial(jnp.dot, preferred_element_type=F32)
    sig = lambda r: _sigmoid(r[...].astype(F32))

    def gated_sum(normalise):
        y = y_ref[...]
        o_a = dot(a_ref[...], wao_ref[...])
        o_s = dot(y, wa_ref[...]) * _sigmoid(dot(y, wb_ref[...]))
        if normalise:
            c = c_ref[...].astype(F32)
            mu = jnp.mean(c, axis=-1, keepdims=True)
            var = jnp.mean(jnp.square(c - mu), axis=-1, keepdims=True)
            ln = (c - mu) * lax.rsqrt(var + EPS) * lg_ref[...] + lb_ref[...]
            cact_ref[...] = (ln * _sigmoid(ln)).astype(cact_ref.dtype)
        o_c = dot(cact_ref[...], wco_ref[...])
        o_ref[...] = (sig(g0_ref) * o_a + sig(g1_ref) * o_s + sig(g2_ref) * o_c).astype(o_ref.dtype)

    first = pl.program_id(1) == 0
    pl.when(first)(lambda: gated_sum(True))
    pl.when(jnp.logical_not(first))(lambda: gated_sum(False))


def _merge(attn, y, conv, ln_g, ln_b, z, w_attn_out, w_glu_a, w_glu_b, w_conv_out, layer, tm=1024, tn=512):
    m = attn.shape[0]
    d = w_attn_out.shape[2]
    nt = d // tn
    col0 = COL_GATES // tn
    tok = lambda a: pl.BlockSpec((tm, a.shape[1]), lambda i, j: (i, 0))
    gate = lambda br: pl.BlockSpec((tm, tn), lambda i, j: (i, col0 + br * nt + j))
    wgt = lambda w: pl.BlockSpec((None, w.shape[1], tn), lambda i, j: (layer, 0, j))
    cw = conv.shape[1]
    vec = pl.BlockSpec((1, cw), lambda i, j: (0, 0))
    return pl.pallas_call(
        _merge_kernel,
        grid=(m // tm, nt),
        in_specs=[tok(attn), tok(y), tok(conv), vec, vec, gate(0), gate(1), gate(2),
                  wgt(w_attn_out), wgt(w_glu_a), wgt(w_glu_b), wgt(w_conv_out)],
        out_specs=pl.BlockSpec((tm, tn), lambda i, j: (i, j)),
        out_shape=jax.ShapeDtypeStruct((m, d), BF16),
        scratch_shapes=[pltpu.VMEM((tm, cw), BF16)],
        compiler_params=_params("parallel", "arbitrary"),
        name="merge",
    )(attn, y, conv, ln_g.reshape(1, cw), ln_b.reshape(1, cw), z, z, z, w_attn_out, w_glu_a, w_glu_b, w_conv_out)


def _residual_proj_kernel(a_ref, w_ref, x_ref, g_ref, xo_ref, ho_ref, acc_ref):
    k = pl.program_id(1)

    @pl.when(k == 0)
    def _():
        acc_ref[...] = x_ref[...]

    acc_ref[...] += jnp.dot(a_ref[...], w_ref[...], preferred_element_type=F32)

    @pl.when(k == pl.num_programs(1) - 1)
    def _():
        x = acc_ref[...]
        xo_ref[...] = x
        ho_ref[...] = _rmsnorm_rows(x, g_ref[...]).astype(ho_ref.dtype)


def _residual_proj(a, w_all, layer, x, g, h_dtype, tm, tk, name):
    m, kdim = a.shape
    d = w_all.shape[2]
    row = lambda i, k: (i, 0)
    return pl.pallas_call(
        _residual_proj_kernel,
        grid=(m // tm, kdim // tk),
        in_specs=[pl.BlockSpec((tm, tk), lambda i, k: (i, k)),
                  pl.BlockSpec((None, tk, d), lambda i, k: (layer, k, 0)),
                  pl.BlockSpec((tm, d), row),
                  pl.BlockSpec((1, d), lambda i, k: (0, 0))],
        out_specs=[pl.BlockSpec((tm, d), row), pl.BlockSpec((tm, d), row)],
        out_shape=[jax.ShapeDtypeStruct((m, d), F32), jax.ShapeDtypeStruct((m, d), h_dtype)],
        scratch_shapes=[pltpu.VMEM((tm, d), F32)],
        compiler_params=_params("parallel", "arbitrary"),
        name=name,
    )(a, w_all, x, g.reshape(1, d))


def _ffn_down_kernel(a_ref, w_ref, g_ref, x_hbm, xo_hbm, ho_hbm, acc_ref, ho_buf, sem_x, sem_out):
    i, k = pl.program_id(0), pl.program_id(1)
    n_i, n_k = pl.num_programs(0), pl.num_programs(1)
    slot = lax.rem(i, 2)
    x_copy = lambda tile, s: pltpu.make_async_copy(x_hbm.at[tile], acc_ref.at[s], sem_x.at[s])
    xo_copy = lambda tile, s: pltpu.make_async_copy(acc_ref.at[s], xo_hbm.at[tile], sem_out.at[0])
    ho_copy = lambda tile: pltpu.make_async_copy(ho_buf, ho_hbm.at[tile], sem_out.at[1])

    @pl.when((i == 0) & (k == 0))
    def _():
        x_copy(0, 0).start()

    @pl.when(k == 0)
    def _():
        x_copy(i, slot).wait()

    @pl.when((k == 1) & (i > 0))
    def _():
        xo_copy(i - 1, 1 - slot).wait()
        ho_copy(i - 1).wait()

    @pl.when((k == 1) & (i + 1 < n_i))
    def _():
        x_copy(i + 1, 1 - slot).start()

    acc_ref[slot] += jnp.dot(a_ref[...], w_ref[...], preferred_element_type=F32)

    @pl.when(k == n_k - 1)
    def _():
        ho_buf[...] = _rmsnorm_rows(acc_ref[slot], g_ref[...]).astype(ho_buf.dtype)
        xo_copy(i, slot).start()
        ho_copy(i).start()

    @pl.when((k == n_k - 1) & (i == n_i - 1))
    def _():
        xo_copy(i, slot).wait()
        ho_copy(i).wait()


def _ffn_down(a, w_all, layer, x, g, h_dtype, tm=1024):
    m, kdim = a.shape
    d = w_all.shape[2]

    def vmem_bytes(tk):
        return (2 * (tm * tk + tk * d) * a.dtype.itemsize + 2 * tm * d * 4 + tm * d * jnp.dtype(h_dtype).itemsize
                + VMEM_COMPILER_SCRATCH)

    n_k = next(n for n in range(2, kdim // LANES + 1)
               if kdim % (n * LANES) == 0 and vmem_bytes(kdim // n) <= VMEM_LIMIT)
    tk = kdim // n_k
    tiles = lambda dtype: jax.ShapeDtypeStruct((m // tm, tm, d), dtype)
    hbm = pl.BlockSpec(memory_space=pl.ANY)
    xo, ho = pl.pallas_call(
        _ffn_down_kernel,
        grid=(m // tm, n_k),
        in_specs=[pl.BlockSpec((tm, tk), lambda i, k: (i, k)),
                  pl.BlockSpec((None, tk, d), lambda i, k: (layer, k, 0)),
                  pl.BlockSpec((1, d), lambda i, k: (0, 0)),
                  hbm],
        out_specs=[hbm, hbm],
        out_shape=[tiles(F32), tiles(h_dtype)],
        scratch_shapes=[pltpu.VMEM((2, tm, d), F32), pltpu.VMEM((tm, d), h_dtype),
                        pltpu.SemaphoreType.DMA((2,)), pltpu.SemaphoreType.DMA((2,))],
        compiler_params=_params("arbitrary", "arbitrary"),
        name="ffn_down",
    )(a, w_all, g.reshape(1, d), x.reshape(m // tm, tm, d))
    return xo.reshape(m, d), ho.reshape(m, d)


def _ffn_up_kernel(h_ref, wg_ref, wu_ref, o_ref, wgb_ref, wub_ref):
    @pl.when(pl.program_id(1) == 0)
    def _():
        wgb_ref[...] = wg_ref[...].astype(BF16)
        wub_ref[...] = wu_ref[...].astype(BF16)

    for r0 in range(0, h_ref.shape[0], FFN_ROW_CHUNK):
        rows = slice(r0, r0 + FFN_ROW_CHUNK)
        h = h_ref[rows, :]
        gate = jnp.dot(h, wgb_ref[...], preferred_element_type=F32)
        up = jnp.dot(h, wub_ref[...], preferred_element_type=F32)
        o_ref[rows, :] = (jax.nn.silu(gate) * up).astype(o_ref.dtype)


def _ffn_up(h, wg_all, wu_all, layer, tm=2048, tn=512):
    m, k = h.shape
    n = wg_all.shape[2]
    wspec = pl.BlockSpec((None, k, tn), lambda j, i: (layer, 0, j))
    return pl.pallas_call(
        _ffn_up_kernel,
        grid=(n // tn, m // tm),
        in_specs=[pl.BlockSpec((tm, k), lambda j, i: (i, 0)), wspec, wspec],
        out_specs=pl.BlockSpec((tm, tn), lambda j, i: (i, j)),
        out_shape=jax.ShapeDtypeStruct((m, n), BF16),
        scratch_shapes=[pltpu.VMEM((k, tn), BF16), pltpu.VMEM((k, tn), BF16)],
        compiler_params=_params("arbitrary", "arbitrary"),
        name="ffn_up",
    )(h, wg_all, wu_all)


def kernel(x, w_in, rel_bias, lam_re, lam_im, log_dt, b_re, b_im, c_re, c_im, d_skip, w_ssm_glu_a, w_ssm_glu_b, conv_w, conv_b, conv_ln_g, conv_ln_b, w_conv_out, w_attn_out, w_out, norm_mix_g, norm_ffn_g, w_ffn_gate, w_ffn_up, w_ffn_down, norm_final_g):
    batch, seq, d_model = x.shape
    m = batch * seq
    si = seq // SLABS
    xf = x.reshape(batch, si, SLABS, d_model).transpose(0, 2, 1, 3).reshape(m, d_model)
    bands = [_band_bias(rel_bias, g) for g in range(N_GROUPS)]
    ssm_ops = _ssm_operators(lam_re, lam_im, log_dt, b_re, b_im, c_re, c_im, d_skip)
    col_scale = jnp.where(jnp.arange(w_in.shape[2]) < ATTN_QW, ATTN_SCALE, 1.0).astype(F32)[None, :]
    bf = lambda w: w.astype(BF16)
    w_attn_out, w_ssm_glu_a, w_ssm_glu_b, w_conv_out = bf(w_attn_out), bf(w_ssm_glu_a), bf(w_ssm_glu_b), bf(w_conv_out)
    w_out, w_ffn_down = bf(w_out), bf(w_ffn_down)

    h = _rmsnorm(xf, norm_mix_g[0], BF16)
    for l in range(DEPTH):
        z = _inproj(h, w_in, l, col_scale)
        attn = _attention(z, bands, batch, seq)
        y, conv = _ssm_conv(z, ssm_ops, conv_w[l], conv_b[l], l, batch, seq)
        merged = _merge(attn, y, conv, conv_ln_g[l], conv_ln_b[l], z, w_attn_out, w_ssm_glu_a, w_ssm_glu_b,
                        w_conv_out, l)
        xf, h = _residual_proj(merged, w_out, l, xf, norm_ffn_g[l], BF16, 512, d_model, "out_proj")
        act = _ffn_up(h, w_ffn_gate, w_ffn_up, l)
        last = l == DEPTH - 1
        g_next = norm_final_g if last else norm_mix_g[l + 1]
        xf, h = _ffn_down(act, w_ffn_down, l, xf, g_next, F32 if last else BF16)
    return h.reshape(batch, SLABS, si, d_model).transpose(0, 2, 1, 3).reshape(batch, seq, d_model)
```

```python
import functools
import math

import jax
import jax.numpy as jnp
import numpy as np
from jax import lax
from jax.experimental import pallas as pl
from jax.experimental.pallas import tpu as pltpu

F32 = jnp.float32
BF16 = jnp.bfloat16

D_MODEL = 2048
DEPTH = 4
HEAD_DIM = 64
HEADS_PER_GROUP = 8
ATTN_GROUPS = ((128, 1), (512, 4), (2048, 16))
N_GROUPS = len(ATTN_GROUPS)
N_ATTN_HEADS = HEADS_PER_GROUP * N_GROUPS
ATTN_QW = N_ATTN_HEADS * HEAD_DIM
ATTN_OUT_W = HEADS_PER_GROUP * HEAD_DIM
ATTN_LAGS = ATTN_GROUPS[0][0] // ATTN_GROUPS[0][1]
NUM_BUCKETS = 32
MAX_DISTANCE = max(w for w, _ in ATTN_GROUPS)
ATTN_SCALE = HEAD_DIM ** -0.5
NEG_INF = -1e30
SSM_WIDTH = 1024
SSM_GROUP = 16
SSM_STATE = 64
CONV_WIDTH = 1024
CONV_K = 31
D_FF = -(-8 * D_MODEL // (3 * 256)) * 256
EPS = 1e-6
COL_U = 3 * ATTN_QW
COL_CONV = COL_U + SSM_WIDTH
COL_GATES = COL_CONV + 2 * CONV_WIDTH

LANES = 128
BF16_ROWS = 16
VMEM_LIMIT = 56 * 2 ** 20
VMEM_COMPILER_SCRATCH = 2 ** 20

SLABS = max(d for _, d in ATTN_GROUPS)
ATTN_TILES = ((8, 8, 64, 1), (32, 32, 128, 2), (128, 128, 256, 4))
HEAD_PAIR_W = 2 * HEAD_DIM
SSM_CHUNK = SLABS
SSM_BLOCKS = SSM_WIDTH // LANES
SSM_GROUPS_PER_BLOCK = LANES // SSM_GROUP
SSM_STATE_LANES = SSM_GROUPS_PER_BLOCK * SSM_STATE
FFN_ROW_CHUNK = 1024
STATE_ROW_PAD = 8
CONV_RC = 64
CONV_PAD = 8
CONV_LOOP_ROWS = 16
CONV_LOOP_SLABS = 8


def _sigmoid(x):
    return 0.5 * jnp.tanh(0.5 * x) + 0.5


def _params(*sem):
    return pltpu.CompilerParams(dimension_semantics=sem, vmem_limit_bytes=VMEM_LIMIT)


def _rmsnorm_rows(x, g):
    return x * lax.rsqrt(jnp.mean(x * x, axis=-1, keepdims=True) + EPS) * g


def _rmsnorm_kernel(x_ref, g_ref, o_ref):
    o_ref[...] = _rmsnorm_rows(x_ref[...], g_ref[...]).astype(o_ref.dtype)


def _rmsnorm(x, g, out_dtype, tm=512):
    m, d = x.shape
    return pl.pallas_call(
        _rmsnorm_kernel,
        grid=(m // tm,),
        in_specs=[pl.BlockSpec((tm, d), lambda i: (i, 0)),
                  pl.BlockSpec((1, d), lambda i: (0, 0))],
        out_specs=pl.BlockSpec((tm, d), lambda i: (i, 0)),
        out_shape=jax.ShapeDtypeStruct((m, d), out_dtype),
        compiler_params=_params("parallel"),
        name="rmsnorm",
    )(x, g.reshape(1, d))


def _inproj_kernel(a_ref, w_ref, scale_ref, o_ref, wb_ref):
    @pl.when(pl.program_id(1) == 0)
    def _():
        wb_ref[...] = (w_ref[...] * scale_ref[...]).astype(BF16)

    o_ref[...] = jnp.dot(a_ref[...], wb_ref[...], preferred_element_type=F32).astype(o_ref.dtype)


def _inproj(h, w_all, layer, col_scale, tm=1024, tn=1536):
    m, k = h.shape
    n = w_all.shape[2]
    return pl.pallas_call(
        _inproj_kernel,
        grid=(n // tn, m // tm),
        in_specs=[pl.BlockSpec((tm, k), lambda j, i: (i, 0)),
                  pl.BlockSpec((None, k, tn), lambda j, i: (layer, 0, j)),
                  pl.BlockSpec((1, tn), lambda j, i: (0, j))],
        out_specs=pl.BlockSpec((tm, tn), lambda j, i: (i, j)),
        out_shape=jax.ShapeDtypeStruct((m, n), BF16),
        scratch_shapes=[pltpu.VMEM((k, tn), BF16)],
        compiler_params=_params("arbitrary", "arbitrary"),
        name="inproj",
    )(h, w_all, col_scale)


def _t5_bucket_causal(dist):
    max_exact = NUM_BUCKETS // 2
    d = np.maximum(dist, 1).astype(np.float32)
    large = max_exact + (np.log(d / max_exact) / math.log(MAX_DISTANCE / max_exact)
                         * (NUM_BUCKETS - max_exact)).astype(np.int32)
    large = np.minimum(large, NUM_BUCKETS - 1)
    return np.where(dist < max_exact, dist, large).astype(np.int32)


def _band_bias(rel_bias, g):
    _, d = ATTN_GROUPS[g]
    n_c = SLABS // d
    ni, n_prev = ATTN_TILES[g][:2]
    qc, qi = np.divmod(np.arange(n_c * ni), ni)
    kc, ki = np.divmod(np.arange(n_c * (n_prev + ni)), n_prev + ni)
    lag = n_c * (qi[:, None] - ki[None, :] + n_prev) + (qc[:, None] - kc[None, :])
    on_band = (lag >= 0) & (lag <= ATTN_LAGS)
    bucket = np.where(on_band, _t5_bucket_causal(np.clip(lag, 0, ATTN_LAGS) * d), -1).astype(np.int32)

    def band_kernel(rel_ref, bucket_ref, out_ref):
        bkt = bucket_ref[...]
        for h in range(HEADS_PER_GROUP):
            acc = jnp.full(bkt.shape, NEG_INF, F32)
            for b in range(NUM_BUCKETS):
                acc = jnp.where(bkt == b, rel_ref[b, g * HEADS_PER_GROUP + h], acc)
            out_ref[h] = acc

    return pl.pallas_call(
        band_kernel,
        in_specs=[pl.BlockSpec(memory_space=pltpu.SMEM), pl.BlockSpec(memory_space=pltpu.VMEM)],
        out_specs=pl.BlockSpec(memory_space=pltpu.VMEM),
        out_shape=jax.ShapeDtypeStruct((HEADS_PER_GROUP,) + bucket.shape, F32),
        name=f"band_bias_g{g}",
    )(rel_bias, jnp.asarray(bucket))


def _attn_kernel(*refs, ni, n_prev, chained, last):
    q_ref, kp_ref, kc_ref, vp_ref, vc_ref, bm_ref = refs[:6]
    po_ref, pl_ref = refs[6:8] if chained else (None, None)
    o_ref = refs[8 if chained else 6]
    lse_ref = None if last else refs[-1]
    tile = pl.program_id(2)
    n_c, n_cls, rows, _ = q_ref.shape
    hist = kp_ref.shape[2]
    win = n_prev + ni
    n_pair = HEADS_PER_GROUP // 2
    pair = lambda hp: slice(hp * HEAD_PAIR_W, (hp + 1) * HEAD_PAIR_W)
    stage = BF16 if ni % BF16_ROWS == 0 else F32
    per_store = max(1, BF16_ROWS // ni)
    upper = lax.broadcasted_iota(jnp.int32, (1, HEAD_PAIR_W), 1) >= HEAD_DIM
    nt_dims = (((1,), (1,)), ((), ()))

    def sub_tile(qs, kwin, vwin, t):
        r0, k0 = t * ni, hist - n_prev + t * ni
        q = jnp.concatenate([qs[c][r0:r0 + ni] for c in range(n_c)], axis=0).astype(BF16)
        kk = jnp.concatenate([kwin[c][k0:k0 + win] for c in range(n_c)], axis=0).astype(BF16)
        vv = jnp.concatenate([vwin[c][k0:k0 + win] for c in range(n_c)], axis=0).astype(BF16)
        logits = []
        for h in range(HEADS_PER_GROUP):
            q2 = q[:, pair(h // 2)]
            qm = jnp.where(upper if h % 2 else ~upper, q2, jnp.zeros_like(q2))
            s = lax.dot_general(qm, kk[:, pair(h // 2)], nt_dims, preferred_element_type=F32) + bm_ref[h]
            if t == 0:
                col = lax.broadcasted_iota(jnp.int32, s.shape, 1)
                s = jnp.where((col % win >= n_prev) | (tile > 0), s, NEG_INF)
            logits.append(s)
        mx = [jnp.max(s, axis=-1, keepdims=True) for s in logits]
        probs = [jnp.exp(s - m).astype(BF16) for s, m in zip(logits, mx)]
        out = []
        for hp in range(n_pair):
            v2 = vv[:, pair(hp)]
            one = jnp.ones_like(v2)
            o0 = jnp.dot(probs[2 * hp], jnp.where(upper, one, v2), preferred_element_type=F32)
            o1 = jnp.dot(probs[2 * hp + 1], jnp.where(upper, v2, one), preferred_element_type=F32)
            den = pltpu.roll(jnp.where(upper, o0, o1), HEAD_DIM, axis=1)
            out.append((jnp.where(upper, o1, o0) * (1.0 / den),
                        jnp.where(upper, mx[2 * hp + 1], mx[2 * hp]) + jnp.log(den)))
        return out

    for cls in range(n_cls):
        qs = [q_ref[c, cls].astype(stage) for c in range(n_c)]
        kwin = [jnp.concatenate([kp_ref[c, cls], kc_ref[c, cls]], axis=0).astype(stage) for c in range(n_c)]
        vwin = [jnp.concatenate([vp_ref[c, cls], vc_ref[c, cls]], axis=0).astype(stage) for c in range(n_c)]
        for t0 in range(0, rows // ni, per_store):
            done = [sub_tile(qs, kwin, vwin, t) for t in range(t0, t0 + per_store)]
            rs = slice(t0 * ni, (t0 + per_store) * ni)
            for hp in range(n_pair):
                for c in range(n_c):
                    cs = slice(c * ni, (c + 1) * ni)
                    o = jnp.concatenate([d[hp][0][cs] for d in done], axis=0)
                    lse = jnp.concatenate([d[hp][1][cs] for d in done], axis=0)
                    if chained:
                        o_p, lse_p = po_ref[c, cls, rs, pair(hp)].astype(F32), pl_ref[c, cls, rs, pair(hp)]
                        top = jnp.maximum(lse_p, lse)
                        e_p, e_c = jnp.exp(lse_p - top), jnp.exp(lse - top)
                        o = (e_p * o_p + e_c * o) / (e_p + e_c)
                        lse = top + jnp.log(e_p + e_c)
                    o_ref[c, cls, rs, pair(hp)] = o.astype(o_ref.dtype)
                    if not last:
                        lse_ref[c, cls, rs, pair(hp)] = lse


def _attention_group(z, bm, g, batch, seq, done=None):
    _, d = ATTN_GROUPS[g]
    n_c, n_r = SLABS // d, d
    ni, n_prev, rows, n_cls = ATTN_TILES[g]
    si = seq // SLABS
    rows = min(rows, si)
    z5 = z.reshape(batch, n_c, n_r, si, z.shape[1])
    kb = ATTN_QW // ATTN_OUT_W
    hist = max(n_prev, BF16_ROWS)
    prev_per_step = rows // hist

    def cur(off):
        return pl.BlockSpec((None, n_c, n_cls, rows, ATTN_OUT_W), lambda b, r, i: (b, 0, r, i, off + g))

    def prev(off):
        return pl.BlockSpec((None, n_c, n_cls, hist, ATTN_OUT_W),
                            lambda b, r, i: (b, 0, r, jnp.maximum(i * prev_per_step - 1, 0), off + g))

    out_spec = pl.BlockSpec((None, n_c, n_cls, rows, ATTN_OUT_W), lambda b, r, i: (b, 0, r, i, 0))
    last = g == N_GROUPS - 1
    shape5 = (batch, n_c, n_r, si, ATTN_OUT_W)
    prev_args = [] if done is None else [a.reshape(shape5) for a in done]
    res = pl.pallas_call(
        functools.partial(_attn_kernel, ni=ni, n_prev=n_prev, chained=done is not None, last=last),
        grid=(batch, n_r // n_cls, si // rows),
        in_specs=[cur(0), prev(kb), cur(kb), prev(2 * kb), cur(2 * kb),
                  pl.BlockSpec(bm.shape, lambda b, r, i: (0, 0, 0))] + [out_spec] * len(prev_args),
        out_specs=[out_spec] * (1 if last else 2),
        out_shape=[jax.ShapeDtypeStruct(shape5, BF16)] + ([] if last else [jax.ShapeDtypeStruct(shape5, F32)]),
        compiler_params=_params("parallel", "parallel", "arbitrary"),
        name=f"attn_g{g}",
    )(z5, z5, z5, z5, z5, bm, *prev_args)
    return tuple(a.reshape(batch * seq, ATTN_OUT_W) for a in res)


def _attention(z, bands, batch, seq):
    done = None
    for g in range(N_GROUPS):
        done = _attention_group(z, bands[g], g, batch, seq, done)
    return done[0]


def _ssm_ops_kernel(v0r_ref, v0i_ref, cmr_ref, cmi_ref, pwr_ref, pwi_ref, pwrt_ref, pwit_ref,
                    wd_ref, wx_ref, v_ref, z_ref):
    t_len, half = SSM_CHUNK, SSM_CHUNK // 2
    blk = lambda i: slice(i * LANES, (i + 1) * LANES)
    v0r, v0i = v0r_ref[...], v0i_ref[...]
    cmr, cmi = cmr_ref[...], cmi_ref[...]
    split = lambda a: (a.astype(BF16), (a - a.astype(BF16).astype(F32)).astype(BF16))
    cm = jnp.concatenate([cmr, -cmi], axis=0)
    cm_hi, cm_lo = split(cm)
    cm3 = jnp.concatenate([cm_hi, cm_lo, cm_hi], axis=0)
    kern = []
    for tau in range(t_len):
        pr, pi = pwr_ref[tau:tau + 1, :], pwi_ref[tau:tau + 1, :]
        vt = jnp.concatenate([v0r * pr - v0i * pi, v0r * pi + v0i * pr], axis=1)
        v_ref[blk(t_len - 1 - tau), :] = vt.astype(v_ref.dtype)
        vt_hi, vt_lo = split(vt)
        kern.append(jnp.dot(jnp.concatenate([vt_hi, vt_hi, vt_lo], axis=1), cm3, preferred_element_type=F32))
    zero = jnp.zeros((LANES, LANES), F32)
    for j in range(half):
        wd_ref[blk(j), :] = jnp.concatenate([kern[i - j] if i >= j else zero for i in range(half)],
                                            axis=1).astype(wd_ref.dtype)
        wx_ref[blk(j), :] = jnp.concatenate([kern[half + i - j] for i in range(half)], axis=1).astype(wx_ref.dtype)
    ns = SSM_STATE_LANES
    for i in range(t_len):
        pr, pi = pwrt_ref[:, i + 1:i + 2], pwit_ref[:, i + 1:i + 2]
        z_ref[0:ns, blk(i)] = (cmr * pr - cmi * pi).astype(z_ref.dtype)
        z_ref[ns:2 * ns, blk(i)] = (-(cmr * pi + cmi * pr)).astype(z_ref.dtype)


def _ssm_operators(lam_re, lam_im, log_dt, b_re, b_im, c_re, c_im, d_skip):
    t_len = SSM_CHUNK
    n_l, n_g, n_p = lam_re.shape
    nb, gl, ns = SSM_BLOCKS, SSM_GROUPS_PER_BLOCK, SSM_STATE_LANES
    dt = jnp.exp(log_dt)[..., None]
    lr, li = lam_re, lam_im
    mag = jnp.exp(lr * dt)
    ab_re, ab_im = mag * jnp.cos(li * dt), mag * jnp.sin(li * dt)
    nr, ni = ab_re - 1.0, ab_im
    den = lr * lr + li * li
    z_re, z_im = (nr * lr + ni * li) / den, (ni * lr - nr * li) / den
    bb_re = z_re[..., None] * b_re - z_im[..., None] * b_im
    bb_im = z_re[..., None] * b_im + z_im[..., None] * b_re
    tau = jnp.arange(t_len + 1, dtype=F32)[:, None, None, None]
    pmag = jnp.exp(lr * dt * tau)
    pw_re, pw_im = pmag * jnp.cos(li * dt * tau), pmag * jnp.sin(li * dt * tau)

    eye = jnp.eye(gl, dtype=F32)
    nlb = n_l * nb
    seed_in = lambda b: jnp.einsum('xgpc,gh->xgchp', b.reshape(nlb, gl, n_p, SSM_GROUP), eye).reshape(nlb, LANES, ns)
    seed_out = lambda c: jnp.einsum('xgop,gh->xgpho', c.reshape(nlb, gl, SSM_GROUP, n_p), eye).reshape(nlb, ns, LANES)
    pw_lanes = lambda p: p.reshape(t_len + 1, nlb, ns).transpose(1, 0, 2)
    pw_rows = lambda p: p.reshape(t_len + 1, nlb, ns).transpose(1, 2, 0)

    per_blk = lambda *shape: pl.BlockSpec((None,) + shape, lambda x: (x, 0, 0))
    cw = t_len * LANES
    wd, wx, v, z = pl.pallas_call(
        _ssm_ops_kernel,
        grid=(nlb,),
        in_specs=[per_blk(LANES, ns), per_blk(LANES, ns), per_blk(ns, LANES), per_blk(ns, LANES),
                  per_blk(t_len + 1, ns), per_blk(t_len + 1, ns), per_blk(ns, t_len + 1), per_blk(ns, t_len + 1)],
        out_specs=[per_blk(cw // 2, cw // 2), per_blk(cw // 2, cw // 2), per_blk(cw, 2 * ns), per_blk(2 * ns, cw)],
        out_shape=[jax.ShapeDtypeStruct((nlb, cw // 2, cw // 2), BF16),
                   jax.ShapeDtypeStruct((nlb, cw // 2, cw // 2), BF16),
                   jax.ShapeDtypeStruct((nlb, cw, 2 * ns), BF16),
                   jax.ShapeDtypeStruct((nlb, 2 * ns, cw), BF16)],
        compiler_params=_params("parallel"),
        name="ssm_operators",
    )(seed_in(bb_re), seed_in(bb_im), seed_out(c_re), seed_out(c_im),
      pw_lanes(pw_re), pw_lanes(pw_im), pw_rows(pw_re), pw_rows(pw_im))

    apow = jnp.concatenate([pw_re[t_len].reshape(nlb, 1, ns), pw_im[t_len].reshape(nlb, 1, ns)], axis=-1)
    dsk = jnp.tile(d_skip.reshape(nlb, 1, LANES), (1, 1, t_len))
    return wd, wx, v, z, apow, dsk


def _tap_source(r, back, n_slab):
    return (r - back) % n_slab, (back - r + n_slab - 1) // n_slab


def _ssm_conv_kernel(u_ref, val_ref, gate_ref, wd_ref, wx_ref, v_ref, z_ref, ap_ref, dsk_ref, cw_ref, cb_ref,
                     y_ref, c_ref, s_ref, buf_ref):
    n_batch, n_slab, n_rows, _ = u_ref.shape
    assert n_batch == 2
    b_stride = s_ref.shape[1] // n_batch
    state_rows = lambda b: slice(b * b_stride, b * b_stride + n_rows)
    nt = SSM_STATE_LANES // LANES
    half = n_slab // 2 * LANES
    lane = lambda t: slice(t * LANES, (t + 1) * LANES)
    dot = functools.partial(jnp.dot, preferred_element_type=F32)
    chunk_rows = lambda b: jnp.concatenate([u_ref[b, r] for r in range(n_slab)], axis=1)

    def fill_shifted(b):
        for r in range(n_slab):
            glu = val_ref[b, r].astype(F32) * jax.nn.sigmoid(gate_ref[b, r].astype(F32))
            for w in range(buf_ref.shape[0]):
                buf_ref[w, r, 0:CONV_PAD, :] = jnp.zeros((CONV_PAD, LANES), F32)
                buf_ref[w, r, w:w + n_rows, :] = glu

    def tap(acc, r, back, rows):
        src, wrap = _tap_source(r, back, n_slab)
        k = CONV_K - 1 - back
        return acc + cw_ref[k:k + 1, :] * buf_ref[wrap, src, rows, :]

    for b in range(n_batch):
        s = dot(chunk_rows(b), v_ref[...])
        for t in range(2 * nt):
            s_ref[t, state_rows(b), :] = s[:, lane(t)]

    ar = [jnp.broadcast_to(ap_ref[:, lane(t)], (n_batch, LANES)) for t in range(nt)]
    ai = [jnp.broadcast_to(ap_ref[:, lane(nt + t)], (n_batch, LANES)) for t in range(nt)]

    def recur(k, carry):
        idx = pl.ds(k, n_batch, stride=b_stride)
        new = []
        for t in range(nt):
            xr, xi = carry[t]
            re_ref, im_ref = s_ref.at[t], s_ref.at[nt + t]
            sr, si = re_ref[idx, :], im_ref[idx, :]
            re_ref[idx, :] = xr
            im_ref[idx, :] = xi
            new.append((ar[t] * xr - ai[t] * xi + sr, ar[t] * xi + ai[t] * xr + si))
        return tuple(new)

    fill_shifted(0)
    bias = cb_ref[...]

    def super_step(j, carry):
        for q in range(CONV_LOOP_ROWS):
            carry = recur(j * CONV_LOOP_ROWS + q, carry)
        rows = pl.ds(pl.multiple_of(j * CONV_LOOP_ROWS, CONV_LOOP_ROWS), CONV_LOOP_ROWS)
        for r_lo in range(0, n_slab, CONV_LOOP_SLABS):
            slabs = range(r_lo, r_lo + CONV_LOOP_SLABS)
            accs = {r: jnp.broadcast_to(bias, (CONV_LOOP_ROWS, LANES)) for r in slabs}
            for back in range(CONV_K):
                for r in slabs:
                    accs[r] = tap(accs[r], r, back, rows)
            for r in slabs:
                c_ref[0, r, rows, :] = accs[r].astype(c_ref.dtype)
        return carry

    zero = jnp.zeros((n_batch, LANES), F32)
    lax.fori_loop(0, n_rows // CONV_LOOP_ROWS, super_step, tuple((zero, zero) for _ in range(nt)))

    def ssm_items(b):
        env = {}

        def load():
            env["u"] = chunk_rows(b)
            env["x"] = jnp.concatenate([s_ref[t, state_rows(b), :] for t in range(2 * nt)], axis=1).astype(BF16)
            env["lo"] = dot(env["u"][:, :half], wd_ref[...])

        def finish():
            u = env["u"]
            y = jnp.concatenate([env["lo"], env["hi"]], axis=1) + dsk_ref[...] * u.astype(F32)
            y = jax.nn.gelu(y).astype(y_ref.dtype)
            for r in range(n_slab):
                y_ref[b, r] = y[:, lane(r)]

        return [load,
                lambda: env.__setitem__("lo", env["lo"] + dot(env["x"], z_ref[:, :half])),
                lambda: env.__setitem__("hi", dot(env["u"][:, :half], wx_ref[...])),
                lambda: env.__setitem__("hi", env["hi"] + dot(env["u"][:, half:], wd_ref[...])),
                lambda: env.__setitem__("hi", env["hi"] + dot(env["x"], z_ref[:, half:])),
                finish]

    def conv_item(b, r, r0):
        def run():
            acc = jnp.broadcast_to(bias, (CONV_RC, LANES))
            for back in range(CONV_K):
                acc = tap(acc, r, back, slice(r0, r0 + CONV_RC))
            c_ref[b, r, r0:r0 + CONV_RC, :] = acc.astype(c_ref.dtype)
        return run

    fill_shifted(1)
    mxu_items = ssm_items(0) + ssm_items(1)
    vpu_items = [conv_item(1, r, r0) for r in range(n_slab) for r0 in range(0, n_rows, CONV_RC)]
    per_mxu = -(-len(vpu_items) // len(mxu_items))
    for n, item in enumerate(mxu_items):
        item()
        for vpu_item in vpu_items[n * per_mxu:(n + 1) * per_mxu]:
            vpu_item()


def _ssm_conv(z, ops, conv_w, conv_b, layer, batch, seq):
    wd, wx, v, zo, apow, dsk = ops
    si = seq // SLABS
    nbs = 2
    z4 = z.reshape(batch, SLABS, si, z.shape[1])
    n_lt = CONV_WIDTH // LANES
    assert n_lt == SSM_BLOCKS and batch % nbs == 0 and si % CONV_RC == 0
    tok = lambda col0: pl.BlockSpec((nbs, SLABS, si, LANES), lambda c, b: (b, 0, 0, col0 + c))
    op = lambda a: pl.BlockSpec((None,) + a.shape[1:], lambda c, b: (layer * SSM_BLOCKS + c, 0, 0))
    per_tile = lambda n: pl.BlockSpec((None, n, LANES), lambda c, b: (c, 0, 0))
    w_tiles = conv_w.reshape(CONV_K, n_lt, LANES).transpose(1, 0, 2)
    out = jax.ShapeDtypeStruct((batch, SLABS, si, SSM_WIDTH), BF16)
    y, c = pl.pallas_call(
        _ssm_conv_kernel,
        grid=(SSM_BLOCKS, batch // nbs),
        in_specs=[tok(COL_U // LANES), tok(COL_CONV // LANES), tok((COL_CONV + CONV_WIDTH) // LANES),
                  op(wd), op(wx), op(v), op(zo), op(apow), op(dsk), per_tile(CONV_K), per_tile(1)],
        out_specs=[tok(0), tok(0)],
        out_shape=[out, out],
        scratch_shapes=[pltpu.VMEM((2 * SSM_STATE_LANES // LANES, nbs * (si + STATE_ROW_PAD), LANES), F32),
                        pltpu.VMEM((3, SLABS, si + CONV_PAD, LANES), F32)],
        compiler_params=_params("parallel", "arbitrary"),
        name="ssm_conv",
    )(z4, z4, z4, wd, wx, v, zo, apow, dsk, w_tiles, conv_b.reshape(n_lt, 1, LANES))
    return y.reshape(batch * seq, SSM_WIDTH), c.reshape(batch * seq, CONV_WIDTH)


def _merge_kernel(a_ref, y_ref, c_ref, lg_ref, lb_ref, g0_ref, g1_ref, g2_ref, wao_ref, wa_ref, wb_ref, wco_ref,
                  o_ref, cact_ref):
    dot = functools.partial(jnp.dot, preferred_element_type=F32)
    sig = lambda r: _sigmoid(r[...].astype(F32))

    def gated_sum(normalise):
        y = y_ref[...]
        o_a = dot(a_ref[...], wao_ref[...])
        o_s = dot(y, wa_ref[...]) * _sigmoid(dot(y, wb_ref[...]))
        if normalise:
            c = c_ref[...].astype(F32)
            mu = jnp.mean(c, axis=-1, keepdims=True)
            var = jnp.mean(jnp.square(c - mu), axis=-1, keepdims=True)
            ln = (c - mu) * lax.rsqrt(var + EPS) * lg_ref[...] + lb_ref[...]
            cact_ref[...] = (ln * _sigmoid(ln)).astype(cact_ref.dtype)
        o_c = dot(cact_ref[...], wco_ref[...])
        o_ref[...] = (sig(g0_ref) * o_a + sig(g1_ref) * o_s + sig(g2_ref) * o_c).astype(o_ref.dtype)

    first = pl.program_id(1) == 0
    pl.when(first)(lambda: gated_sum(True))
    pl.when(jnp.logical_not(first))(lambda: gated_sum(False))


def _merge(attn, y, conv, ln_g, ln_b, z, w_attn_out, w_glu_a, w_glu_b, w_conv_out, layer, tm=1024, tn=512):
    m = attn.shape[0]
    d = w_attn_out.shape[2]
    nt = d // tn
    col0 = COL_GATES // tn
    tok = lambda a: pl.BlockSpec((tm, a.shape[1]), lambda i, j: (i, 0))
    gate = lambda br: pl.BlockSpec((tm, tn), lambda i, j: (i, col0 + br * nt + j))
    wgt = lambda w: pl.BlockSpec((None, w.shape[1], tn), lambda i, j: (layer, 0, j))
    cw = conv.shape[1]
    vec = pl.BlockSpec((1, cw), lambda i, j: (0, 0))
    return pl.pallas_call(
        _merge_kernel,
        grid=(m // tm, nt),
        in_specs=[tok(attn), tok(y), tok(conv), vec, vec, gate(0), gate(1), gate(2),
                  wgt(w_attn_out), wgt(w_glu_a), wgt(w_glu_b), wgt(w_conv_out)],
        out_specs=pl.BlockSpec((tm, tn), lambda i, j: (i, j)),
        out_shape=jax.ShapeDtypeStruct((m, d), BF16),
        scratch_shapes=[pltpu.VMEM((tm, cw), BF16)],
        compiler_params=_params("parallel", "arbitrary"),
        name="merge",
    )(attn, y, conv, ln_g.reshape(1, cw), ln_b.reshape(1, cw), z, z, z, w_attn_out, w_glu_a, w_glu_b, w_conv_out)


def _residual_proj_kernel(a_ref, w_ref, x_ref, g_ref, xo_ref, ho_ref, acc_ref):
    k = pl.program_id(1)

    @pl.when(k == 0)
    def _():
        acc_ref[...] = x_ref[...]

    acc_ref[...] += jnp.dot(a_ref[...], w_ref[...], preferred_element_type=F32)

    @pl.when(k == pl.num_programs(1) - 1)
    def _():
        x = acc_ref[...]
        xo_ref[...] = x
        ho_ref[...] = _rmsnorm_rows(x, g_ref[...]).astype(ho_ref.dtype)


def _residual_proj(a, w_all, layer, x, g, h_dtype, tm, tk, name):
    m, kdim = a.shape
    d = w_all.shape[2]
    row = lambda i, k: (i, 0)
    return pl.pallas_call(
        _residual_proj_kernel,
        grid=(m // tm, kdim // tk),
        in_specs=[pl.BlockSpec((tm, tk), lambda i, k: (i, k)),
                  pl.BlockSpec((None, tk, d), lambda i, k: (layer, k, 0)),
                  pl.BlockSpec((tm, d), row),
                  pl.BlockSpec((1, d), lambda i, k: (0, 0))],
        out_specs=[pl.BlockSpec((tm, d), row), pl.BlockSpec((tm, d), row)],
        out_shape=[jax.ShapeDtypeStruct((m, d), F32), jax.ShapeDtypeStruct((m, d), h_dtype)],
        scratch_shapes=[pltpu.VMEM((tm, d), F32)],
        compiler_params=_params("parallel", "arbitrary"),
        name=name,
    )(a, w_all, x, g.reshape(1, d))


def _ffn_down_kernel(a_ref, w_ref, g_ref, x_hbm, xo_hbm, ho_hbm, acc_ref, ho_buf, sem_x, sem_out):
    i, k = pl.program_id(0), pl.program_id(1)
    n_i, n_k = pl.num_programs(0), pl.num_programs(1)
    slot = lax.rem(i, 2)
    x_copy = lambda tile, s: pltpu.make_async_copy(x_hbm.at[tile], acc_ref.at[s], sem_x.at[s])
    xo_copy = lambda tile, s: pltpu.make_async_copy(acc_ref.at[s], xo_hbm.at[tile], sem_out.at[0])
    ho_copy = lambda tile: pltpu.make_async_copy(ho_buf, ho_hbm.at[tile], sem_out.at[1])

    @pl.when((i == 0) & (k == 0))
    def _():
        x_copy(0, 0).start()

    @pl.when(k == 0)
    def _():
        x_copy(i, slot).wait()

    @pl.when((k == 1) & (i > 0))
    def _():
        xo_copy(i - 1, 1 - slot).wait()
        ho_copy(i - 1).wait()

    @pl.when((k == 1) & (i + 1 < n_i))
    def _():
        x_copy(i + 1, 1 - slot).start()

    acc_ref[slot] += jnp.dot(a_ref[...], w_ref[...], preferred_element_type=F32)

    @pl.when(k == n_k - 1)
    def _():
        ho_buf[...] = _rmsnorm_rows(acc_ref[slot], g_ref[...]).astype(ho_buf.dtype)
        xo_copy(i, slot).start()
        ho_copy(i).start()

    @pl.when((k == n_k - 1) & (i == n_i - 1))
    def _():
        xo_copy(i, slot).wait()
        ho_copy(i).wait()


def _ffn_down(a, w_all, layer, x, g, h_dtype, tm=1024):
    m, kdim = a.shape
    d = w_all.shape[2]

    def vmem_bytes(tk):
        return (2 * (tm * tk + tk * d) * a.dtype.itemsize + 2 * tm * d * 4 + tm * d * jnp.dtype(h_dtype).itemsize
                + VMEM_COMPILER_SCRATCH)

    n_k = next(n for n in range(2, kdim // LANES + 1)
               if kdim % (n * LANES) == 0 and vmem_bytes(kdim // n) <= VMEM_LIMIT)
    tk = kdim // n_k
    tiles = lambda dtype: jax.ShapeDtypeStruct((m // tm, tm, d), dtype)
    hbm = pl.BlockSpec(memory_space=pl.ANY)
    xo, ho = pl.pallas_call(
        _ffn_down_kernel,
        grid=(m // tm, n_k),
        in_specs=[pl.BlockSpec((tm, tk), lambda i, k: (i, k)),
                  pl.BlockSpec((None, tk, d), lambda i, k: (layer, k, 0)),
                  pl.BlockSpec((1, d), lambda i, k: (0, 0)),
                  hbm],
        out_specs=[hbm, hbm],
        out_shape=[tiles(F32), tiles(h_dtype)],
        scratch_shapes=[pltpu.VMEM((2, tm, d), F32), pltpu.VMEM((tm, d), h_dtype),
                        pltpu.SemaphoreType.DMA((2,)), pltpu.SemaphoreType.DMA((2,))],
        compiler_params=_params("arbitrary", "arbitrary"),
        name="ffn_down",
    )(a, w_all, g.reshape(1, d), x.reshape(m // tm, tm, d))
    return xo.reshape(m, d), ho.reshape(m, d)


def _ffn_up_kernel(h_ref, wg_ref, wu_ref, o_ref, wgb_ref, wub_ref):
    @pl.when(pl.program_id(1) == 0)
    def _():
        wgb_ref[...] = wg_ref[...].astype(BF16)
        wub_ref[...] = wu_ref[...].astype(BF16)

    for r0 in range(0, h_ref.shape[0], FFN_ROW_CHUNK):
        rows = slice(r0, r0 + FFN_ROW_CHUNK)
        h = h_ref[rows, :]
        gate = jnp.dot(h, wgb_ref[...], preferred_element_type=F32)
        up = jnp.dot(h, wub_ref[...], preferred_element_type=F32)
        o_ref[rows, :] = (jax.nn.silu(gate) * up).astype(o_ref.dtype)


def _ffn_up(h, wg_all, wu_all, layer, tm=2048, tn=512):
    m, k = h.shape
    n = wg_all.shape[2]
    wspec = pl.BlockSpec((None, k, tn), lambda j, i: (layer, 0, j))
    return pl.pallas_call(
        _ffn_up_kernel,
        grid=(n // tn, m // tm),
        in_specs=[pl.BlockSpec((tm, k), lambda j, i: (i, 0)), wspec, wspec],
        out_specs=pl.BlockSpec((tm, tn), lambda j, i: (i, j)),
        out_shape=jax.ShapeDtypeStruct((m, n), BF16),
        scratch_shapes=[pltpu.VMEM((k, tn), BF16), pltpu.VMEM((k, tn), BF16)],
        compiler_params=_params("arbitrary", "arbitrary"),
        name="ffn_up",
    )(h, wg_all, wu_all)


def kernel(x, w_in, rel_bias, lam_re, lam_im, log_dt, b_re, b_im, c_re, c_im, d_skip, w_ssm_glu_a, w_ssm_glu_b, conv_w, conv_b, conv_ln_g, conv_ln_b, w_conv_out, w_attn_out, w_out, norm_mix_g, norm_ffn_g, w_ffn_gate, w_ffn_up, w_ffn_down, norm_final_g):
    batch, seq, d_model = x.shape
    m = batch * seq
    si = seq // SLABS
    xf = x.reshape(batch, si, SLABS, d_model).transpose(0, 2, 1, 3).reshape(m, d_model)
    bands = [_band_bias(rel_bias, g) for g in range(N_GROUPS)]
    ssm_ops = _ssm_operators(lam_re, lam_im, log_dt, b_re, b_im, c_re, c_im, d_skip)
    col_scale = jnp.where(jnp.arange(w_in.shape[2]) < ATTN_QW, ATTN_SCALE, 1.0).astype(F32)[None, :]
    bf = lambda w: w.astype(BF16)
    w_attn_out, w_ssm_glu_a, w_ssm_glu_b, w_conv_out = bf(w_attn_out), bf(w_ssm_glu_a), bf(w_ssm_glu_b), bf(w_conv_out)
    w_out, w_ffn_down = bf(w_out), bf(w_ffn_down)

    h = _rmsnorm(xf, norm_mix_g[0], BF16)
    for l in range(DEPTH):
        z = _inproj(h, w_in, l, col_scale)
        attn = _attention(z, bands, batch, seq)
        y, conv = _ssm_conv(z, ssm_ops, conv_w[l], conv_b[l], l, batch, seq)
        merged = _merge(attn, y, conv, conv_ln_g[l], conv_ln_b[l], z, w_attn_out, w_ssm_glu_a, w_ssm_glu_b,
                        w_conv_out, l)
        xf, h = _residual_proj(merged, w_out, l, xf, norm_ffn_g[l], BF16, 512, d_model, "out_proj")
        act = _ffn_up(h, w_ffn_gate, w_ffn_up, l)
        last = l == DEPTH - 1
        g_next = norm_final_g if last else norm_mix_g[l + 1]
        xf, h = _ffn_down(act, w_ffn_down, l, xf, g_next, F32 if last else BF16)
    return h.reshape(batch, SLABS, si, d_model).transpose(0, 2, 1, 3).reshape(batch, seq, d_model)
```

```python
import functools
import math

import jax
import jax.numpy as jnp
import numpy as np
from jax import lax
from jax.experimental import pallas as pl
from jax.experimental.pallas import tpu as pltpu

F32 = jnp.float32
BF16 = jnp.bfloat16

D_MODEL = 2048
DEPTH = 4
HEAD_DIM = 64
HEADS_PER_GROUP = 8
ATTN_GROUPS = ((128, 1), (512, 4), (2048, 16))
N_GROUPS = len(ATTN_GROUPS)
N_ATTN_HEADS = HEADS_PER_GROUP * N_GROUPS
ATTN_QW = N_ATTN_HEADS * HEAD_DIM
ATTN_OUT_W = HEADS_PER_GROUP * HEAD_DIM
ATTN_LAGS = ATTN_GROUPS[0][0] // ATTN_GROUPS[0][1]
NUM_BUCKETS = 32
MAX_DISTANCE = max(w for w, _ in ATTN_GROUPS)
ATTN_SCALE = HEAD_DIM ** -0.5
NEG_INF = -1e30
SSM_WIDTH = 1024
SSM_GROUP = 16
SSM_STATE = 64
CONV_WIDTH = 1024
CONV_K = 31
D_FF = -(-8 * D_MODEL // (3 * 256)) * 256
EPS = 1e-6
COL_U = 3 * ATTN_QW
COL_CONV = COL_U + SSM_WIDTH
COL_GATES = COL_CONV + 2 * CONV_WIDTH

LANES = 128
BF16_ROWS = 16
VMEM_LIMIT = 56 * 2 ** 20
VMEM_COMPILER_SCRATCH = 2 ** 20

SLABS = max(d for _, d in ATTN_GROUPS)
ATTN_TILES = ((8, 8, 128, 1), (32, 32, 128, 4), (128, 128, 256, 8))
HEAD_PAIR_W = 2 * HEAD_DIM
SSM_CHUNK = SLABS
SSM_BLOCKS = SSM_WIDTH // LANES
SSM_GROUPS_PER_BLOCK = LANES // SSM_GROUP
SSM_STATE_LANES = SSM_GROUPS_PER_BLOCK * SSM_STATE
FFN_ROW_CHUNK = 1024
STATE_ROW_PAD = 8
CONV_RC = 64
CONV_PAD = 8
CONV_LOOP_ROWS = 16
CONV_LOOP_SLABS = 8


def _sigmoid(x):
    return 0.5 * jnp.tanh(0.5 * x) + 0.5


def _params(*sem):
    return pltpu.CompilerParams(dimension_semantics=sem, vmem_limit_bytes=VMEM_LIMIT)


def _rmsnorm_rows(x, g):
    return x * lax.rsqrt(jnp.mean(x * x, axis=-1, keepdims=True) + EPS) * g


def _rmsnorm_kernel(x_ref, g_ref, o_ref):
    o_ref[...] = _rmsnorm_rows(x_ref[...], g_ref[...]).astype(o_ref.dtype)


def _rmsnorm(x, g, out_dtype, tm=512):
    m, d = x.shape
    return pl.pallas_call(
        _rmsnorm_kernel,
        grid=(m // tm,),
        in_specs=[pl.BlockSpec((tm, d), lambda i: (i, 0)),
                  pl.BlockSpec((1, d), lambda i: (0, 0))],
        out_specs=pl.BlockSpec((tm, d), lambda i: (i, 0)),
        out_shape=jax.ShapeDtypeStruct((m, d), out_dtype),
        compiler_params=_params("parallel"),
        name="rmsnorm",
    )(x, g.reshape(1, d))


def _inproj_kernel(a_ref, w_ref, scale_ref, o_ref, wb_ref):
    @pl.when(pl.program_id(1) == 0)
    def _():
        wb_ref[...] = (w_ref[...] * scale_ref[...]).astype(BF16)

    o_ref[...] = jnp.dot(a_ref[...], wb_ref[...], preferred_element_type=F32).astype(o_ref.dtype)


def _inproj(h, w_all, layer, col_scale, tm=1024, tn=1536):
    m, k = h.shape
    n = w_all.shape[2]
    return pl.pallas_call(
        _inproj_kernel,
        grid=(n // tn, m // tm),
        in_specs=[pl.BlockSpec((tm, k), lambda j, i: (i, 0)),
                  pl.BlockSpec((None, k, tn), lambda j, i: (layer, 0, j)),
                  pl.BlockSpec((1, tn), lambda j, i: (0, j))],
        out_specs=pl.BlockSpec((tm, tn), lambda j, i: (i, j)),
        out_shape=jax.ShapeDtypeStruct((m, n), BF16),
        scratch_shapes=[pltpu.VMEM((k, tn), BF16)],
        compiler_params=_params("arbitrary", "arbitrary"),
        name="inproj",
    )(h, w_all, col_scale)


def _t5_bucket_causal(dist):
    max_exact = NUM_BUCKETS // 2
    d = np.maximum(dist, 1).astype(np.float32)
    large = max_exact + (np.log(d / max_exact) / math.log(MAX_DISTANCE / max_exact)
                         * (NUM_BUCKETS - max_exact)).astype(np.int32)
    large = np.minimum(large, NUM_BUCKETS - 1)
    return np.where(dist < max_exact, dist, large).astype(np.int32)


def _band_bias(rel_bias, g):
    _, d = ATTN_GROUPS[g]
    n_c = SLABS // d
    ni, n_prev = ATTN_TILES[g][:2]
    qc, qi = np.divmod(np.arange(n_c * ni), ni)
    kc, ki = np.divmod(np.arange(n_c * (n_prev + ni)), n_prev + ni)
    lag = n_c * (qi[:, None] - ki[None, :] + n_prev) + (qc[:, None] - kc[None, :])
    on_band = (lag >= 0) & (lag <= ATTN_LAGS)
    bucket = np.where(on_band, _t5_bucket_causal(np.clip(lag, 0, ATTN_LAGS) * d), -1).astype(np.int32)

    def band_kernel(rel_ref, bucket_ref, out_ref):
        bkt = bucket_ref[...]
        for h in range(HEADS_PER_GROUP):
            acc = jnp.full(bkt.shape, NEG_INF, F32)
            for b in range(NUM_BUCKETS):
                acc = jnp.where(bkt == b, rel_ref[b, g * HEADS_PER_GROUP + h], acc)
            out_ref[h] = acc

    return pl.pallas_call(
        band_kernel,
        in_specs=[pl.BlockSpec(memory_space=pltpu.SMEM), pl.BlockSpec(memory_space=pltpu.VMEM)],
        out_specs=pl.BlockSpec(memory_space=pltpu.VMEM),
        out_shape=jax.ShapeDtypeStruct((HEADS_PER_GROUP,) + bucket.shape, F32),
        name=f"band_bias_g{g}",
    )(rel_bias, jnp.asarray(bucket))


def _attn_kernel(*refs, ni, n_prev, chained, last):
    q_ref, kp_ref, kc_ref, vp_ref, vc_ref, bm_ref = refs[:6]
    po_ref, pl_ref = refs[6:8] if chained else (None, None)
    o_ref = refs[8 if chained else 6]
    lse_ref = None if last else refs[-1]
    tile = pl.program_id(2)
    n_c, n_cls, rows, _ = q_ref.shape
    hist = kp_ref.shape[2]
    win = n_prev + ni
    n_pair = HEADS_PER_GROUP // 2
    pair = lambda hp: slice(hp * HEAD_PAIR_W, (hp + 1) * HEAD_PAIR_W)
    stage = BF16 if ni % BF16_ROWS == 0 else F32
    per_store = max(1, BF16_ROWS // ni)
    upper = lax.broadcasted_iota(jnp.int32, (1, HEAD_PAIR_W), 1) >= HEAD_DIM
    nt_dims = (((1,), (1,)), ((), ()))

    def sub_tile(qs, kwin, vwin, t):
        r0, k0 = t * ni, hist - n_prev + t * ni
        q = jnp.concatenate([qs[c][r0:r0 + ni] for c in range(n_c)], axis=0).astype(BF16)
        kk = jnp.concatenate([kwin[c][k0:k0 + win] for c in range(n_c)], axis=0).astype(BF16)
        vv = jnp.concatenate([vwin[c][k0:k0 + win] for c in range(n_c)], axis=0).astype(BF16)
        logits = []
        for h in range(HEADS_PER_GROUP):
            q2 = q[:, pair(h // 2)]
            qm = jnp.where(upper if h % 2 else ~upper, q2, jnp.zeros_like(q2))
            s = lax.dot_general(qm, kk[:, pair(h // 2)], nt_dims, preferred_element_type=F32) + bm_ref[h]
            if t == 0:
                col = lax.broadcasted_iota(jnp.int32, s.shape, 1)
                s = jnp.where((col % win >= n_prev) | (tile > 0), s, NEG_INF)
            logits.append(s)
        mx = [jnp.max(s, axis=-1, keepdims=True) for s in logits]
        probs = [jnp.exp(s - m).astype(BF16) for s, m in zip(logits, mx)]
        out = []
        for hp in range(n_pair):
            v2 = vv[:, pair(hp)]
            one = jnp.ones_like(v2)
            o0 = jnp.dot(probs[2 * hp], jnp.where(upper, one, v2), preferred_element_type=F32)
            o1 = jnp.dot(probs[2 * hp + 1], jnp.where(upper, v2, one), preferred_element_type=F32)
            den = pltpu.roll(jnp.where(upper, o0, o1), HEAD_DIM, axis=1)
            out.append((jnp.where(upper, o1, o0) * (1.0 / den),
                        jnp.where(upper, mx[2 * hp + 1], mx[2 * hp]) + jnp.log(den)))
        return out

    for cls in range(n_cls):
        qs = [q_ref[c, cls].astype(stage) for c in range(n_c)]
        kwin = [jnp.concatenate([kp_ref[c, cls], kc_ref[c, cls]], axis=0).astype(stage) for c in range(n_c)]
        vwin = [jnp.concatenate([vp_ref[c, cls], vc_ref[c, cls]], axis=0).astype(stage) for c in range(n_c)]
        for t0 in range(0, rows // ni, per_store):
            done = [sub_tile(qs, kwin, vwin, t) for t in range(t0, t0 + per_store)]
            rs = slice(t0 * ni, (t0 + per_store) * ni)
            for hp in range(n_pair):
                for c in range(n_c):
                    cs = slice(c * ni, (c + 1) * ni)
                    o = jnp.concatenate([d[hp][0][cs] for d in done], axis=0)
                    lse = jnp.concatenate([d[hp][1][cs] for d in done], axis=0)
                    if chained:
                        o_p, lse_p = po_ref[c, cls, rs, pair(hp)].astype(F32), pl_ref[c, cls, rs, pair(hp)]
                        top = jnp.maximum(lse_p, lse)
                        e_p, e_c = jnp.exp(lse_p - top), jnp.exp(lse - top)
                        o = (e_p * o_p + e_c * o) / (e_p + e_c)
                        lse = top + jnp.log(e_p + e_c)
                    o_ref[c, cls, rs, pair(hp)] = o.astype(o_ref.dtype)
                    if not last:
                        lse_ref[c, cls, rs, pair(hp)] = lse


def _attention_group(z, bm, g, batch, seq, done=None):
    _, d = ATTN_GROUPS[g]
    n_c, n_r = SLABS // d, d
    ni, n_prev, rows, n_cls = ATTN_TILES[g]
    si = seq // SLABS
    rows = min(rows, si)
    z5 = z.reshape(batch, n_c, n_r, si, z.shape[1])
    kb = ATTN_QW // ATTN_OUT_W
    hist = max(n_prev, BF16_ROWS)
    prev_per_step = rows // hist

    def cur(off):
        return pl.BlockSpec((None, n_c, n_cls, rows, ATTN_OUT_W), lambda b, r, i: (b, 0, r, i, off + g))

    def prev(off):
        return pl.BlockSpec((None, n_c, n_cls, hist, ATTN_OUT_W),
                            lambda b, r, i: (b, 0, r, jnp.maximum(i * prev_per_step - 1, 0), off + g))

    out_spec = pl.BlockSpec((None, n_c, n_cls, rows, ATTN_OUT_W), lambda b, r, i: (b, 0, r, i, 0))
    last = g == N_GROUPS - 1
    shape5 = (batch, n_c, n_r, si, ATTN_OUT_W)
    prev_args = [] if done is None else [a.reshape(shape5) for a in done]
    res = pl.pallas_call(
        functools.partial(_attn_kernel, ni=ni, n_prev=n_prev, chained=done is not None, last=last),
        grid=(batch, n_r // n_cls, si // rows),
        in_specs=[cur(0), prev(kb), cur(kb), prev(2 * kb), cur(2 * kb),
                  pl.BlockSpec(bm.shape, lambda b, r, i: (0, 0, 0))] + [out_spec] * len(prev_args),
        out_specs=[out_spec] * (1 if last else 2),
        out_shape=[jax.ShapeDtypeStruct(shape5, BF16)] + ([] if last else [jax.ShapeDtypeStruct(shape5, F32)]),
        compiler_params=_params("parallel", "parallel", "arbitrary"),
        name=f"attn_g{g}",
    )(z5, z5, z5, z5, z5, bm, *prev_args)
    return tuple(a.reshape(batch * seq, ATTN_OUT_W) for a in res)


def _attention(z, bands, batch, seq):
    done = None
    for g in range(N_GROUPS):
        done = _attention_group(z, bands[g], g, batch, seq, done)
    return done[0]


def _ssm_ops_kernel(v0r_ref, v0i_ref, cmr_ref, cmi_ref, pwr_ref, pwi_ref, pwrt_ref, pwit_ref,
                    wd_ref, wx_ref, v_ref, z_ref):
    t_len, half = SSM_CHUNK, SSM_CHUNK // 2
    blk = lambda i: slice(i * LANES, (i + 1) * LANES)
    v0r, v0i = v0r_ref[...], v0i_ref[...]
    cmr, cmi = cmr_ref[...], cmi_ref[...]
    split = lambda a: (a.astype(BF16), (a - a.astype(BF16).astype(F32)).astype(BF16))
    cm = jnp.concatenate([cmr, -cmi], axis=0)
    cm_hi, cm_lo = split(cm)
    cm3 = jnp.concatenate([cm_hi, cm_lo, cm_hi], axis=0)
    kern = []
    for tau in range(t_len):
        pr, pi = pwr_ref[tau:tau + 1, :], pwi_ref[tau:tau + 1, :]
        vt = jnp.concatenate([v0r * pr - v0i * pi, v0r * pi + v0i * pr], axis=1)
        v_ref[blk(t_len - 1 - tau), :] = vt.astype(v_ref.dtype)
        vt_hi, vt_lo = split(vt)
        kern.append(jnp.dot(jnp.concatenate([vt_hi, vt_hi, vt_lo], axis=1), cm3, preferred_element_type=F32))
    zero = jnp.zeros((LANES, LANES), F32)
    for j in range(half):
        wd_ref[blk(j), :] = jnp.concatenate([kern[i - j] if i >= j else zero for i in range(half)],
                                            axis=1).astype(wd_ref.dtype)
        wx_ref[blk(j), :] = jnp.concatenate([kern[half + i - j] for i in range(half)], axis=1).astype(wx_ref.dtype)
    ns = SSM_STATE_LANES
    for i in range(t_len):
        pr, pi = pwrt_ref[:, i + 1:i + 2], pwit_ref[:, i + 1:i + 2]
        z_ref[0:ns, blk(i)] = (cmr * pr - cmi * pi).astype(z_ref.dtype)
        z_ref[ns:2 * ns, blk(i)] = (-(cmr * pi + cmi * pr)).astype(z_ref.dtype)


def _ssm_operators(lam_re, lam_im, log_dt, b_re, b_im, c_re, c_im, d_skip):
    t_len = SSM_CHUNK
    n_l, n_g, n_p = lam_re.shape
    nb, gl, ns = SSM_BLOCKS, SSM_GROUPS_PER_BLOCK, SSM_STATE_LANES
    dt = jnp.exp(log_dt)[..., None]
    lr, li = lam_re, lam_im
    mag = jnp.exp(lr * dt)
    ab_re, ab_im = mag * jnp.cos(li * dt), mag * jnp.sin(li * dt)
    nr, ni = ab_re - 1.0, ab_im
    den = lr * lr + li * li
    z_re, z_im = (nr * lr + ni * li) / den, (ni * lr - nr * li) / den
    bb_re = z_re[..., None] * b_re - z_im[..., None] * b_im
    bb_im = z_re[..., None] * b_im + z_im[..., None] * b_re
    tau = jnp.arange(t_len + 1, dtype=F32)[:, None, None, None]
    pmag = jnp.exp(lr * dt * tau)
    pw_re, pw_im = pmag * jnp.cos(li * dt * tau), pmag * jnp.sin(li * dt * tau)

    eye = jnp.eye(gl, dtype=F32)
    nlb = n_l * nb
    seed_in = lambda b: jnp.einsum('xgpc,gh->xgchp', b.reshape(nlb, gl, n_p, SSM_GROUP), eye).reshape(nlb, LANES, ns)
    seed_out = lambda c: jnp.einsum('xgop,gh->xgpho', c.reshape(nlb, gl, SSM_GROUP, n_p), eye).reshape(nlb, ns, LANES)
    pw_lanes = lambda p: p.reshape(t_len + 1, nlb, ns).transpose(1, 0, 2)
    pw_rows = lambda p: p.reshape(t_len + 1, nlb, ns).transpose(1, 2, 0)

    per_blk = lambda *shape: pl.BlockSpec((None,) + shape, lambda x: (x, 0, 0))
    cw = t_len * LANES
    wd, wx, v, z = pl.pallas_call(
        _ssm_ops_kernel,
        grid=(nlb,),
        in_specs=[per_blk(LANES, ns), per_blk(LANES, ns), per_blk(ns, LANES), per_blk(ns, LANES),
                  per_blk(t_len + 1, ns), per_blk(t_len + 1, ns), per_blk(ns, t_len + 1), per_blk(ns, t_len + 1)],
        out_specs=[per_blk(cw // 2, cw // 2), per_blk(cw // 2, cw // 2), per_blk(cw, 2 * ns), per_blk(2 * ns, cw)],
        out_shape=[jax.ShapeDtypeStruct((nlb, cw // 2, cw // 2), BF16),
                   jax.ShapeDtypeStruct((nlb, cw // 2, cw // 2), BF16),
                   jax.ShapeDtypeStruct((nlb, cw, 2 * ns), BF16),
                   jax.ShapeDtypeStruct((nlb, 2 * ns, cw), BF16)],
        compiler_params=_params("parallel"),
        name="ssm_operators",
    )(seed_in(bb_re), seed_in(bb_im), seed_out(c_re), seed_out(c_im),
      pw_lanes(pw_re), pw_lanes(pw_im), pw_rows(pw_re), pw_rows(pw_im))

    apow = jnp.concatenate([pw_re[t_len].reshape(nlb, 1, ns), pw_im[t_len].reshape(nlb, 1, ns)], axis=-1)
    dsk = jnp.tile(d_skip.reshape(nlb, 1, LANES), (1, 1, t_len))
    return wd, wx, v, z, apow, dsk


def _tap_source(r, back, n_slab):
    return (r - back) % n_slab, (back - r + n_slab - 1) // n_slab


def _ssm_conv_kernel(u_ref, val_ref, gate_ref, wd_ref, wx_ref, v_ref, z_ref, ap_ref, dsk_ref, cw_ref, cb_ref,
                     y_ref, c_ref, s_ref, buf_ref):
    n_batch, n_slab, n_rows, _ = u_ref.shape
    assert n_batch == 2
    b_stride = s_ref.shape[1] // n_batch
    state_rows = lambda b: slice(b * b_stride, b * b_stride + n_rows)
    nt = SSM_STATE_LANES // LANES
    half = n_slab // 2 * LANES
    lane = lambda t: slice(t * LANES, (t + 1) * LANES)
    dot = functools.partial(jnp.dot, preferred_element_type=F32)
    chunk_rows = lambda b: jnp.concatenate([u_ref[b, r] for r in range(n_slab)], axis=1)

    def fill_shifted(b):
        for r in range(n_slab):
            glu = val_ref[b, r].astype(F32) * jax.nn.sigmoid(gate_ref[b, r].astype(F32))
            for w in range(buf_ref.shape[0]):
                buf_ref[w, r, 0:CONV_PAD, :] = jnp.zeros((CONV_PAD, LANES), F32)
                buf_ref[w, r, w:w + n_rows, :] = glu

    def tap(acc, r, back, rows):
        src, wrap = _tap_source(r, back, n_slab)
        k = CONV_K - 1 - back
        return acc + cw_ref[k:k + 1, :] * buf_ref[wrap, src, rows, :]

    for b in range(n_batch):
        s = dot(chunk_rows(b), v_ref[...])
        for t in range(2 * nt):
            s_ref[t, state_rows(b), :] = s[:, lane(t)]

    ar = [jnp.broadcast_to(ap_ref[:, lane(t)], (n_batch, LANES)) for t in range(nt)]
    ai = [jnp.broadcast_to(ap_ref[:, lane(nt + t)], (n_batch, LANES)) for t in range(nt)]

    def recur(k, carry):
        idx = pl.ds(k, n_batch, stride=b_stride)
        new = []
        for t in range(nt):
            xr, xi = carry[t]
            re_ref, im_ref = s_ref.at[t], s_ref.at[nt + t]
            sr, si = re_ref[idx, :], im_ref[idx, :]
            re_ref[idx, :] = xr
            im_ref[idx, :] = xi
            new.append((ar[t] * xr - ai[t] * xi + sr, ar[t] * xi + ai[t] * xr + si))
        return tuple(new)

    fill_shifted(0)
    bias = cb_ref[...]

    def super_step(j, carry):
        for q in range(CONV_LOOP_ROWS):
            carry = recur(j * CONV_LOOP_ROWS + q, carry)
        rows = pl.ds(pl.multiple_of(j * CONV_LOOP_ROWS, CONV_LOOP_ROWS), CONV_LOOP_ROWS)
        for r_lo in range(0, n_slab, CONV_LOOP_SLABS):
            slabs = range(r_lo, r_lo + CONV_LOOP_SLABS)
            accs = {r: jnp.broadcast_to(bias, (CONV_LOOP_ROWS, LANES)) for r in slabs}
            for back in range(CONV_K):
                for r in slabs:
                    accs[r] = tap(accs[r], r, back, rows)
            for r in slabs:
                c_ref[0, r, rows, :] = accs[r].astype(c_ref.dtype)
        return carry

    zero = jnp.zeros((n_batch, LANES), F32)
    lax.fori_loop(0, n_rows // CONV_LOOP_ROWS, super_step, tuple((zero, zero) for _ in range(nt)))

    def ssm_items(b):
        env = {}

        def load():
            env["u"] = chunk_rows(b)
            env["x"] = jnp.concatenate([s_ref[t, state_rows(b), :] for t in range(2 * nt)], axis=1).astype(BF16)
            env["lo"] = dot(env["u"][:, :half], wd_ref[...])

        def finish():
            u = env["u"]
            y = jnp.concatenate([env["lo"], env["hi"]], axis=1) + dsk_ref[...] * u.astype(F32)
            y = jax.nn.gelu(y).astype(y_ref.dtype)
            for r in range(n_slab):
                y_ref[b, r] = y[:, lane(r)]

        return [load,
                lambda: env.__setitem__("lo", env["lo"] + dot(env["x"], z_ref[:, :half])),
                lambda: env.__setitem__("hi", dot(env["u"][:, :half], wx_ref[...])),
                lambda: env.__setitem__("hi", env["hi"] + dot(env["u"][:, half:], wd_ref[...])),
                lambda: env.__setitem__("hi", env["hi"] + dot(env["x"], z_ref[:, half:])),
                finish]

    def conv_item(b, r, r0):
        def run():
            acc = jnp.broadcast_to(bias, (CONV_RC, LANES))
            for back in range(CONV_K):
                acc = tap(acc, r, back, slice(r0, r0 + CONV_RC))
            c_ref[b, r, r0:r0 + CONV_RC, :] = acc.astype(c_ref.dtype)
        return run

    fill_shifted(1)
    mxu_items = ssm_items(0) + ssm_items(1)
    vpu_items = [conv_item(1, r, r0) for r in range(n_slab) for r0 in range(0, n_rows, CONV_RC)]
    per_mxu = -(-len(vpu_items) // len(mxu_items))
    for n, item in enumerate(mxu_items):
        item()
        for vpu_item in vpu_items[n * per_mxu:(n + 1) * per_mxu]:
            vpu_item()


def _ssm_conv(z, ops, conv_w, conv_b, layer, batch, seq):
    wd, wx, v, zo, apow, dsk = ops
    si = seq // SLABS
    nbs = 2
    z4 = z.reshape(batch, SLABS, si, z.shape[1])
    n_lt = CONV_WIDTH // LANES
    assert n_lt == SSM_BLOCKS and batch % nbs == 0 and si % CONV_RC == 0
    tok = lambda col0: pl.BlockSpec((nbs, SLABS, si, LANES), lambda c, b: (b, 0, 0, col0 + c))
    op = lambda a: pl.BlockSpec((None,) + a.shape[1:], lambda c, b: (layer * SSM_BLOCKS + c, 0, 0))
    per_tile = lambda n: pl.BlockSpec((None, n, LANES), lambda c, b: (c, 0, 0))
    w_tiles = conv_w.reshape(CONV_K, n_lt, LANES).transpose(1, 0, 2)
    out = jax.ShapeDtypeStruct((batch, SLABS, si, SSM_WIDTH), BF16)
    y, c = pl.pallas_call(
        _ssm_conv_kernel,
        grid=(SSM_BLOCKS, batch // nbs),
        in_specs=[tok(COL_U // LANES), tok(COL_CONV // LANES), tok((COL_CONV + CONV_WIDTH) // LANES),
                  op(wd), op(wx), op(v), op(zo), op(apow), op(dsk), per_tile(CONV_K), per_tile(1)],
        out_specs=[tok(0), tok(0)],
        out_shape=[out, out],
        scratch_shapes=[pltpu.VMEM((2 * SSM_STATE_LANES // LANES, nbs * (si + STATE_ROW_PAD), LANES), F32),
                        pltpu.VMEM((3, SLABS, si + CONV_PAD, LANES), F32)],
        compiler_params=_params("parallel", "arbitrary"),
        name="ssm_conv",
    )(z4, z4, z4, wd, wx, v, zo, apow, dsk, w_tiles, conv_b.reshape(n_lt, 1, LANES))
    return y.reshape(batch * seq, SSM_WIDTH), c.reshape(batch * seq, CONV_WIDTH)


def _merge_kernel(a_ref, y_ref, c_ref, lg_ref, lb_ref, g0_ref, g1_ref, g2_ref, wao_ref, wa_ref, wb_ref, wco_ref,
                  o_ref, cact_ref):
    dot = functools.partial(jnp.dot, preferred_element_type=F32)
    sig = lambda r: _sigmoid(r[...].astype(F32))

    def gated_sum(normalise):
        y = y_ref[...]
        o_a = dot(a_ref[...], wao_ref[...])
        o_s = dot(y, wa_ref[...]) * _sigmoid(dot(y, wb_ref[...]))
        if normalise:
            c = c_ref[...].astype(F32)
            mu = jnp.mean(c, axis=-1, keepdims=True)
            var = jnp.mean(jnp.square(c - mu), axis=-1, keepdims=True)
            ln = (c - mu) * lax.rsqrt(var + EPS) * lg_ref[...] + lb_ref[...]
            cact_ref[...] = (ln * _sigmoid(ln)).astype(cact_ref.dtype)
        o_c = dot(cact_ref[...], wco_ref[...])
        o_ref[...] = (sig(g0_ref) * o_a + sig(g1_ref) * o_s + sig(g2_ref) * o_c).astype(o_ref.dtype)

    first = pl.program_id(1) == 0
    pl.when(first)(lambda: gated_sum(True))
    pl.when(jnp.logical_not(first))(lambda: gated_sum(False))


def _merge(attn, y, conv, ln_g, ln_b, z, w_attn_out, w_glu_a, w_glu_b, w_conv_out, layer, tm=1024, tn=512):
    m = attn.shape[0]
    d = w_attn_out.shape[2]
    nt = d // tn
    col0 = COL_GATES // tn
    tok = lambda a: pl.BlockSpec((tm, a.shape[1]), lambda i, j: (i, 0))
    gate = lambda br: pl.BlockSpec((tm, tn), lambda i, j: (i, col0 + br * nt + j))
    wgt = lambda w: pl.BlockSpec((None, w.shape[1], tn), lambda i, j: (layer, 0, j))
    cw = conv.shape[1]
    vec = pl.BlockSpec((1, cw), lambda i, j: (0, 0))
    return pl.pallas_call(
        _merge_kernel,
        grid=(m // tm, nt),
        in_specs=[tok(attn), tok(y), tok(conv), vec, vec, gate(0), gate(1), gate(2),
                  wgt(w_attn_out), wgt(w_glu_a), wgt(w_glu_b), wgt(w_conv_out)],
        out_specs=pl.BlockSpec((tm, tn), lambda i, j: (i, j)),
        out_shape=jax.ShapeDtypeStruct((m, d), BF16),
        scratch_shapes=[pltpu.VMEM((tm, cw), BF16)],
        compiler_params=_params("parallel", "arbitrary"),
        name="merge",
    )(attn, y, conv, ln_g.reshape(1, cw), ln_b.reshape(1, cw), z, z, z, w_attn_out, w_glu_a, w_glu_b, w_conv_out)


def _residual_proj_kernel(a_ref, w_ref, x_ref, g_ref, xo_ref, ho_ref, acc_ref):
    k = pl.program_id(1)

    @pl.when(k == 0)
    def _():
        acc_ref[...] = x_ref[...]

    acc_ref[...] += jnp.dot(a_ref[...], w_ref[...], preferred_element_type=F32)

    @pl.when(k == pl.num_programs(1) - 1)
    def _():
        x = acc_ref[...]
        xo_ref[...] = x
        ho_ref[...] = _rmsnorm_rows(x, g_ref[...]).astype(ho_ref.dtype)


def _residual_proj(a, w_all, layer, x, g, h_dtype, tm, tk, name):
    m, kdim = a.shape
    d = w_all.shape[2]
    row = lambda i, k: (i, 0)
    return pl.pallas_call(
        _residual_proj_kernel,
        grid=(m // tm, kdim // tk),
        in_specs=[pl.BlockSpec((tm, tk), lambda i, k: (i, k)),
                  pl.BlockSpec((None, tk, d), lambda i, k: (layer, k, 0)),
                  pl.BlockSpec((tm, d), row),
                  pl.BlockSpec((1, d), lambda i, k: (0, 0))],
        out_specs=[pl.BlockSpec((tm, d), row), pl.BlockSpec((tm, d), row)],
        out_shape=[jax.ShapeDtypeStruct((m, d), F32), jax.ShapeDtypeStruct((m, d), h_dtype)],
        scratch_shapes=[pltpu.VMEM((tm, d), F32)],
        compiler_params=_params("parallel", "arbitrary"),
        name=name,
    )(a, w_all, x, g.reshape(1, d))


def _ffn_down_kernel(a_ref, w_ref, g_ref, x_hbm, xo_hbm, ho_hbm, acc_ref, ho_buf, sem_x, sem_out):
    i, k = pl.program_id(0), pl.program_id(1)
    n_i, n_k = pl.num_programs(0), pl.num_programs(1)
    slot = lax.rem(i, 2)
    x_copy = lambda tile, s: pltpu.make_async_copy(x_hbm.at[tile], acc_ref.at[s], sem_x.at[s])
    xo_copy = lambda tile, s: pltpu.make_async_copy(acc_ref.at[s], xo_hbm.at[tile], sem_out.at[0])
    ho_copy = lambda tile: pltpu.make_async_copy(ho_buf, ho_hbm.at[tile], sem_out.at[1])

    @pl.when((i == 0) & (k == 0))
    def _():
        x_copy(0, 0).start()

    @pl.when(k == 0)
    def _():
        x_copy(i, slot).wait()

    @pl.when((k == 1) & (i > 0))
    def _():
        xo_copy(i - 1, 1 - slot).wait()
        ho_copy(i - 1).wait()

    @pl.when((k == 1) & (i + 1 < n_i))
    def _():
        x_copy(i + 1, 1 - slot).start()

    acc_ref[slot] += jnp.dot(a_ref[...], w_ref[...], preferred_element_type=F32)

    @pl.when(k == n_k - 1)
    def _():
        ho_buf[...] = _rmsnorm_rows(acc_ref[slot], g_ref[...]).astype(ho_buf.dtype)
        xo_copy(i, slot).start()
        ho_copy(i).start()

    @pl.when((k == n_k - 1) & (i == n_i - 1))
    def _():
        xo_copy(i, slot).wait()
        ho_copy(i).wait()


def _ffn_down(a, w_all, layer, x, g, h_dtype, tm=1024):
    m, kdim = a.shape
    d = w_all.shape[2]

    def vmem_bytes(tk):
        return (2 * (tm * tk + tk * d) * a.dtype.itemsize + 2 * tm * d * 4 + tm * d * jnp.dtype(h_dtype).itemsize
                + VMEM_COMPILER_SCRATCH)

    n_k = next(n for n in range(2, kdim // LANES + 1)
               if kdim % (n * LANES) == 0 and vmem_bytes(kdim // n) <= VMEM_LIMIT)
    tk = kdim // n_k
    tiles = lambda dtype: jax.ShapeDtypeStruct((m // tm, tm, d), dtype)
    hbm = pl.BlockSpec(memory_space=pl.ANY)
    xo, ho = pl.pallas_call(
        _ffn_down_kernel,
        grid=(m // tm, n_k),
        in_specs=[pl.BlockSpec((tm, tk), lambda i, k: (i, k)),
                  pl.BlockSpec((None, tk, d), lambda i, k: (layer, k, 0)),
                  pl.BlockSpec((1, d), lambda i, k: (0, 0)),
                  hbm],
        out_specs=[hbm, hbm],
        out_shape=[tiles(F32), tiles(h_dtype)],
        scratch_shapes=[pltpu.VMEM((2, tm, d), F32), pltpu.VMEM((tm, d), h_dtype),
                        pltpu.SemaphoreType.DMA((2,)), pltpu.SemaphoreType.DMA((2,))],
        compiler_params=_params("arbitrary", "arbitrary"),
        name="ffn_down",
    )(a, w_all, g.reshape(1, d), x.reshape(m // tm, tm, d))
    return xo.reshape(m, d), ho.reshape(m, d)


def _ffn_up_kernel(h_ref, wg_ref, wu_ref, o_ref, wgb_ref, wub_ref):
    @pl.when(pl.program_id(1) == 0)
    def _():
        wgb_ref[...] = wg_ref[...].astype(BF16)
        wub_ref[...] = wu_ref[...].astype(BF16)

    for r0 in range(0, h_ref.shape[0], FFN_ROW_CHUNK):
        rows = slice(r0, r0 + FFN_ROW_CHUNK)
        h = h_ref[rows, :]
        gate = jnp.dot(h, wgb_ref[...], preferred_element_type=F32)
        up = jnp.dot(h, wub_ref[...], preferred_element_type=F32)
        o_ref[rows, :] = (jax.nn.silu(gate) * up).astype(o_ref.dtype)


def _ffn_up(h, wg_all, wu_all, layer, tm=2048, tn=512):
    m, k = h.shape
    n = wg_all.shape[2]
    wspec = pl.BlockSpec((None, k, tn), lambda j, i: (layer, 0, j))
    return pl.pallas_call(
        _ffn_up_kernel,
        grid=(n // tn, m // tm),
        in_specs=[pl.BlockSpec((tm, k), lambda j, i: (i, 0)), wspec, wspec],
        out_specs=pl.BlockSpec((tm, tn), lambda j, i: (i, j)),
        out_shape=jax.ShapeDtypeStruct((m, n), BF16),
        scratch_shapes=[pltpu.VMEM((k, tn), BF16), pltpu.VMEM((k, tn), BF16)],
        compiler_params=_params("arbitrary", "arbitrary"),
        name="ffn_up",
    )(h, wg_all, wu_all)


def kernel(x, w_in, rel_bias, lam_re, lam_im, log_dt, b_re, b_im, c_re, c_im, d_skip, w_ssm_glu_a, w_ssm_glu_b, conv_w, conv_b, conv_ln_g, conv_ln_b, w_conv_out, w_attn_out, w_out, norm_mix_g, norm_ffn_g, w_ffn_gate, w_ffn_up, w_ffn_down, norm_final_g):
    batch, seq, d_model = x.shape
    m = batch * seq
    si = seq // SLABS
    xf = x.reshape(batch, si, SLABS, d_model).transpose(0, 2, 1, 3).reshape(m, d_model)
    bands = [_band_bias(rel_bias, g) for g in range(N_GROUPS)]
    ssm_ops = _ssm_operators(lam_re, lam_im, log_dt, b_re, b_im, c_re, c_im, d_skip)
    col_scale = jnp.where(jnp.arange(w_in.shape[2]) < ATTN_QW, ATTN_SCALE, 1.0).astype(F32)[None, :]
    bf = lambda w: w.astype(BF16)
    w_attn_out, w_ssm_glu_a, w_ssm_glu_b, w_conv_out = bf(w_attn_out), bf(w_ssm_glu_a), bf(w_ssm_glu_b), bf(w_conv_out)
    w_out, w_ffn_down = bf(w_out), bf(w_ffn_down)

    h = _rmsnorm(xf, norm_mix_g[0], BF16)
    for l in range(DEPTH):
        z = _inproj(h, w_in, l, col_scale)
        attn = _attention(z, bands, batch, seq)
        y, conv = _ssm_conv(z, ssm_ops, conv_w[l], conv_b[l], l, batch, seq)
        merged = _merge(attn, y, conv, conv_ln_g[l], conv_ln_b[l], z, w_attn_out, w_ssm_glu_a, w_ssm_glu_b,
                        w_conv_out, l)
        xf, h = _residual_proj(merged, w_out, l, xf, norm_ffn_g[l], BF16, 512, d_model, "out_proj")
        act = _ffn_up(h, w_ffn_gate, w_ffn_up, l)
        last = l == DEPTH - 1
        g_next = norm_final_g if last else norm_mix_g[l + 1]
        xf, h = _ffn_down(act, w_ffn_down, l, xf, g_next, F32 if last else BF16)
    return h.reshape(batch, SLABS, si, d_model).transpose(0, 2, 1, 3).reshape(batch, seq, d_model)
```

```python
import functools
import math

import jax
import jax.numpy as jnp
import numpy as np
from jax import lax
from jax.experimental import pallas as pl
from jax.experimental.pallas import tpu as pltpu

F32 = jnp.float32
BF16 = jnp.bfloat16

D_MODEL = 2048
DEPTH = 4
HEAD_DIM = 64
HEADS_PER_GROUP = 8
ATTN_GROUPS = ((128, 1), (512, 4), (2048, 16))
N_GROUPS = len(ATTN_GROUPS)
N_ATTN_HEADS = HEADS_PER_GROUP * N_GROUPS
ATTN_QW = N_ATTN_HEADS * HEAD_DIM
ATTN_OUT_W = HEADS_PER_GROUP * HEAD_DIM
ATTN_LAGS = ATTN_GROUPS[0][0] // ATTN_GROUPS[0][1]
NUM_BUCKETS = 32
MAX_DISTANCE = max(w for w, _ in ATTN_GROUPS)
ATTN_SCALE = HEAD_DIM ** -0.5
NEG_INF = -1e30
SSM_WIDTH = 1024
SSM_GROUP = 16
SSM_STATE = 64
CONV_WIDTH = 1024
CONV_K = 31
D_FF = -(-8 * D_MODEL // (3 * 256)) * 256
EPS = 1e-6
COL_U = 3 * ATTN_QW
COL_CONV = COL_U + SSM_WIDTH
COL_GATES = COL_CONV + 2 * CONV_WIDTH

LANES = 128
BF16_ROWS = 16
VMEM_LIMIT = 56 * 2 ** 20
VMEM_COMPILER_SCRATCH = 2 ** 20

SLABS = max(d for _, d in ATTN_GROUPS)
ATTN_TILES = ((8, 8, 64, 1), (32, 32, 128, 2), (128, 128, 256, 4))
HEAD_PAIR_W = 2 * HEAD_DIM
SSM_CHUNK = SLABS
SSM_BLOCKS = SSM_WIDTH // LANES
SSM_GROUPS_PER_BLOCK = LANES // SSM_GROUP
SSM_STATE_LANES = SSM_GROUPS_PER_BLOCK * SSM_STATE
FFN_ROW_CHUNK = 1024
STATE_ROW_PAD = 8
CONV_RC = 64
CONV_PAD = 8
CONV_LOOP_ROWS = 16
CONV_LOOP_SLABS = 8


def _sigmoid(x):
    return 0.5 * jnp.tanh(0.5 * x) + 0.5


def _params(*sem):
    return pltpu.CompilerParams(dimension_semantics=sem, vmem_limit_bytes=VMEM_LIMIT)


def _rmsnorm_rows(x, g):
    return x * lax.rsqrt(jnp.mean(x * x, axis=-1, keepdims=True) + EPS) * g


def _rmsnorm_kernel(x_ref, g_ref, o_ref):
    o_ref[...] = _rmsnorm_rows(x_ref[...], g_ref[...]).astype(o_ref.dtype)


def _rmsnorm(x, g, out_dtype, tm=512):
    m, d = x.shape
    return pl.pallas_call(
        _rmsnorm_kernel,
        grid=(m // tm,),
        in_specs=[pl.BlockSpec((tm, d), lambda i: (i, 0)),
                  pl.BlockSpec((1, d), lambda i: (0, 0))],
        out_specs=pl.BlockSpec((tm, d), lambda i: (i, 0)),
        out_shape=jax.ShapeDtypeStruct((m, d), out_dtype),
        compiler_params=_params("parallel"),
        name="rmsnorm",
    )(x, g.reshape(1, d))


def _inproj_kernel(a_ref, w_ref, scale_ref, o_ref, wb_ref):
    @pl.when(pl.program_id(1) == 0)
    def _():
        wb_ref[...] = (w_ref[...] * scale_ref[...]).astype(BF16)

    o_ref[...] = jnp.dot(a_ref[...], wb_ref[...], preferred_element_type=F32).astype(o_ref.dtype)


def _inproj(h, w_all, layer, col_scale, tm=1024, tn=1536):
    m, k = h.shape
    n = w_all.shape[2]
    return pl.pallas_call(
        _inproj_kernel,
        grid=(n // tn, m // tm),
        in_specs=[pl.BlockSpec((tm, k), lambda j, i: (i, 0)),
                  pl.BlockSpec((None, k, tn), lambda j, i: (layer, 0, j)),
                  pl.BlockSpec((1, tn), lambda j, i: (0, j))],
        out_specs=pl.BlockSpec((tm, tn), lambda j, i: (i, j)),
        out_shape=jax.ShapeDtypeStruct((m, n), BF16),
        scratch_shapes=[pltpu.VMEM((k, tn), BF16)],
        compiler_params=_params("arbitrary", "arbitrary"),
        name="inproj",
    )(h, w_all, col_scale)


def _t5_bucket_causal(dist):
    max_exact = NUM_BUCKETS // 2
    d = np.maximum(dist, 1).astype(np.float32)
    large = max_exact + (np.log(d / max_exact) / math.log(MAX_DISTANCE / max_exact)
                         * (NUM_BUCKETS - max_exact)).astype(np.int32)
    large = np.minimum(large, NUM_BUCKETS - 1)
    return np.where(dist < max_exact, dist, large).astype(np.int32)


def _band_bias(rel_bias, g):
    _, d = ATTN_GROUPS[g]
    n_c = SLABS // d
    ni, n_prev = ATTN_TILES[g][:2]
    qc, qi = np.divmod(np.arange(n_c * ni), ni)
    kc, ki = np.divmod(np.arange(n_c * (n_prev + ni)), n_prev + ni)
    lag = n_c * (qi[:, None] - ki[None, :] + n_prev) + (qc[:, None] - kc[None, :])
    on_band = (lag >= 0) & (lag <= ATTN_LAGS)
    bucket = np.where(on_band, _t5_bucket_causal(np.clip(lag, 0, ATTN_LAGS) * d), -1).astype(np.int32)

    def band_kernel(rel_ref, bucket_ref, out_ref):
        bkt = bucket_ref[...]
        for h in range(HEADS_PER_GROUP):
            acc = jnp.full(bkt.shape, NEG_INF, F32)
            for b in range(NUM_BUCKETS):
                acc = jnp.where(bkt == b, rel_ref[b, g * HEADS_PER_GROUP + h], acc)
            out_ref[h] = acc

    return pl.pallas_call(
        band_kernel,
        in_specs=[pl.BlockSpec(memory_space=pltpu.SMEM), pl.BlockSpec(memory_space=pltpu.VMEM)],
        out_specs=pl.BlockSpec(memory_space=pltpu.VMEM),
        out_shape=jax.ShapeDtypeStruct((HEADS_PER_GROUP,) + bucket.shape, F32),
        name=f"band_bias_g{g}",
    )(rel_bias, jnp.asarray(bucket))


def _attn_kernel(*refs, ni, n_prev, chained, last):
    q_ref, kp_ref, kc_ref, vp_ref, vc_ref, bm_ref = refs[:6]
    po_ref, pl_ref = refs[6:8] if chained else (None, None)
    o_ref = refs[8 if chained else 6]
    lse_ref = None if last else refs[-1]
    tile = pl.program_id(2)
    n_c, n_cls, rows, _ = q_ref.shape
    hist = kp_ref.shape[2]
    win = n_prev + ni
    n_pair = HEADS_PER_GROUP // 2
    pair = lambda hp: slice(hp * HEAD_PAIR_W, (hp + 1) * HEAD_PAIR_W)
    stage = BF16 if ni % BF16_ROWS == 0 else F32
    per_store = max(1, BF16_ROWS // ni)
    upper = lax.broadcasted_iota(jnp.int32, (1, HEAD_PAIR_W), 1) >= HEAD_DIM
    nt_dims = (((1,), (1,)), ((), ()))

    def sub_tile(qs, kwin, vwin, t):
        r0, k0 = t * ni, hist - n_prev + t * ni
        q = jnp.concatenate([qs[c][r0:r0 + ni] for c in range(n_c)], axis=0).astype(BF16)
        kk = jnp.concatenate([kwin[c][k0:k0 + win] for c in range(n_c)], axis=0).astype(BF16)
        vv = jnp.concatenate([vwin[c][k0:k0 + win] for c in range(n_c)], axis=0).astype(BF16)
        logits = []
        for h in range(HEADS_PER_GROUP):
            q2 = q[:, pair(h // 2)]
            qm = jnp.where(upper if h % 2 else ~upper, q2, jnp.zeros_like(q2))
            s = lax.dot_general(qm, kk[:, pair(h // 2)], nt_dims, preferred_element_type=F32) + bm_ref[h]
            if t == 0:
                col = lax.broadcasted_iota(jnp.int32, s.shape, 1)
                s = jnp.where((col % win >= n_prev) | (tile > 0), s, NEG_INF)
            logits.append(s)
        mx = [jnp.max(s, axis=-1, keepdims=True) for s in logits]
        probs = [jnp.exp(s - m).astype(BF16) for s, m in zip(logits, mx)]
        out = []
        for hp in range(n_pair):
            v2 = vv[:, pair(hp)]
            one = jnp.ones_like(v2)
            o0 = jnp.dot(probs[2 * hp], jnp.where(upper, one, v2), preferred_element_type=F32)
            o1 = jnp.dot(probs[2 * hp + 1], jnp.where(upper, v2, one), preferred_element_type=F32)
            den = pltpu.roll(jnp.where(upper, o0, o1), HEAD_DIM, axis=1)
            out.append((jnp.where(upper, o1, o0) * (1.0 / den),
                        jnp.where(upper, mx[2 * hp + 1], mx[2 * hp]) + jnp.log(den)))
        return out

    for cls in range(n_cls):
        qs = [q_ref[c, cls].astype(stage) for c in range(n_c)]
        kwin = [jnp.concatenate([kp_ref[c, cls], kc_ref[c, cls]], axis=0).astype(stage) for c in range(n_c)]
        vwin = [jnp.concatenate([vp_ref[c, cls], vc_ref[c, cls]], axis=0).astype(stage) for c in range(n_c)]
        for t0 in range(0, rows // ni, per_store):
            done = [sub_tile(qs, kwin, vwin, t) for t in range(t0, t0 + per_store)]
            rs = slice(t0 * ni, (t0 + per_store) * ni)
            for hp in range(n_pair):
                for c in range(n_c):
                    cs = slice(c * ni, (c + 1) * ni)
                    o = jnp.concatenate([d[hp][0][cs] for d in done], axis=0)
                    lse = jnp.concatenate([d[hp][1][cs] for d in done], axis=0)
                    if chained:
                        o_p, lse_p = po_ref[c, cls, rs, pair(hp)].astype(F32), pl_ref[c, cls, rs, pair(hp)]
                        top = jnp.maximum(lse_p, lse)
                        e_p, e_c = jnp.exp(lse_p - top), jnp.exp(lse - top)
                        o = (e_p * o_p + e_c * o) / (e_p + e_c)
                        lse = top + jnp.log(e_p + e_c)
                    o_ref[c, cls, rs, pair(hp)] = o.astype(o_ref.dtype)
                    if not last:
                        lse_ref[c, cls, rs, pair(hp)] = lse


def _attention_group(z, bm, g, batch, seq, done=None):
    _, d = ATTN_GROUPS[g]
    n_c, n_r = SLABS // d, d
    ni, n_prev, rows, n_cls = ATTN_TILES[g]
    si = seq // SLABS
    rows = min(rows, si)
    z5 = z.reshape(batch, n_c, n_r, si, z.shape[1])
    kb = ATTN_QW // ATTN_OUT_W
    hist = max(n_prev, BF16_ROWS)
    prev_per_step = rows // hist

    def cur(off):
        return pl.BlockSpec((None, n_c, n_cls, rows, ATTN_OUT_W), lambda b, r, i: (b, 0, r, i, off + g))

    def prev(off):
        return pl.BlockSpec((None, n_c, n_cls, hist, ATTN_OUT_W),
                            lambda b, r, i: (b, 0, r, jnp.maximum(i * prev_per_step - 1, 0), off + g))

    out_spec = pl.BlockSpec((None, n_c, n_cls, rows, ATTN_OUT_W), lambda b, r, i: (b, 0, r, i, 0))
    last = g == N_GROUPS - 1
    shape5 = (batch, n_c, n_r, si, ATTN_OUT_W)
    prev_args = [] if done is None else [a.reshape(shape5) for a in done]
    res = pl.pallas_call(
        functools.partial(_attn_kernel, ni=ni, n_prev=n_prev, chained=done is not None, last=last),
        grid=(batch, n_r // n_cls, si // rows),
        in_specs=[cur(0), prev(kb), cur(kb), prev(2 * kb), cur(2 * kb),
                  pl.BlockSpec(bm.shape, lambda b, r, i: (0, 0, 0))] + [out_spec] * len(prev_args),
        out_specs=[out_spec] * (1 if last else 2),
        out_shape=[jax.ShapeDtypeStruct(shape5, BF16)] + ([] if last else [jax.ShapeDtypeStruct(shape5, F32)]),
        compiler_params=_params("parallel", "parallel", "arbitrary"),
        name=f"attn_g{g}",
    )(z5, z5, z5, z5, z5, bm, *prev_args)
    return tuple(a.reshape(batch * seq, ATTN_OUT_W) for a in res)


def _attention(z, bands, batch, seq):
    done = None
    for g in range(N_GROUPS):
        done = _attention_group(z, bands[g], g, batch, seq, done)
    return done[0]


def _ssm_ops_kernel(v0r_ref, v0i_ref, cmr_ref, cmi_ref, pwr_ref, pwi_ref, pwrt_ref, pwit_ref,
                    wd_ref, wx_ref, v_ref, z_ref):
    t_len, half = SSM_CHUNK, SSM_CHUNK // 2
    blk = lambda i: slice(i * LANES, (i + 1) * LANES)
    v0r, v0i = v0r_ref[...], v0i_ref[...]
    cmr, cmi = cmr_ref[...], cmi_ref[...]
    split = lambda a: (a.astype(BF16), (a - a.astype(BF16).astype(F32)).astype(BF16))
    cm = jnp.concatenate([cmr, -cmi], axis=0)
    cm_hi, cm_lo = split(cm)
    cm3 = jnp.concatenate([cm_hi, cm_lo, cm_hi], axis=0)
    kern = []
    for tau in range(t_len):
        pr, pi = pwr_ref[tau:tau + 1, :], pwi_ref[tau:tau + 1, :]
        vt = jnp.concatenate([v0r * pr - v0i * pi, v0r * pi + v0i * pr], axis=1)
        v_ref[blk(t_len - 1 - tau), :] = vt.astype(v_ref.dtype)
        vt_hi, vt_lo = split(vt)
        kern.append(jnp.dot(jnp.concatenate([vt_hi, vt_hi, vt_lo], axis=1), cm3, preferred_element_type=F32))
    zero = jnp.zeros((LANES, LANES), F32)
    for j in range(half):
        wd_ref[blk(j), :] = jnp.concatenate([kern[i - j] if i >= j else zero for i in range(half)],
                                            axis=1).astype(wd_ref.dtype)
        wx_ref[blk(j), :] = jnp.concatenate([kern[half + i - j] for i in range(half)], axis=1).astype(wx_ref.dtype)
    ns = SSM_STATE_LANES
    for i in range(t_len):
        pr, pi = pwrt_ref[:, i + 1:i + 2], pwit_ref[:, i + 1:i + 2]
        z_ref[0:ns, blk(i)] = (cmr * pr - cmi * pi).astype(z_ref.dtype)
        z_ref[ns:2 * ns, blk(i)] = (-(cmr * pi + cmi * pr)).astype(z_ref.dtype)


def _ssm_operators(lam_re, lam_im, log_dt, b_re, b_im, c_re, c_im, d_skip):
    t_len = SSM_CHUNK
    n_l, n_g, n_p = lam_re.shape
    nb, gl, ns = SSM_BLOCKS, SSM_GROUPS_PER_BLOCK, SSM_STATE_LANES
    dt = jnp.exp(log_dt)[..., None]
    lr, li = lam_re, lam_im
    mag = jnp.exp(lr * dt)
    ab_re, ab_im = mag * jnp.cos(li * dt), mag * jnp.sin(li * dt)
    nr, ni = ab_re - 1.0, ab_im
    den = lr * lr + li * li
    z_re, z_im = (nr * lr + ni * li) / den, (ni * lr - nr * li) / den
    bb_re = z_re[..., None] * b_re - z_im[..., None] * b_im
    bb_im = z_re[..., None] * b_im + z_im[..., None] * b_re
    tau = jnp.arange(t_len + 1, dtype=F32)[:, None, None, None]
    pmag = jnp.exp(lr * dt * tau)
    pw_re, pw_im = pmag * jnp.cos(li * dt * tau), pmag * jnp.sin(li * dt * tau)

    eye = jnp.eye(gl, dtype=F32)
    nlb = n_l * nb
    seed_in = lambda b: jnp.einsum('xgpc,gh->xgchp', b.reshape(nlb, gl, n_p, SSM_GROUP), eye).reshape(nlb, LANES, ns)
    seed_out = lambda c: jnp.einsum('xgop,gh->xgpho', c.reshape(nlb, gl, SSM_GROUP, n_p), eye).reshape(nlb, ns, LANES)
    pw_lanes = lambda p: p.reshape(t_len + 1, nlb, ns).transpose(1, 0, 2)
    pw_rows = lambda p: p.reshape(t_len + 1, nlb, ns).transpose(1, 2, 0)

    per_blk = lambda *shape: pl.BlockSpec((None,) + shape, lambda x: (x, 0, 0))
    cw = t_len * LANES
    wd, wx, v, z = pl.pallas_call(
        _ssm_ops_kernel,
        grid=(nlb,),
        in_specs=[per_blk(LANES, ns), per_blk(LANES, ns), per_blk(ns, LANES), per_blk(ns, LANES),
                  per_blk(t_len + 1, ns), per_blk(t_len + 1, ns), per_blk(ns, t_len + 1), per_blk(ns, t_len + 1)],
        out_specs=[per_blk(cw // 2, cw // 2), per_blk(cw // 2, cw // 2), per_blk(cw, 2 * ns), per_blk(2 * ns, cw)],
        out_shape=[jax.ShapeDtypeStruct((nlb, cw // 2, cw // 2), BF16),
                   jax.ShapeDtypeStruct((nlb, cw // 2, cw // 2), BF16),
                   jax.ShapeDtypeStruct((nlb, cw, 2 * ns), BF16),
                   jax.ShapeDtypeStruct((nlb, 2 * ns, cw), BF16)],
        compiler_params=_params("parallel"),
        name="ssm_operators",
    )(seed_in(bb_re), seed_in(bb_im), seed_out(c_re), seed_out(c_im),
      pw_lanes(pw_re), pw_lanes(pw_im), pw_rows(pw_re), pw_rows(pw_im))

    apow = jnp.concatenate([pw_re[t_len].reshape(nlb, 1, ns), pw_im[t_len].reshape(nlb, 1, ns)], axis=-1)
    dsk = jnp.tile(d_skip.reshape(nlb, 1, LANES), (1, 1, t_len))
    return wd, wx, v, z, apow, dsk


def _tap_source(r, back, n_slab):
    return (r - back) % n_slab, (back - r + n_slab - 1) // n_slab


def _ssm_conv_kernel(u_ref, val_ref, gate_ref, wd_ref, wx_ref, v_ref, z_ref, ap_ref, dsk_ref, cw_ref, cb_ref,
                     y_ref, c_ref, s_ref, buf_ref):
    n_batch, n_slab, n_rows, _ = u_ref.shape
    assert n_batch == 2
    b_stride = s_ref.shape[1] // n_batch
    state_rows = lambda b: slice(b * b_stride, b * b_stride + n_rows)
    nt = SSM_STATE_LANES // LANES
    half = n_slab // 2 * LANES
    lane = lambda t: slice(t * LANES, (t + 1) * LANES)
    dot = functools.partial(jnp.dot, preferred_element_type=F32)
    chunk_rows = lambda b: jnp.concatenate([u_ref[b, r] for r in range(n_slab)], axis=1)

    def fill_shifted(b):
        for r in range(n_slab):
            glu = val_ref[b, r].astype(F32) * jax.nn.sigmoid(gate_ref[b, r].astype(F32))
            for w in range(buf_ref.shape[0]):
                buf_ref[w, r, 0:CONV_PAD, :] = jnp.zeros((CONV_PAD, LANES), F32)
                buf_ref[w, r, w:w + n_rows, :] = glu

    def tap(acc, r, back, rows):
        src, wrap = _tap_source(r, back, n_slab)
        k = CONV_K - 1 - back
        return acc + cw_ref[k:k + 1, :] * buf_ref[wrap, src, rows, :]

    for b in range(n_batch):
        s = dot(chunk_rows(b), v_ref[...])
        for t in range(2 * nt):
            s_ref[t, state_rows(b), :] = s[:, lane(t)]

    ar = [jnp.broadcast_to(ap_ref[:, lane(t)], (n_batch, LANES)) for t in range(nt)]
    ai = [jnp.broadcast_to(ap_ref[:, lane(nt + t)], (n_batch, LANES)) for t in range(nt)]

    def recur(k, carry):
        idx = pl.ds(k, n_batch, stride=b_stride)
        new = []
        for t in range(nt):
            xr, xi = carry[t]
            re_ref, im_ref = s_ref.at[t], s_ref.at[nt + t]
            sr, si = re_ref[idx, :], im_ref[idx, :]
            re_ref[idx, :] = xr
            im_ref[idx, :] = xi
            new.append((ar[t] * xr - ai[t] * xi + sr, ar[t] * xi + ai[t] * xr + si))
        return tuple(new)

    fill_shifted(0)
    bias = cb_ref[...]

    def super_step(j, carry):
        for q in range(CONV_LOOP_ROWS):
            carry = recur(j * CONV_LOOP_ROWS + q, carry)
        rows = pl.ds(pl.multiple_of(j * CONV_LOOP_ROWS, CONV_LOOP_ROWS), CONV_LOOP_ROWS)
        for r_lo in range(0, n_slab, CONV_LOOP_SLABS):
            slabs = range(r_lo, r_lo + CONV_LOOP_SLABS)
            accs = {r: jnp.broadcast_to(bias, (CONV_LOOP_ROWS, LANES)) for r in slabs}
            for back in range(CONV_K):
                for r in slabs:
                    accs[r] = tap(accs[r], r, back, rows)
            for r in slabs:
                c_ref[0, r, rows, :] = accs[r].astype(c_ref.dtype)
        return carry

    zero = jnp.zeros((n_batch, LANES), F32)
    lax.fori_loop(0, n_rows // CONV_LOOP_ROWS, super_step, tuple((zero, zero) for _ in range(nt)))

    def ssm_items(b):
        env = {}

        def load():
            env["u"] = chunk_rows(b)
            env["x"] = jnp.concatenate([s_ref[t, state_rows(b), :] for t in range(2 * nt)], axis=1).astype(BF16)
            env["lo"] = dot(env["u"][:, :half], wd_ref[...])

        def finish():
            u = env["u"]
            y = jnp.concatenate([env["lo"], env["hi"]], axis=1) + dsk_ref[...] * u.astype(F32)
            y = jax.nn.gelu(y).astype(y_ref.dtype)
            for r in range(n_slab):
                y_ref[b, r] = y[:, lane(r)]

        return [load,
                lambda: env.__setitem__("lo", env["lo"] + dot(env["x"], z_ref[:, :half])),
                lambda: env.__setitem__("hi", dot(env["u"][:, :half], wx_ref[...])),
                lambda: env.__setitem__("hi", env["hi"] + dot(env["u"][:, half:], wd_ref[...])),
                lambda: env.__setitem__("hi", env["hi"] + dot(env["x"], z_ref[:, half:])),
                finish]

    def conv_item(b, r, r0):
        def run():
            acc = jnp.broadcast_to(bias, (CONV_RC, LANES))
            for back in range(CONV_K):
                acc = tap(acc, r, back, slice(r0, r0 + CONV_RC))
            c_ref[b, r, r0:r0 + CONV_RC, :] = acc.astype(c_ref.dtype)
        return run

    fill_shifted(1)
    mxu_items = ssm_items(0) + ssm_items(1)
    vpu_items = [conv_item(1, r, r0) for r in range(n_slab) for r0 in range(0, n_rows, CONV_RC)]
    per_mxu = -(-len(vpu_items) // len(mxu_items))
    for n, item in enumerate(mxu_items):
        item()
        for vpu_item in vpu_items[n * per_mxu:(n + 1) * per_mxu]:
            vpu_item()


def _ssm_conv(z, ops, conv_w, conv_b, layer, batch, seq):
    wd, wx, v, zo, apow, dsk = ops
    si = seq // SLABS
    nbs = 2
    z4 = z.reshape(batch, SLABS, si, z.shape[1])
    n_lt = CONV_WIDTH // LANES
    assert n_lt == SSM_BLOCKS and batch % nbs == 0 and si % CONV_RC == 0
    tok = lambda col0: pl.BlockSpec((nbs, SLABS, si, LANES), lambda c, b: (b, 0, 0, col0 + c))
    op = lambda a: pl.BlockSpec((None,) + a.shape[1:], lambda c, b: (layer * SSM_BLOCKS + c, 0, 0))
    per_tile = lambda n: pl.BlockSpec((None, n, LANES), lambda c, b: (c, 0, 0))
    w_tiles = conv_w.reshape(CONV_K, n_lt, LANES).transpose(1, 0, 2)
    out = jax.ShapeDtypeStruct((batch, SLABS, si, SSM_WIDTH), BF16)
    y, c = pl.pallas_call(
        _ssm_conv_kernel,
        grid=(SSM_BLOCKS, batch // nbs),
        in_specs=[tok(COL_U // LANES), tok(COL_CONV // LANES), tok((COL_CONV + CONV_WIDTH) // LANES),
                  op(wd), op(wx), op(v), op(zo), op(apow), op(dsk), per_tile(CONV_K), per_tile(1)],
        out_specs=[tok(0), tok(0)],
        out_shape=[out, out],
        scratch_shapes=[pltpu.VMEM((2 * SSM_STATE_LANES // LANES, nbs * (si + STATE_ROW_PAD), LANES), F32),
                        pltpu.VMEM((3, SLABS, si + CONV_PAD, LANES), F32)],
        compiler_params=_params("parallel", "arbitrary"),
        name="ssm_conv",
    )(z4, z4, z4, wd, wx, v, zo, apow, dsk, w_tiles, conv_b.reshape(n_lt, 1, LANES))
    return y.reshape(batch * seq, SSM_WIDTH), c.reshape(batch * seq, CONV_WIDTH)


def _merge_kernel(a_ref, y_ref, c_ref, lg_ref, lb_ref, g0_ref, g1_ref, g2_ref, wao_ref, wa_ref, wb_ref, wco_ref,
                  o_ref, cact_ref):
    dot = functools.partial(jnp.dot, preferred_element_type=F32)
    sig = lambda r: _sigmoid(r[...].astype(F32))

    def gated_sum(normalise):
        y = y_ref[...]
        o_a = dot(a_ref[...], wao_ref[...])
        o_s = dot(y, wa_ref[...]) * _sigmoid(dot(y, wb_ref[...]))
        if normalise:
            c = c_ref[...].astype(F32)
            mu = jnp.mean(c, axis=-1, keepdims=True)
            var = jnp.mean(jnp.square(c - mu), axis=-1, keepdims=True)
            ln = (c - mu) * lax.rsqrt(var + EPS) * lg_ref[...] + lb_ref[...]
            cact_ref[...] = (ln * _sigmoid(ln)).astype(cact_ref.dtype)
        o_c = dot(cact_ref[...], wco_ref[...])
        o_ref[...] = (sig(g0_ref) * o_a + sig(g1_ref) * o_s + sig(g2_ref) * o_c).astype(o_ref.dtype)

    first = pl.program_id(1) == 0
    pl.when(first)(lambda: gated_sum(True))
    pl.when(jnp.logical_not(first))(lambda: gated_sum(False))


def _merge(attn, y, conv, ln_g, ln_b, z, w_attn_out, w_glu_a, w_glu_b, w_conv_out, layer, tm=1024, tn=512):
    m = attn.shape[0]
    d = w_attn_out.shape[2]
    nt = d // tn
    col0 = COL_GATES // tn
    tok = lambda a: pl.BlockSpec((tm, a.shape[1]), lambda i, j: (i, 0))
    gate = lambda br: pl.BlockSpec((tm, tn), lambda i, j: (i, col0 + br * nt + j))
    wgt = lambda w: pl.BlockSpec((None, w.shape[1], tn), lambda i, j: (layer, 0, j))
    cw = conv.shape[1]
    vec = pl.BlockSpec((1, cw), lambda i, j: (0, 0))
    return pl.pallas_call(
        _merge_kernel,
        grid=(m // tm, nt),
        in_specs=[tok(attn), tok(y), tok(conv), vec, vec, gate(0), gate(1), gate(2),
                  wgt(w_attn_out), wgt(w_glu_a), wgt(w_glu_b), wgt(w_conv_out)],
        out_specs=pl.BlockSpec((tm, tn), lambda i, j: (i, j)),
        out_shape=jax.ShapeDtypeStruct((m, d), BF16),
        scratch_shapes=[pltpu.VMEM((tm, cw), BF16)],
        compiler_params=_params("parallel", "arbitrary"),
        name="merge",
    )(attn, y, conv, ln_g.reshape(1, cw), ln_b.reshape(1, cw), z, z, z, w_attn_out, w_glu_a, w_glu_b, w_conv_out)


def _residual_proj_kernel(a_ref, w_ref, g_ref, x_hbm, xo_hbm, ho_hbm, acc_ref, ho_buf, sem_x, sem_out):
    i, k = pl.program_id(0), pl.program_id(1)
    n_i, n_k = pl.num_programs(0), pl.num_programs(1)
    slot = lax.rem(i, 2)
    x_copy = lambda tile, s: pltpu.make_async_copy(x_hbm.at[tile], acc_ref.at[s], sem_x.at[s])
    xo_copy = lambda tile, s: pltpu.make_async_copy(acc_ref.at[s], xo_hbm.at[tile], sem_out.at[0])
    ho_copy = lambda tile: pltpu.make_async_copy(ho_buf, ho_hbm.at[tile], sem_out.at[1])

    @pl.when((i == 0) & (k == 0))
    def _():
        x_copy(0, 0).start()

    @pl.when(k == 0)
    def _():
        x_copy(i, slot).wait()

    @pl.when((k == 1) & (i > 0))
    def _():
        xo_copy(i - 1, 1 - slot).wait()
        ho_copy(i - 1).wait()

    @pl.when((k == 1) & (i + 1 < n_i))
    def _():
        x_copy(i + 1, 1 - slot).start()

    acc_ref[slot] += jnp.dot(a_ref[...], w_ref[...], preferred_element_type=F32)

    @pl.when(k == n_k - 1)
    def _():
        ho_buf[...] = _rmsnorm_rows(acc_ref[slot], g_ref[...]).astype(ho_buf.dtype)
        xo_copy(i, slot).start()
        ho_copy(i).start()

    @pl.when((k == n_k - 1) & (i == n_i - 1))
    def _():
        xo_copy(i, slot).wait()
        ho_copy(i).wait()


def _residual_proj(a, w_all, layer, x, g, h_dtype, name, tm=1024):
    m, kdim = a.shape
    d = w_all.shape[2]

    def vmem_bytes(tk):
        return (2 * (tm * tk + tk * d) * a.dtype.itemsize + 2 * tm * d * 4 + tm * d * jnp.dtype(h_dtype).itemsize
                + VMEM_COMPILER_SCRATCH)

    n_k = next(n for n in range(2, kdim // LANES + 1)
               if kdim % (n * LANES) == 0 and vmem_bytes(kdim // n) <= VMEM_LIMIT)
    tk = kdim // n_k
    tiles = lambda dtype: jax.ShapeDtypeStruct((m // tm, tm, d), dtype)
    hbm = pl.BlockSpec(memory_space=pl.ANY)
    xo, ho = pl.pallas_call(
        _residual_proj_kernel,
        grid=(m // tm, n_k),
        in_specs=[pl.BlockSpec((tm, tk), lambda i, k: (i, k)),
                  pl.BlockSpec((None, tk, d), lambda i, k: (layer, k, 0)),
                  pl.BlockSpec((1, d), lambda i, k: (0, 0)),
                  hbm],
        out_specs=[hbm, hbm],
        out_shape=[tiles(F32), tiles(h_dtype)],
        scratch_shapes=[pltpu.VMEM((2, tm, d), F32), pltpu.VMEM((tm, d), h_dtype),
                        pltpu.SemaphoreType.DMA((2,)), pltpu.SemaphoreType.DMA((2,))],
        compiler_params=_params("arbitrary", "arbitrary"),
        name=name,
    )(a, w_all, g.reshape(1, d), x.reshape(m // tm, tm, d))
    return xo.reshape(m, d), ho.reshape(m, d)


def _ffn_up_kernel(h_ref, wg_ref, wu_ref, o_ref, wgb_ref, wub_ref):
    @pl.when(pl.program_id(1) == 0)
    def _():
        wgb_ref[...] = wg_ref[...].astype(BF16)
        wub_ref[...] = wu_ref[...].astype(BF16)

    for r0 in range(0, h_ref.shape[0], FFN_ROW_CHUNK):
        rows = slice(r0, r0 + FFN_ROW_CHUNK)
        h = h_ref[rows, :]
        gate = jnp.dot(h, wgb_ref[...], preferred_element_type=F32)
        up = jnp.dot(h, wub_ref[...], preferred_element_type=F32)
        o_ref[rows, :] = (jax.nn.silu(gate) * up).astype(o_ref.dtype)


def _ffn_up(h, wg_all, wu_all, layer, tm=2048, tn=512):
    m, k = h.shape
    n = wg_all.shape[2]
    wspec = pl.BlockSpec((None, k, tn), lambda j, i: (layer, 0, j))
    return pl.pallas_call(
        _ffn_up_kernel,
        grid=(n // tn, m // tm),
        in_specs=[pl.BlockSpec((tm, k), lambda j, i: (i, 0)), wspec, wspec],
        out_specs=pl.BlockSpec((tm, tn), lambda j, i: (i, j)),
        out_shape=jax.ShapeDtypeStruct((m, n), BF16),
        scratch_shapes=[pltpu.VMEM((k, tn), BF16), pltpu.VMEM((k, tn), BF16)],
        compiler_params=_params("arbitrary", "arbitrary"),
        name="ffn_up",
    )(h, wg_all, wu_all)


def kernel(x, w_in, rel_bias, lam_re, lam_im, log_dt, b_re, b_im, c_re, c_im, d_skip, w_ssm_glu_a, w_ssm_glu_b, conv_w, conv_b, conv_ln_g, conv_ln_b, w_conv_out, w_attn_out, w_out, norm_mix_g, norm_ffn_g, w_ffn_gate, w_ffn_up, w_ffn_down, norm_final_g):
    batch, seq, d_model = x.shape
    m = batch * seq
    si = seq // SLABS
    xf = x.reshape(batch, si, SLABS, d_model).transpose(0, 2, 1, 3).reshape(m, d_model)
    bands = [_band_bias(rel_bias, g) for g in range(N_GROUPS)]
    ssm_ops = _ssm_operators(lam_re, lam_im, log_dt, b_re, b_im, c_re, c_im, d_skip)
    col_scale = jnp.where(jnp.arange(w_in.shape[2]) < ATTN_QW, ATTN_SCALE, 1.0).astype(F32)[None, :]
    bf = lambda w: w.astype(BF16)
    w_attn_out, w_ssm_glu_a, w_ssm_glu_b, w_conv_out = bf(w_attn_out), bf(w_ssm_glu_a), bf(w_ssm_glu_b), bf(w_conv_out)
    w_out, w_ffn_down = bf(w_out), bf(w_ffn_down)

    h = _rmsnorm(xf, norm_mix_g[0], BF16)
    for l in range(DEPTH):
        z = _inproj(h, w_in, l, col_scale)
        attn = _attention(z, bands, batch, seq)
        y, conv = _ssm_conv(z, ssm_ops, conv_w[l], conv_b[l], l, batch, seq)
        merged = _merge(attn, y, conv, conv_ln_g[l], conv_ln_b[l], z, w_attn_out, w_ssm_glu_a, w_ssm_glu_b,
                        w_conv_out, l)
        xf, h = _residual_proj(merged, w_out, l, xf, norm_ffn_g[l], BF16, "out_proj")
        act = _ffn_up(h, w_ffn_gate, w_ffn_up, l)
        last = l == DEPTH - 1
        g_next = norm_final_g if last else norm_mix_g[l + 1]
        xf, h = _residual_proj(act, w_ffn_down, l, xf, g_next, F32 if last else BF16, "ffn_down")
    return h.reshape(batch, SLABS, si, d_model).transpose(0, 2, 1, 3).reshape(batch, seq, d_model)
```
